```python
import math
import jax
import jax.numpy as jnp
from jax import lax
import numpy as np

D_MODEL = 2048
BATCH = 4
SEQ = 2048
DEPTH = 2
DEC_BATCH = 8
DEC_SEQ = 1
PAST_LEN = 16384
PAGE_SIZE = 128

N_EVEN = (DEPTH + 1) // 2
N_ODD = DEPTH // 2

CONV_W = 4
D_RG = D_MODEL
RG_HEADS = 8
RG_BW = D_RG // RG_HEADS
RG_C = 8.0
D_SSD = D_MODEL
SSD_HEAD_DIM = 64
SSD_HEADS = D_SSD // SSD_HEAD_DIM
SSD_GROUPS = 4
SSD_STATE = 128
SSD_CHUNK = 128
D_XBC = D_SSD + 2 * SSD_GROUPS * SSD_STATE
D_CONV = D_RG + D_XBC
D_IN = D_CONV + D_RG + D_SSD + SSD_HEADS
D_MIX = D_RG + D_SSD
N_HEADS = 16
HEAD_DIM = D_MODEL // N_HEADS
MOBA_BLOCK = 256
MOBA_TOPK = 3
MOBA_QB = 16
D_FF = 4 * D_MODEL
ALPHA = (2 * DEPTH) ** 0.25
BETA = (8 * DEPTH) ** -0.25
LN_EPS = 1e-5
RMS_EPS = 1e-6

kernel_name = 'hawk_ssd_moba_deepnorm_step'


def layer_norm(x, g, b):
    xf = x.astype(jnp.float32)
    mu = jnp.mean(xf, axis=-1, keepdims=True)
    xc = xf - mu
    var = jnp.mean(xc * xc, axis=-1, keepdims=True)
    y = xc * lax.rsqrt(var + LN_EPS) * g.astype(jnp.float32) + b.astype(jnp.float32)
    return y.astype(x.dtype)


def causal_conv(u, buf, w, bias):
    l = u.shape[1]
    full = jnp.concatenate([buf.astype(u.dtype), u], axis=1)
    y = full[:, 0:l] * w[0]
    for k in range(1, CONV_W):
        y = y + full[:, k:k + l] * w[k]
    return y + bias, full[:, l:]


def _lin_combine(left, right):
    a1, b1 = left
    a2, b2 = right
    return a1 * a2, a2 * b1 + b2


def rg_lru(u, wa, ba, wx, bx, lam, h0, pos0):
    b, l, _ = u.shape
    uf = u.astype(jnp.float32)
    uh = uf.reshape(b, l, RG_HEADS, RG_BW)
    r = jax.nn.sigmoid(jnp.einsum('blhi,hij->blhj', uh, wa.astype(jnp.float32)).reshape(b, l, D_RG) + ba)
    i = jax.nn.sigmoid(jnp.einsum('blhi,hij->blhj', uh, wx.astype(jnp.float32)).reshape(b, l, D_RG) + bx)
    log_a = -RG_C * r * jax.nn.softplus(-lam.astype(jnp.float32))
    start = ((pos0 + jnp.arange(l)) == 0)[None, :, None]
    a = jnp.where(start, 0.0, jnp.exp(log_a))
    mult = jnp.where(start, 1.0, jnp.sqrt(-jnp.expm1(2.0 * log_a)))
    a_cum, b_cum = lax.associative_scan(_lin_combine, (a, mult * i * uf), axis=1)
    h = a_cum * h0.astype(jnp.float32)[:, None, :] + b_cum
    return h, h[:, -1]


def ssd_scan(x, dt, A, Bm, Cm, s0):
    b, l, h, p = x.shape
    g, n = Bm.shape[2], Bm.shape[3]
    e = h // g
    q = SSD_CHUNK if l % SSD_CHUNK == 0 else l
    c = l // q
    xr = x.reshape(b, c, q, g, e, p)
    dtr = dt.reshape(b, c, q, g, e)
    Br = Bm.reshape(b, c, q, g, n)
    Cr = Cm.reshape(b, c, q, g, n)
    cs = jnp.cumsum(dtr * A.reshape(g, e), axis=2)
    causal = jnp.tril(jnp.ones((q, q), dtype=bool))[:, :, None, None]
    seg = cs[:, :, :, None] - cs[:, :, None, :]
    lmat = jnp.exp(jnp.where(causal, seg, -jnp.inf))
    xdt = xr * dtr[..., None]
    cb = jnp.einsum('bcign,bcjgn->bcijg', Cr, Br)
    y_diag = jnp.einsum('bcijg,bcijge,bcjgep->bcigep', cb, lmat, xdt)
    decay_to_end = jnp.exp(cs[:, :, -1:] - cs)
    chunk_states = jnp.einsum('bcjgn,bcjge,bcjgep->bcgepn', Br, decay_to_end, xdt)
    chunk_decay = jnp.exp(cs[:, :, -1])

    def step(s, inp):
        st, dec = inp
        return s * dec[..., None, None] + st, s

    s_last, s_prev = lax.scan(step, s0.reshape(b, g, e, p, n),
                              (jnp.moveaxis(chunk_states, 1, 0), jnp.moveaxis(chunk_decay, 1, 0)))
    s_prev = jnp.moveaxis(s_prev, 0, 1)
    y_off = jnp.einsum('bcign,bcgepn,bcige->bcigep', Cr, s_prev, jnp.exp(cs))
    return (y_diag + y_off).reshape(b, l, h, p), s_last.reshape(b, h, p, n)


def recurrent_mixer(x, conv_buf, h0, s0, pos0, w_in, conv_w, conv_b, rg_wa, rg_ba, rg_wx, rg_bx, rg_lambda,
                    dt_bias, a_log, d_skip, norm_w, w_out):
    b, l, _ = x.shape
    f32 = jnp.float32
    proj = x @ w_in
    conv_out, new_buf = causal_conv(proj[..., :D_CONV], conv_buf, conv_w, conv_b)
    gate_rg = proj[..., D_CONV:D_CONV + D_RG].astype(f32)
    z = proj[..., D_CONV + D_RG:D_CONV + D_RG + D_SSD].astype(f32)
    dt_raw = proj[..., D_CONV + D_RG + D_SSD:].astype(f32)
    h, h_last = rg_lru(conv_out[..., :D_RG], rg_wa, rg_ba, rg_wx, rg_bx, rg_lambda, h0, pos0)
    y_rg = h * jax.nn.gelu(gate_rg)
    xbc = jax.nn.silu(conv_out[..., D_RG:].astype(f32))
    gn = SSD_GROUPS * SSD_STATE
    xs = xbc[..., :D_SSD].reshape(b, l, SSD_HEADS, SSD_HEAD_DIM)
    Bm = xbc[..., D_SSD:D_SSD + gn].reshape(b, l, SSD_GROUPS, SSD_STATE)
    Cm = xbc[..., D_SSD + gn:].reshape(b, l, SSD_GROUPS, SSD_STATE)
    dt = jax.nn.softplus(dt_raw + dt_bias.astype(f32))
    A = -jnp.exp(a_log.astype(f32))
    y, s_last = ssd_scan(xs, dt, A, Bm, Cm, s0.astype(f32))
    y = y + d_skip.astype(f32)[:, None] * xs
    y = y.reshape(b, l, D_SSD) * jax.nn.silu(z)
    yg = y.reshape(b, l, SSD_GROUPS, D_SSD // SSD_GROUPS)
    yg = yg * lax.rsqrt(jnp.mean(yg * yg, axis=-1, keepdims=True) + RMS_EPS)
    y_ssd = yg.reshape(b, l, D_SSD) * norm_w.astype(f32)
    out = jnp.concatenate([y_rg, y_ssd], axis=-1).astype(x.dtype) @ w_out
    return out, new_buf, h_last, s_last


def alibi_slopes():
    return jnp.exp2(-8.0 * jnp.arange(1, N_HEADS + 1, dtype=jnp.float32) / N_HEADS)


def moba_attend(q, k, v):
    f32 = jnp.float32
    b, lq = q.shape[0], q.shape[1]
    t = k.shape[1]
    nb = max(-(-t // MOBA_BLOCK), MOBA_TOPK)
    pad = ((0, 0), (0, (nb + 1) * MOBA_BLOCK - t), (0, 0), (0, 0))
    kp = jnp.pad(k, pad)
    vp = jnp.pad(v, pad)
    kb = kp[:, :nb * MOBA_BLOCK].reshape(b, nb, MOBA_BLOCK, N_HEADS, HEAD_DIM)
    vb = vp[:, :nb * MOBA_BLOCK].reshape(b, nb, MOBA_BLOCK, N_HEADS, HEAD_DIM)
    kmean = jnp.mean(kb.astype(f32), axis=2)
    slopes = alibi_slopes()[None, :, None, None]
    scale = HEAD_DIM ** -0.5
    qb = math.gcd(lq, MOBA_QB)
    bi = jnp.arange(b)[:, None, None, None]
    hi = jnp.arange(N_HEADS)[None, :, None, None]
    blk_off = jnp.arange(MOBA_BLOCK)
    win_off = jnp.arange(2 * MOBA_BLOCK)
    n_sel = MOBA_TOPK * MOBA_BLOCK

    def chunk(start):
        qc = lax.dynamic_slice_in_dim(q, start, qb, axis=1).astype(f32)
        qpos = t - lq + start + jnp.arange(qb)
        qblk = qpos // MOBA_BLOCK
        gate = jnp.einsum('bqhd,bnhd->bhqn', qc, kmean)
        cand = jnp.arange(nb)[None, :] < qblk[:, None]
        gate = jnp.where(cand, gate, -jnp.inf)
        _, top_idx = lax.top_k(gate, MOBA_TOPK)
        sel_ok = top_idx < qblk[:, None]
        kg = kb[bi, top_idx, :, hi].astype(f32)
        vg = vb[bi, top_idx, :, hi].astype(f32)
        kpos_sel = top_idx[..., None] * MOBA_BLOCK + blk_off
        s_sel = jnp.einsum('bqhd,bhqksd->bhqks', qc, kg) * scale
        dist_sel = (qpos[:, None, None] - kpos_sel).astype(f32)
        s_sel = jnp.where(sel_ok[..., None], s_sel - slopes[..., None] * dist_sel, -jnp.inf)
        s_sel = s_sel.reshape(b, N_HEADS, qb, n_sel)
        w0 = qblk[0] * MOBA_BLOCK
        kw = lax.dynamic_slice_in_dim(kp, w0, 2 * MOBA_BLOCK, axis=1).astype(f32)
        vw = lax.dynamic_slice_in_dim(vp, w0, 2 * MOBA_BLOCK, axis=1).astype(f32)
        kpos_w = w0 + win_off
        s_w = jnp.einsum('bqhd,bshd->bhqs', qc, kw) * scale
        dist_w = (qpos[:, None] - kpos_w[None, :]).astype(f32)
        ok_w = (kpos_w[None, :] >= (qblk * MOBA_BLOCK)[:, None]) & (kpos_w[None, :] <= qpos[:, None])
        s_w = jnp.where(ok_w, s_w - slopes * dist_w, -jnp.inf)
        prob = jax.nn.softmax(jnp.concatenate([s_sel, s_w], axis=-1), axis=-1)
        p_sel = prob[..., :n_sel].reshape(b, N_HEADS, qb, MOBA_TOPK, MOBA_BLOCK)
        p_w = prob[..., n_sel:]
        return (jnp.einsum('bhqks,bhqksd->bqhd', p_sel, vg)
                + jnp.einsum('bhqs,bshd->bqhd', p_w, vw))

    starts = jnp.arange(lq // qb) * qb
    o = lax.map(chunk, starts)
    return jnp.moveaxis(o, 0, 1).reshape(b, lq, N_HEADS * HEAD_DIM)


def moba_mixer(x, past_k, past_v, w_qkv, w_o):
    b, l, _ = x.shape
    qkv = (x @ w_qkv).reshape(b, l, 3, N_HEADS, HEAD_DIM)
    q, k, v = qkv[:, :, 0], qkv[:, :, 1], qkv[:, :, 2]
    if past_k is None:
        kf, vf = k, v
    else:
        kf = jnp.concatenate([past_k.astype(k.dtype), k], axis=1)
        vf = jnp.concatenate([past_v.astype(v.dtype), v], axis=1)
    o = moba_attend(q, kf, vf)
    return o.astype(x.dtype) @ w_o, k, v


def sq_relu_mlp(x, w_up, w_down):
    hdn = jax.nn.relu(x @ w_up)
    return (hdn * hdn) @ w_down


def trunk(x, conv_buf, rg_h, ssd_s, pos0, cache_k, cache_v, page_table, w_in, conv_w, conv_b, rg_wa, rg_ba,
          rg_wx, rg_bx, rg_lambda, ssd_dt_bias, ssd_a_log, ssd_d, ssd_norm_w, w_out, w_qkv, w_o,
          ln1_g, ln1_b, ln2_g, ln2_b, w_up, w_down):
    new_conv, new_h, new_s, new_k, new_v = [], [], [], [], []
    for layer in range(DEPTH):
        j = layer // 2
        if layer % 2 == 0:
            mix, cb, hl, sl = recurrent_mixer(x, conv_buf[j], rg_h[j], ssd_s[j], pos0, w_in[j], conv_w[j], conv_b[j],
                                              rg_wa[j], rg_ba[j], rg_wx[j], rg_bx[j], rg_lambda[j], ssd_dt_bias[j],
                                              ssd_a_log[j], ssd_d[j], ssd_norm_w[j], w_out[j])
            new_conv.append(cb)
            new_h.append(hl)
            new_s.append(sl)
        else:
            if cache_k is None:
                pk, pv = None, None
            else:
                nbat = page_table.shape[0]
                pk = cache_k[j, page_table].reshape(nbat, -1, N_HEADS, HEAD_DIM)
                pv = cache_v[j, page_table].reshape(nbat, -1, N_HEADS, HEAD_DIM)
            mix, kn, vn = moba_mixer(x, pk, pv, w_qkv[j], w_o[j])
            new_k.append(kn)
            new_v.append(vn)
        x = layer_norm(ALPHA * x + mix, ln1_g[layer], ln1_b[layer])
        x = layer_norm(ALPHA * x + sq_relu_mlp(x, w_up[layer], w_down[layer]), ln2_g[layer], ln2_b[layer])
    return x, jnp.stack(new_conv), jnp.stack(new_h), jnp.stack(new_s), jnp.stack(new_k), jnp.stack(new_v)


def setup_inputs(seed: int = 0) -> dict:
    key = jax.random.key(seed)
    ks = jax.random.split(key, 32)
    f32 = jnp.float32
    n_pages = PAST_LEN // PAGE_SIZE
    n_used = DEC_BATCH * n_pages
    n_phys = n_used + max(1, n_used // 4)

    def nrm(k, shape, scale):
        return jax.random.normal(k, shape, f32) * scale

    a0 = jax.random.uniform(ks[15], (N_EVEN, D_RG), f32, 0.9, 0.999) ** (1.0 / RG_C)
    dt0 = jnp.exp(jax.random.uniform(ks[16], (N_EVEN, SSD_HEADS), f32, math.log(1e-3), math.log(1e-1)))
    return {
        'x_prompt': nrm(ks[0], (BATCH, SEQ, D_MODEL), 1.0),
        'x_sample': nrm(ks[1], (DEC_BATCH, DEC_SEQ, D_MODEL), 1.0),
        'state_conv': nrm(ks[2], (N_EVEN, DEC_BATCH, CONV_W - 1, D_CONV), 1.0),
        'state_rglru': nrm(ks[3], (N_EVEN, DEC_BATCH, D_RG), 0.5),
        'state_ssd': nrm(ks[4], (N_EVEN, DEC_BATCH, SSD_HEADS, SSD_HEAD_DIM, SSD_STATE), 0.1),
        'cache_k': nrm(ks[5], (N_ODD, n_phys, PAGE_SIZE, N_HEADS, HEAD_DIM), 1.0),
        'cache_v': nrm(ks[6], (N_ODD, n_phys, PAGE_SIZE, N_HEADS, HEAD_DIM), 1.0),
        'page_table': jax.random.permutation(ks[7], n_phys)[:n_used].reshape(DEC_BATCH, n_pages).astype(jnp.int32),
        'w_in': nrm(ks[8], (N_EVEN, D_MODEL, D_IN), D_MODEL ** -0.5),
        'conv_w': nrm(ks[9], (N_EVEN, CONV_W, D_CONV), CONV_W ** -0.5),
        'conv_b': nrm(ks[10], (N_EVEN, D_CONV), 0.02),
        'rg_wa': nrm(ks[11], (N_EVEN, RG_HEADS, RG_BW, RG_BW), RG_BW ** -0.5),
        'rg_ba': nrm(ks[12], (N_EVEN, D_RG), 0.02),
        'rg_wx': nrm(ks[13], (N_EVEN, RG_HEADS, RG_BW, RG_BW), RG_BW ** -0.5),
        'rg_bx': nrm(ks[14], (N_EVEN, D_RG), 0.02),
        'rg_lambda': jnp.log(a0) - jnp.log1p(-a0),
        'ssd_dt_bias': dt0 + jnp.log(-jnp.expm1(-dt0)),
        'ssd_a_log': jnp.log(jax.random.uniform(ks[17], (N_EVEN, SSD_HEADS), f32, 1.0, 16.0)),
        'ssd_d': 1.0 + nrm(ks[18], (N_EVEN, SSD_HEADS), 0.1),
        'ssd_norm_w': 1.0 + nrm(ks[19], (N_EVEN, D_SSD), 0.1),
        'w_out': nrm(ks[20], (N_EVEN, D_MIX, D_MODEL), D_MIX ** -0.5 * BETA),
        'w_qkv': nrm(ks[21], (N_ODD, D_MODEL, 3 * D_MODEL), D_MODEL ** -0.5),
        'w_o': nrm(ks[22], (N_ODD, D_MODEL, D_MODEL), D_MODEL ** -0.5 * BETA),
        'ln1_g': 1.0 + nrm(ks[23], (DEPTH, D_MODEL), 0.05),
        'ln1_b': nrm(ks[24], (DEPTH, D_MODEL), 0.02),
        'ln2_g': 1.0 + nrm(ks[25], (DEPTH, D_MODEL), 0.05),
        'ln2_b': nrm(ks[26], (DEPTH, D_MODEL), 0.02),
        'w_up': nrm(ks[27], (DEPTH, D_MODEL, D_FF), D_MODEL ** -0.5),
        'w_down': nrm(ks[28], (DEPTH, D_FF, D_MODEL), D_FF ** -0.5 * BETA),
    }


def reference(x_prompt, x_sample, state_conv, state_rglru, state_ssd, cache_k, cache_v, page_table,
              w_in, conv_w, conv_b, rg_wa, rg_ba, rg_wx, rg_bx, rg_lambda, ssd_dt_bias, ssd_a_log, ssd_d,
              ssd_norm_w, w_out, w_qkv, w_o, ln1_g, ln1_b, ln2_g, ln2_b, w_up, w_down):
    bp = x_prompt.shape[0]
    zero_conv = jnp.zeros((N_EVEN, bp, CONV_W - 1, D_CONV), x_prompt.dtype)
    zero_h = jnp.zeros((N_EVEN, bp, D_RG), jnp.float32)
    zero_s = jnp.zeros((N_EVEN, bp, SSD_HEADS, SSD_HEAD_DIM, SSD_STATE), jnp.float32)
    y_prompt, conv_p, rglru_p, ssd_p, k_p, v_p = trunk(
        x_prompt, zero_conv, zero_h, zero_s, 0, None, None, None,
        w_in, conv_w, conv_b, rg_wa, rg_ba, rg_wx, rg_bx, rg_lambda, ssd_dt_bias, ssd_a_log, ssd_d,
        ssd_norm_w, w_out, w_qkv, w_o, ln1_g, ln1_b, ln2_g, ln2_b, w_up, w_down)
    past_len = page_table.shape[1] * PAGE_SIZE
    y_sample, conv_s, rglru_s, ssd_s, k_s, v_s = trunk(
        x_sample, state_conv, state_rglru, state_ssd, past_len, cache_k, cache_v, page_table,
        w_in, conv_w, conv_b, rg_wa, rg_ba, rg_wx, rg_bx, rg_lambda, ssd_dt_bias, ssd_a_log, ssd_d,
        ssd_norm_w, w_out, w_qkv, w_o, ln1_g, ln1_b, ln2_g, ln2_b, w_up, w_down)
    return (y_prompt, y_sample, conv_p, rglru_p, ssd_p, k_p, v_p, conv_s, rglru_s, ssd_s, k_s, v_s)
```

```python
import functools
import math

import jax
import jax.numpy as jnp
from jax import lax
from jax.experimental import pallas as pl
from jax.experimental.pallas import tpu as pltpu

F32 = jnp.float32
BF16 = jnp.bfloat16

D_MODEL = 2048
DEPTH = 2
PAGE_SIZE = 128
CONV_W = 4
D_RG = D_MODEL
RG_HEADS = 8
RG_BW = D_RG // RG_HEADS
RG_C = 8.0
D_SSD = D_MODEL
SSD_HEAD_DIM = 64
SSD_HEADS = D_SSD // SSD_HEAD_DIM
SSD_GROUPS = 4
SSD_STATE = 128
SSD_CHUNK = 128
SSD_GN = SSD_GROUPS * SSD_STATE
D_XBC = D_SSD + 2 * SSD_GN
D_CONV = D_RG + D_XBC
D_IN = D_CONV + D_RG + D_SSD + SSD_HEADS
D_IN_PAD = D_IN + 128 - SSD_HEADS
D_MIX = D_RG + D_SSD
N_HEADS = 16
HEAD_DIM = D_MODEL // N_HEADS
MOBA_BLOCK = 256
MOBA_TOPK = 3
D_FF = 4 * D_MODEL
ALPHA = (2 * DEPTH) ** 0.25
LN_EPS = 1e-5
RMS_EPS = 1e-6

SUBLANES = 8
LANES = 128
VMEM_LIMIT_BYTES = 56 * 1024 * 1024

NT_DIMS = (((1,), (1,)), ((), ()))
TN_DIMS = (((0,), (0,)), ((), ()))


def _params(*semantics):
    return pltpu.CompilerParams(dimension_semantics=semantics, vmem_limit_bytes=VMEM_LIMIT_BYTES)


def _bdot(a, b):
    return jnp.dot(a.astype(BF16), b.astype(BF16), preferred_element_type=F32)


def _layer_norm_rows(y, g, b):
    mu = jnp.mean(y, axis=-1, keepdims=True)
    yc = y - mu
    var = jnp.mean(yc * yc, axis=-1, keepdims=True)
    return yc * lax.rsqrt(var + LN_EPS) * g + b


def _softplus(x):
    return jnp.maximum(x, 0.0) + jnp.log1p(jnp.exp(-jnp.abs(x)))


def _sqrt_one_minus_sq(log_a, a):
    return jnp.sqrt(-jnp.tanh(log_a) * (a * a + 1.0))


def _silu(x):
    return x * jax.nn.sigmoid(x)


def _matmul_kernel(a_ref, w_ref, o_ref):
    o_ref[...] = _bdot(a_ref[...], w_ref[...]).astype(o_ref.dtype)


def matmul(a, w, *, tm, tn, col_block_offset=0, n_out=None):
    m, k = a.shape
    n = w.shape[1] if n_out is None else n_out
    return pl.pallas_call(
        _matmul_kernel,
        grid=(pl.cdiv(m, tm), pl.cdiv(n, tn)),
        in_specs=[pl.BlockSpec((tm, k), lambda i, j: (i, 0)),
                  pl.BlockSpec((k, tn), lambda i, j: (0, j + col_block_offset))],
        out_specs=pl.BlockSpec((tm, tn), lambda i, j: (i, j)),
        out_shape=jax.ShapeDtypeStruct((m, n), F32),
        compiler_params=_params("parallel", "parallel"),
        name="matmul",
    )(a, w)


def _matmul_ln_kernel(a_ref, w_ref, res_ref, g_ref, b_ref, o_ref, acc_ref):
    kk = pl.program_id(1)

    @pl.when(kk == 0)
    def _():
        acc_ref[...] = jnp.zeros_like(acc_ref)

    acc_ref[...] += _bdot(a_ref[...], w_ref[...])

    @pl.when(kk == pl.num_programs(1) - 1)
    def _():
        y = ALPHA * res_ref[...] + acc_ref[...]
        o_ref[...] = _layer_norm_rows(y, g_ref[...], b_ref[...])


def matmul_ln(a, w, res, g, b, *, tm, tk):
    m, k = a.shape
    n = w.shape[1]
    return pl.pallas_call(
        _matmul_ln_kernel,
        grid=(pl.cdiv(m, tm), k // tk),
        in_specs=[pl.BlockSpec((tm, tk), lambda i, kk: (i, kk)),
                  pl.BlockSpec((tk, n), lambda i, kk: (kk, 0)),
                  pl.BlockSpec((tm, n), lambda i, kk: (i, 0)),
                  pl.BlockSpec((1, n), lambda i, kk: (0, 0)),
                  pl.BlockSpec((1, n), lambda i, kk: (0, 0))],
        out_specs=pl.BlockSpec((tm, n), lambda i, kk: (i, 0)),
        out_shape=jax.ShapeDtypeStruct((m, n), F32),
        scratch_shapes=[pltpu.VMEM((tm, n), F32)],
        compiler_params=_params("parallel", "arbitrary"),
        name="matmul_ln",
    )(a, w, res, g.reshape(1, n), b.reshape(1, n))


def _mlp_ln_kernel(x_ref, wu_ref, wd_ref, g_ref, b_ref, o_ref, xb_ref, acc_ref):
    f = pl.program_id(1)

    @pl.when(f == 0)
    def _():
        xb_ref[...] = x_ref[...].astype(BF16)
        acc_ref[...] = jnp.zeros_like(acc_ref)

    hdn = jnp.maximum(jnp.dot(xb_ref[...], wu_ref[...], preferred_element_type=F32), 0.0)
    acc_ref[...] += _bdot(hdn * hdn, wd_ref[...])

    @pl.when(f == pl.num_programs(1) - 1)
    def _():
        y = ALPHA * x_ref[...] + acc_ref[...]
        o_ref[...] = _layer_norm_rows(y, g_ref[...], b_ref[...])


def mlp_ln(x, w_up, w_down, g, b, *, tm, tf):
    m, d = x.shape
    ff = w_up.shape[1]
    return pl.pallas_call(
        _mlp_ln_kernel,
        grid=(pl.cdiv(m, tm), ff // tf),
        in_specs=[pl.BlockSpec((tm, d), lambda i, f: (i, 0)),
                  pl.BlockSpec((d, tf), lambda i, f: (0, f)),
                  pl.BlockSpec((tf, d), lambda i, f: (f, 0)),
                  pl.BlockSpec((1, d), lambda i, f: (0, 0)),
                  pl.BlockSpec((1, d), lambda i, f: (0, 0))],
        out_specs=pl.BlockSpec((tm, d), lambda i, f: (i, 0)),
        out_shape=jax.ShapeDtypeStruct((m, d), F32),
        scratch_shapes=[pltpu.VMEM((tm, d), BF16), pltpu.VMEM((tm, d), F32)],
        compiler_params=_params("parallel", "arbitrary"),
        name="mlp_ln",
    )(x, w_up, w_down, g.reshape(1, d), b.reshape(1, d))


def _mixer_prompt_kernel(xc_ref, g0_ref, g1_ref, z0_ref, z1_ref, dt_ref,
                         convw_ref, convb_ref, wa_ref, wx_ref, ba_ref, bx_ref, lam_ref,
                         dtb_ref, alog_ref, dskip_ref, normw_ref,
                         ymix_ref, hlast_ref, state_ref,
                         buf_ref, a_scr, h_scr, y_scr):
    t = SSD_CHUNK
    c = pl.program_id(1)
    head_rows = SUBLANES

    @pl.when(c == 0)
    def _():
        buf_ref[0:head_rows, :] = jnp.zeros((head_rows, D_CONV), F32)
        hlast_ref[...] = jnp.zeros_like(hlast_ref)
        state_ref[...] = jnp.zeros_like(state_ref)

    @pl.when(c > 0)
    def _():
        buf_ref[0:head_rows, :] = buf_ref[t:t + head_rows, :]

    buf_ref[head_rows:head_rows + t, :] = xc_ref[...]

    def conv(lo, width):
        acc = convb_ref[:, lo:lo + width]
        for k in range(CONV_W):
            r0 = head_rows - (CONV_W - 1) + k
            acc = acc + convw_ref[k:k + 1, lo:lo + width] * buf_ref[r0:r0 + t, lo:lo + width]
        return acc

    row = lax.broadcasted_iota(jnp.int32, (t, RG_BW), 0)
    row_in_tile = row % SUBLANES
    is_start = jnp.logical_and(c == 0, row == 0)
    for hd in range(RG_HEADS):
        lo = hd * RG_BW
        u = conv(lo, RG_BW)
        ub = u.astype(BF16)
        r = jax.nn.sigmoid(jnp.dot(ub, wa_ref[hd], preferred_element_type=F32) + ba_ref[:, lo:lo + RG_BW])
        i = jax.nn.sigmoid(jnp.dot(ub, wx_ref[hd], preferred_element_type=F32) + bx_ref[:, lo:lo + RG_BW])
        log_a = -RG_C * r * _softplus(-lam_ref[:, lo:lo + RG_BW])
        a_raw = jnp.exp(log_a)
        a = jnp.where(is_start, 0.0, a_raw)
        mult = jnp.where(is_start, 1.0, _sqrt_one_minus_sq(log_a, a_raw))
        bt = mult * i * u
        for s in (1, 2, 4):
            keep = row_in_tile >= s
            a_sh = pltpu.roll(a, s, axis=0)
            b_sh = pltpu.roll(bt, s, axis=0)
            bt = jnp.where(keep, a * b_sh + bt, bt)
            a = jnp.where(keep, a * a_sh, a)
        a_scr[:, lo:lo + RG_BW] = a
        h_scr[:, lo:lo + RG_BW] = bt

    def carry_tile(gi, carry):
        r0 = pl.multiple_of(gi * SUBLANES, SUBLANES)
        h = a_scr[pl.ds(r0, SUBLANES), :] * carry + h_scr[pl.ds(r0, SUBLANES), :]
        h_scr[pl.ds(r0, SUBLANES), :] = h
        return h[SUBLANES - 1:SUBLANES, :]

    hlast_ref[...] = lax.fori_loop(0, t // SUBLANES, carry_tile, hlast_ref[...])
    half = D_RG // 2
    ymix_ref[:, 0:half] = (h_scr[:, 0:half] * jax.nn.gelu(g0_ref[...])).astype(BF16)
    ymix_ref[:, half:D_RG] = (h_scr[:, half:D_RG] * jax.nn.gelu(g1_ref[...])).astype(BF16)

    lane = lax.broadcasted_iota(jnp.int32, (t, LANES), 1)
    rows = lax.broadcasted_iota(jnp.int32, (t, LANES), 0)
    dt = jnp.where(lane < SSD_HEADS, _softplus(dt_ref[...] + dtb_ref[...]), 0.0)
    cs = dt * (-jnp.exp(alog_ref[...]))
    s = 1
    while s < t:
        cs = cs + jnp.where(rows >= s, pltpu.roll(cs, s, axis=0), 0.0)
        s *= 2
    cs_t = cs.T
    cs_last = cs[t - 1:t, :]
    decay_to_end = jnp.exp(cs_last - cs)
    exp_cs = jnp.exp(cs)
    chunk_decay = jnp.exp(cs_last)

    ii = lax.broadcasted_iota(jnp.int32, (t, t), 0)
    jj = lax.broadcasted_iota(jnp.int32, (t, t), 1)
    causal = jj <= ii
    first_head = lane < SSD_HEAD_DIM
    first_head_rows = rows < SSD_HEAD_DIM

    for g in range(SSD_GROUPS):
        bmat = _silu(conv(D_RG + D_SSD + g * SSD_STATE, SSD_STATE)).astype(BF16)
        cmat = _silu(conv(D_RG + D_SSD + SSD_GN + g * SSD_STATE, SSD_STATE)).astype(BF16)
        cb = lax.dot_general(cmat, bmat, NT_DIMS, preferred_element_type=F32)
        for pair in range(SSD_HEADS // SSD_GROUPS // 2):
            h0 = g * (SSD_HEADS // SSD_GROUPS) + 2 * pair
            lo = h0 * SSD_HEAD_DIM

            def per_head(x2d):
                return jnp.where(first_head, x2d[:, h0:h0 + 1], x2d[:, h0 + 1:h0 + 2])

            xs = _silu(conv(D_RG + lo, LANES))
            xdt = xs * per_head(dt)
            xdt_b = xdt.astype(BF16)
            y_heads = []
            for hh in (h0, h0 + 1):
                seg = cs[:, hh:hh + 1] - cs_t[hh:hh + 1, :]
                lmat = jnp.exp(jnp.where(causal, seg, -jnp.inf))
                y_heads.append(jnp.dot((cb * lmat).astype(BF16), xdt_b, preferred_element_type=F32))
            y = jnp.where(first_head, y_heads[0], y_heads[1])
            s_prev = state_ref[lo:lo + LANES, :]
            y = y + lax.dot_general(cmat, s_prev.astype(BF16), NT_DIMS,
                                    preferred_element_type=F32) * per_head(exp_cs)
            y_scr[:, lo:lo + LANES] = y + dskip_ref[:, lo:lo + LANES] * xs
            decay_rows = jnp.where(first_head_rows, chunk_decay[:, h0:h0 + 1], chunk_decay[:, h0 + 1:h0 + 2])
            state_ref[lo:lo + LANES, :] = s_prev * decay_rows + lax.dot_general(
                (xdt * per_head(decay_to_end)).astype(BF16), bmat, TN_DIMS, preferred_element_type=F32)

    gw = D_SSD // SSD_GROUPS
    for g in range(SSD_GROUPS):
        lo = g * gw
        z_ref = z0_ref if g < SSD_GROUPS // 2 else z1_ref
        zlo = lo % (D_SSD // 2)
        yz = y_scr[:, lo:lo + gw] * _silu(z_ref[:, zlo:zlo + gw])
        ms = jnp.mean(yz * yz, axis=-1, keepdims=True)
        ymix_ref[:, D_RG + lo:D_RG + lo + gw] = (yz * lax.rsqrt(ms + RMS_EPS)
                                                 * normw_ref[:, lo:lo + gw]).astype(BF16)


def _pad_lanes(v):
    return jnp.zeros((1, LANES), F32).at[0, :v.shape[0]].set(v)


def mixer_prompt(proj, nb, conv_w, conv_b, wa, wx, ba, bx, lam, dt_bias, a_log, d_skip, norm_w):
    m = proj.shape[0]
    l = m // nb
    t = SSD_CHUNK
    nc = l // t
    wide = 1024
    row_blk = lambda bi, ci: bi * nc + ci
    full = lambda shape: pl.BlockSpec(shape, lambda bi, ci: (0,) * len(shape))
    gate_blk = D_CONV // wide
    in_specs = [pl.BlockSpec((t, D_CONV), lambda bi, ci: (row_blk(bi, ci), 0))]
    in_specs += [pl.BlockSpec((t, wide), functools.partial(lambda j, bi, ci: (bi * nc + ci, j), gate_blk + j))
                 for j in range(4)]
    in_specs += [pl.BlockSpec((t, LANES), lambda bi, ci: (row_blk(bi, ci), (D_IN - SSD_HEADS) // LANES))]
    in_specs += [full((CONV_W, D_CONV)), full((1, D_CONV)),
                 full((RG_HEADS, RG_BW, RG_BW)), full((RG_HEADS, RG_BW, RG_BW)),
                 full((1, D_RG)), full((1, D_RG)), full((1, D_RG)),
                 full((1, LANES)), full((1, LANES)), full((1, D_SSD)), full((1, D_SSD))]
    ymix, hlast, state = pl.pallas_call(
        _mixer_prompt_kernel,
        grid=(nb, nc),
        in_specs=in_specs,
        out_specs=[pl.BlockSpec((t, D_MIX), lambda bi, ci: (row_blk(bi, ci), 0)),
                   pl.BlockSpec((None, 1, D_RG), lambda bi, ci: (bi, 0, 0)),
                   pl.BlockSpec((None, D_SSD, SSD_STATE), lambda bi, ci: (bi, 0, 0))],
        out_shape=[jax.ShapeDtypeStruct((m, D_MIX), BF16),
                   jax.ShapeDtypeStruct((nb, 1, D_RG), F32),
                   jax.ShapeDtypeStruct((nb, D_SSD, SSD_STATE), F32)],
        scratch_shapes=[pltpu.VMEM((t + 2 * SUBLANES, D_CONV), F32),
                        pltpu.VMEM((t, D_RG), F32), pltpu.VMEM((t, D_RG), F32),
                        pltpu.VMEM((t, D_SSD), F32)],
        compiler_params=_params("parallel", "arbitrary"),
        name="mixer_prompt",
    )(proj, proj, proj, proj, proj, proj,
      conv_w, conv_b.reshape(1, D_CONV), wa.astype(BF16), wx.astype(BF16),
      ba.reshape(1, D_RG), bx.reshape(1, D_RG), lam.reshape(1, D_RG),
      _pad_lanes(dt_bias), _pad_lanes(a_log),
      jnp.repeat(d_skip, SSD_HEAD_DIM).reshape(1, D_SSD), norm_w.reshape(1, D_SSD))
    return ymix, hlast.reshape(nb, D_RG), state


def _moba_prompt_kernel(slopes_ref, q_ref, k_ref, v_ref, o_ref, kmean_ref, gate_ref):
    blk = MOBA_BLOCK
    nblk = k_ref.shape[0] // blk
    hd = pl.program_id(1)
    qi = pl.program_id(2)
    slope = slopes_ref[hd]
    scale = HEAD_DIM ** -0.5

    @pl.when(qi == 0)
    def _():
        for n in range(nblk):
            kmean_ref[n:n + 1, :] = jnp.mean(k_ref[n * blk:(n + 1) * blk, :], axis=0, keepdims=True)

    q = q_ref[...]
    qb = q.astype(BF16)
    for n in range(nblk):
        gate_n = jnp.sum(q * kmean_ref[n:n + 1, :], axis=-1, keepdims=True)
        gate_ref[n] = jnp.where(n < qi, gate_n, -jnp.inf)

    r_minus_c = (lax.broadcasted_iota(jnp.int32, (blk, blk), 0)
                 - lax.broadcasted_iota(jnp.int32, (blk, blk), 1))

    def scores(j):
        k0 = pl.multiple_of(j * blk, blk)
        kj = k_ref[pl.ds(k0, blk), :].astype(BF16)
        dist = ((qi - j) * blk + r_minus_c).astype(F32)
        return lax.dot_general(qb, kj, NT_DIMS, preferred_element_type=F32) * scale - slope * dist, k0

    s_own, k0 = scores(qi)
    s_own = jnp.where(r_minus_c >= 0, s_own, -jnp.inf)
    m0 = jnp.max(s_own, axis=-1, keepdims=True)
    p0 = jnp.exp(s_own - m0)
    l0 = jnp.sum(p0, axis=-1, keepdims=True)
    acc0 = _bdot(p0, v_ref[pl.ds(k0, blk), :])

    def past_block(j, carry):
        m_run, l_run, acc = carry
        gate_j = gate_ref[j]
        ahead = jnp.zeros(gate_j.shape, jnp.int32)
        for n in range(nblk):
            gate_n = gate_ref[n]
            before = jnp.logical_or(gate_n > gate_j, jnp.logical_and(gate_n == gate_j, n < j))
            ahead = ahead + before.astype(jnp.int32)
        selected = ahead < MOBA_TOPK
        s_j, k0 = scores(j)
        s_j = jnp.where(selected, s_j, -jnp.inf)
        m_new = jnp.maximum(m_run, jnp.max(s_j, axis=-1, keepdims=True))
        alpha = jnp.exp(m_run - m_new)
        p = jnp.exp(s_j - m_new)
        l_new = alpha * l_run + jnp.sum(p, axis=-1, keepdims=True)
        acc_new = alpha * acc + _bdot(p, v_ref[pl.ds(k0, blk), :])
        return m_new, l_new, acc_new

    _, l_fin, acc_fin = lax.fori_loop(0, qi, past_block, (m0, l0, acc0))
    o_ref[...] = (acc_fin / l_fin).astype(o_ref.dtype)


def alibi_slopes():
    return jnp.exp2(-8.0 * jnp.arange(1, N_HEADS + 1, dtype=F32) / N_HEADS)


def moba_prompt(q, k, v, nb):
    m = q.shape[0]
    l = m // nb
    nq = l // MOBA_BLOCK
    return pl.pallas_call(
        _moba_prompt_kernel,
        grid=(nb, N_HEADS, nq),
        in_specs=[pl.BlockSpec(memory_space=pltpu.SMEM),
                  pl.BlockSpec((MOBA_BLOCK, HEAD_DIM), lambda bi, hi, qi: (bi * nq + qi, hi)),
                  pl.BlockSpec((l, HEAD_DIM), lambda bi, hi, qi: (bi, hi)),
                  pl.BlockSpec((l, HEAD_DIM), lambda bi, hi, qi: (bi, hi))],
        out_specs=pl.BlockSpec((MOBA_BLOCK, HEAD_DIM), lambda bi, hi, qi: (bi * nq + qi, hi)),
        out_shape=jax.ShapeDtypeStruct((m, D_MODEL), BF16),
        scratch_shapes=[pltpu.VMEM((nq, HEAD_DIM), F32), pltpu.VMEM((nq, MOBA_BLOCK, 1), F32)],
        compiler_params=_params("parallel", "parallel", "arbitrary"),
        name="moba_prompt",
    )(alibi_slopes(), q, k, v)


def _mixer_step_a_kernel(proj_ref, b0_ref, b1_ref, b2_ref, h0_ref,
                         convw_ref, convb_ref, wa_ref, wx_ref, ba_ref, bx_ref, lam_ref, dtb_ref,
                         yrg_ref, hnew_ref, xbc_ref, dt_ref):
    conv = (convb_ref[...] + convw_ref[0:1, :] * b0_ref[...] + convw_ref[1:2, :] * b1_ref[...]
            + convw_ref[2:3, :] * b2_ref[...] + convw_ref[3:4, :] * proj_ref[:, 0:D_CONV])
    xbc_ref[...] = _silu(conv[:, D_RG:])
    dt_ref[...] = _softplus(proj_ref[:, D_IN - SSD_HEADS:D_IN] + dtb_ref[...])
    for hd in range(RG_HEADS):
        lo = hd * RG_BW
        u = conv[:, lo:lo + RG_BW]
        ub = u.astype(BF16)
        r = jax.nn.sigmoid(jnp.dot(ub, wa_ref[hd], preferred_element_type=F32) + ba_ref[:, lo:lo + RG_BW])
        i = jax.nn.sigmoid(jnp.dot(ub, wx_ref[hd], preferred_element_type=F32) + bx_ref[:, lo:lo + RG_BW])
        log_a = -RG_C * r * _softplus(-lam_ref[:, lo:lo + RG_BW])
        a = jnp.exp(log_a)
        h = a * h0_ref[:, lo:lo + RG_BW] + _sqrt_one_minus_sq(log_a, a) * i * u
        hnew_ref[:, lo:lo + RG_BW] = h
        gate = proj_ref[:, D_CONV + lo:D_CONV + lo + RG_BW]
        yrg_ref[:, lo:lo + RG_BW] = h * jax.nn.gelu(gate)


def _ssd_step_kernel(s0_ref, xs_ref, dt_ref, alog_ref, b_ref, c_ref, snew_ref, y_ref):
    dt = dt_ref[...]
    decay = jnp.exp(dt * (-jnp.exp(alog_ref[...])))
    s_new = s0_ref[...] * decay + (xs_ref[...] * dt) * b_ref[...]
    snew_ref[...] = s_new
    y_ref[...] = jnp.sum(s_new * c_ref[...], axis=-1, keepdims=True)


def _mixer_step_b_kernel(yrg_ref, y_ref, xs_ref, z_ref, dskip_ref, normw_ref, o_ref):
    o_ref[:, 0:D_RG] = yrg_ref[...].astype(BF16)
    yz = (y_ref[...] + dskip_ref[...] * xs_ref[...]) * _silu(z_ref[...])
    gw = D_SSD // SSD_GROUPS
    for g in range(SSD_GROUPS):
        blk = yz[:, g * gw:(g + 1) * gw]
        ms = jnp.mean(blk * blk, axis=-1, keepdims=True)
        o_ref[:, D_RG + g * gw:D_RG + (g + 1) * gw] = (
            blk * lax.rsqrt(ms + RMS_EPS) * normw_ref[:, g * gw:(g + 1) * gw]).astype(BF16)


def _whole(shape):
    return pl.BlockSpec(shape, lambda *_: (0,) * len(shape))


def mixer_sample(proj, conv_buf, h0, s0, conv_w, conv_b, wa, wx, ba, bx, lam, dt_bias, a_log, d_skip, norm_w):
    nb = proj.shape[0]
    hpg = SSD_HEADS // SSD_GROUPS
    a_ins = (proj, conv_buf[:, 0], conv_buf[:, 1], conv_buf[:, 2], h0,
             conv_w, conv_b.reshape(1, D_CONV), wa.astype(BF16), wx.astype(BF16),
             ba.reshape(1, D_RG), bx.reshape(1, D_RG), lam.reshape(1, D_RG), dt_bias.reshape(1, SSD_HEADS))
    yrg, hnew, xbc, dt = pl.pallas_call(
        _mixer_step_a_kernel,
        grid=(1,),
        in_specs=[_whole(a.shape) for a in a_ins],
        out_specs=[_whole((nb, D_RG)), _whole((nb, D_RG)), _whole((nb, D_XBC)), _whole((nb, SSD_HEADS))],
        out_shape=[jax.ShapeDtypeStruct((nb, D_RG), F32), jax.ShapeDtypeStruct((nb, D_RG), F32),
                   jax.ShapeDtypeStruct((nb, D_XBC), F32), jax.ShapeDtypeStruct((nb, SSD_HEADS), F32)],
        compiler_params=_params("arbitrary"),
        name="mixer_step_a",
    )(*a_ins)
    xs = xbc[:, :D_SSD]
    snew, y = pl.pallas_call(
        _ssd_step_kernel,
        grid=(nb, SSD_GROUPS),
        in_specs=[pl.BlockSpec((None, hpg, SSD_HEAD_DIM, SSD_STATE), lambda bi, gi: (bi, gi, 0, 0)),
                  pl.BlockSpec((None, hpg, SSD_HEAD_DIM, 1), lambda bi, gi: (bi, gi, 0, 0)),
                  pl.BlockSpec((None, hpg, 1, 1), lambda bi, gi: (bi, gi, 0, 0)),
                  pl.BlockSpec((hpg, 1, 1), lambda bi, gi: (gi, 0, 0)),
                  pl.BlockSpec((None, None, 1, SSD_STATE), lambda bi, gi: (bi, gi, 0, 0)),
                  pl.BlockSpec((None, None, 1, SSD_STATE), lambda bi, gi: (bi, gi, 0, 0))],
        out_specs=[pl.BlockSpec((None, hpg, SSD_HEAD_DIM, SSD_STATE), lambda bi, gi: (bi, gi, 0, 0)),
                   pl.BlockSpec((None, hpg, SSD_HEAD_DIM, 1), lambda bi, gi: (bi, gi, 0, 0))],
        out_shape=[jax.ShapeDtypeStruct((nb, SSD_HEADS, SSD_HEAD_DIM, SSD_STATE), F32),
                   jax.ShapeDtypeStruct((nb, SSD_HEADS, SSD_HEAD_DIM, 1), F32)],
        compiler_params=_params("parallel", "parallel"),
        name="ssd_step",
    )(s0, xs.reshape(nb, SSD_HEADS, SSD_HEAD_DIM, 1), dt.reshape(nb, SSD_HEADS, 1, 1),
      a_log.reshape(SSD_HEADS, 1, 1),
      xbc[:, D_SSD:D_SSD + SSD_GN].reshape(nb, SSD_GROUPS, 1, SSD_STATE),
      xbc[:, D_SSD + SSD_GN:].reshape(nb, SSD_GROUPS, 1, SSD_STATE))
    b_ins = (yrg, y.reshape(nb, D_SSD), xs, proj[:, D_CONV + D_RG:D_CONV + D_RG + D_SSD],
             jnp.repeat(d_skip, SSD_HEAD_DIM).reshape(1, D_SSD), norm_w.reshape(1, D_SSD))
    ymix = pl.pallas_call(
        _mixer_step_b_kernel,
        grid=(1,),
        in_specs=[_whole(a.shape) for a in b_ins],
        out_specs=_whole((nb, D_MIX)),
        out_shape=jax.ShapeDtypeStruct((nb, D_MIX), BF16),
        compiler_params=_params("arbitrary"),
        name="mixer_step_b",
    )(*b_ins)
    return ymix, hnew, snew


def _block_ksum_kernel(pt_ref, k0_ref, k1_ref, o_ref):
    o_ref[...] = (jnp.sum(k0_ref[...], axis=0, keepdims=True)
                  + jnp.sum(k1_ref[...], axis=0, keepdims=True))


def block_key_sums(cache_k, page_table):
    nb, n_pages = page_table.shape
    ppb = MOBA_BLOCK // PAGE_SIZE
    nblk = n_pages // ppb
    return pl.pallas_call(
        _block_ksum_kernel,
        grid_spec=pltpu.PrefetchScalarGridSpec(
            num_scalar_prefetch=1,
            grid=(nb, nblk),
            in_specs=[pl.BlockSpec((None, PAGE_SIZE, D_MODEL), lambda bi, ni, pt: (pt[bi, ppb * ni], 0, 0)),
                      pl.BlockSpec((None, PAGE_SIZE, D_MODEL), lambda bi, ni, pt: (pt[bi, ppb * ni + 1], 0, 0))],
            out_specs=pl.BlockSpec((None, None, 1, D_MODEL), lambda bi, ni, pt: (bi, ni, 0, 0))),
        out_shape=jax.ShapeDtypeStruct((nb, nblk, 1, D_MODEL), F32),
        compiler_params=_params("parallel", "parallel"),
        name="block_key_sums",
    )(page_table, cache_k, cache_k)


def _moba_topk_kernel(q_ref, ksum_ref, idx_ref):
    nblk = ksum_ref.shape[0]
    prod = ksum_ref[...] * (q_ref[...] * (1.0 / MOBA_BLOCK))
    lane = lax.broadcasted_iota(jnp.int32, (nblk, LANES), 1)
    gate = jnp.full((nblk, LANES), -jnp.inf, F32)
    for hd in range(N_HEADS):
        g_h = jnp.sum(prod[:, hd * HEAD_DIM:(hd + 1) * HEAD_DIM], axis=-1, keepdims=True)
        gate = jnp.where(lane == hd, g_h, gate)
    blk_id = lax.broadcasted_iota(jnp.int32, (nblk, LANES), 0)
    out_row = lax.broadcasted_iota(jnp.int32, (SUBLANES, LANES), 0)
    out = jnp.zeros((SUBLANES, LANES), jnp.int32)
    for kk in range(MOBA_TOPK):
        best = jnp.max(gate, axis=0, keepdims=True)
        pick = jnp.min(jnp.where(gate == best, blk_id, nblk), axis=0, keepdims=True)
        out = jnp.where(out_row == kk, pick, out)
        gate = jnp.where(blk_id == pick, -jnp.inf, gate)
    idx_ref[...] = out


def moba_topk(q, ksum):
    nb = q.shape[0]
    nblk = ksum.shape[1]
    return pl.pallas_call(
        _moba_topk_kernel,
        grid=(nb,),
        in_specs=[pl.BlockSpec((None, 1, D_MODEL), lambda bi: (bi, 0, 0)),
                  pl.BlockSpec((None, nblk, D_MODEL), lambda bi: (bi, 0, 0))],
        out_specs=pl.BlockSpec((None, SUBLANES, LANES), lambda bi: (bi, 0, 0)),
        out_shape=jax.ShapeDtypeStruct((nb, SUBLANES, LANES), jnp.int32),
        compiler_params=_params("parallel"),
        name="moba_topk",
    )(q.reshape(nb, 1, D_MODEL), ksum.reshape(nb, nblk, D_MODEL))


def _moba_decode_kernel(pt_ref, top_ref, slopes_ref, q_ref, kn_ref, vn_ref,
                        k0_ref, k1_ref, v0_ref, v1_ref, o_ref, m_ref, l_ref, acc_ref, *, past_len):
    hd = pl.program_id(0)
    bi = pl.program_id(1)
    kk = pl.program_id(2)
    scale = HEAD_DIM ** -0.5
    slope = slopes_ref[hd]
    q = q_ref[pl.ds(bi, 1), :]

    @pl.when(kk == 0)
    def _():
        m_ref[...] = jnp.sum(q * kn_ref[pl.ds(bi, 1), :], axis=-1, keepdims=True) * scale
        l_ref[...] = jnp.ones_like(l_ref)
        acc_ref[...] = vn_ref[pl.ds(bi, 1), :]

    blk_id = top_ref[bi, hd, kk]
    q8 = jnp.broadcast_to(q, (SUBLANES, HEAD_DIM)).astype(BF16)
    kpos0 = blk_id * MOBA_BLOCK
    m_run = m_ref[...]
    l_run = l_ref[...]
    acc = acc_ref[...]
    for half, (k_ref, v_ref) in enumerate(((k0_ref, v0_ref), (k1_ref, v1_ref))):
        kpos = kpos0 + half * PAGE_SIZE + lax.broadcasted_iota(jnp.int32, (1, PAGE_SIZE), 1)
        dist = (past_len - kpos).astype(F32)
        s = lax.dot_general(q8, k_ref[...].astype(BF16), NT_DIMS, preferred_element_type=F32)[0:1, :]
        s = s * scale - slope * dist
        m_new = jnp.maximum(m_run, jnp.max(s, axis=-1, keepdims=True))
        alpha = jnp.exp(m_run - m_new)
        p = jnp.exp(s - m_new)
        l_run = alpha * l_run + jnp.sum(p, axis=-1, keepdims=True)
        p8 = jnp.broadcast_to(p, (SUBLANES, PAGE_SIZE))
        acc = alpha * acc + _bdot(p8, v_ref[...])[0:1, :]
        m_run = m_new
    m_ref[...] = m_run
    l_ref[...] = l_run
    acc_ref[...] = acc

    @pl.when(kk == pl.num_programs(2) - 1)
    def _():
        o_ref[pl.ds(bi, 1), :] = (acc / l_run).astype(o_ref.dtype)


def moba_decode(q, k_new, v_new, cache_k, cache_v, page_table, top_idx):
    nb, n_pages = page_table.shape
    ppb = MOBA_BLOCK // PAGE_SIZE
    past_len = n_pages * PAGE_SIZE

    def page_spec(half):
        return pl.BlockSpec((None, PAGE_SIZE, HEAD_DIM),
                            lambda hi, bi, kk, pt, top: (pt[bi, ppb * top[bi, hi, kk] + half], 0, hi))

    row_spec = pl.BlockSpec((nb, HEAD_DIM), lambda hi, bi, kk, pt, top: (0, hi))
    return pl.pallas_call(
        functools.partial(_moba_decode_kernel, past_len=past_len),
        grid_spec=pltpu.PrefetchScalarGridSpec(
            num_scalar_prefetch=2,
            grid=(N_HEADS, nb, MOBA_TOPK),
            in_specs=[pl.BlockSpec(memory_space=pltpu.SMEM), row_spec, row_spec, row_spec,
                      page_spec(0), page_spec(1), page_spec(0), page_spec(1)],
            out_specs=row_spec,
            scratch_shapes=[pltpu.VMEM((1, 1), F32), pltpu.VMEM((1, 1), F32), pltpu.VMEM((1, HEAD_DIM), F32)]),
        out_shape=jax.ShapeDtypeStruct((nb, D_MODEL), F32),
        compiler_params=_params("parallel", "arbitrary", "arbitrary"),
        name="moba_decode",
    )(page_table, top_idx, alibi_slopes(), q, k_new, v_new, cache_k, cache_k, cache_v, cache_v)


def _trunk_prompt(x, w, nb):
    tm = 512
    proj = matmul(x, w["w_in"], tm=tm, tn=1024)
    l = x.shape[0] // nb
    conv_tail = proj.reshape(nb, l, D_IN_PAD)[:, l - (CONV_W - 1):, :D_CONV]
    ymix, h_last, s_last = mixer_prompt(proj, nb, *w["mixer"])
    x = matmul_ln(ymix, w["w_out"], x, w["ln1_g"][0], w["ln1_b"][0], tm=tm, tk=2048)
    x = mlp_ln(x, w["w_up"][0], w["w_down"][0], w["ln2_g"][0], w["ln2_b"][0], tm=tm, tf=1024)
    nqb = D_MODEL // 1024
    q, k, v = (matmul(x, w["w_qkv"], tm=tm, tn=1024, col_block_offset=i * nqb, n_out=D_MODEL) for i in range(3))
    o = moba_prompt(q, k, v, nb)
    x = matmul_ln(o, w["w_o"], x, w["ln1_g"][1], w["ln1_b"][1], tm=tm, tk=2048)
    x = mlp_ln(x, w["w_up"][1], w["w_down"][1], w["ln2_g"][1], w["ln2_b"][1], tm=tm, tf=1024)
    return x, conv_tail, h_last, s_last, k, v


def _trunk_sample(x, conv_buf, h0, s0, cache_k, cache_v, page_table, w):
    nb = x.shape[0]
    proj = matmul(x, w["w_in"], tm=nb, tn=1024)
    new_buf = jnp.concatenate([conv_buf[:, 1:], proj[:, None, :D_CONV]], axis=1)
    ymix, h_new, s_new = mixer_sample(proj, conv_buf, h0, s0, *w["mixer"])
    x = matmul_ln(ymix, w["w_out"], x, w["ln1_g"][0], w["ln1_b"][0], tm=nb, tk=2048)
    x = mlp_ln(x, w["w_up"][0], w["w_down"][0], w["ln2_g"][0], w["ln2_b"][0], tm=nb, tf=1024)
    qkv = matmul(x, w["w_qkv"], tm=nb, tn=1024)
    q, k, v = qkv[:, :D_MODEL], qkv[:, D_MODEL:2 * D_MODEL], qkv[:, 2 * D_MODEL:]
    ksum = block_key_sums(cache_k, page_table)
    top = moba_topk(q, ksum)
    top_idx = jnp.transpose(top[:, :MOBA_TOPK, :N_HEADS], (0, 2, 1))
    o = moba_decode(q, k, v, cache_k, cache_v, page_table, top_idx)
    x = matmul_ln(o, w["w_o"], x, w["ln1_g"][1], w["ln1_b"][1], tm=nb, tk=2048)
    x = mlp_ln(x, w["w_up"][1], w["w_down"][1], w["ln2_g"][1], w["ln2_b"][1], tm=nb, tf=1024)
    return x, new_buf, h_new, s_new, k, v


def kernel(x_prompt, x_sample, state_conv, state_rglru, state_ssd, cache_k, cache_v, page_table, w_in, conv_w, conv_b, rg_wa, rg_ba, rg_wx, rg_bx, rg_lambda, ssd_dt_bias, ssd_a_log, ssd_d, ssd_norm_w, w_out, w_qkv, w_o, ln1_g, ln1_b, ln2_g, ln2_b, w_up, w_down):
    bp, lp, _ = x_prompt.shape
    bs = x_sample.shape[0]
    w = {
        "w_in": jnp.pad(w_in[0].astype(BF16), ((0, 0), (0, D_IN_PAD - D_IN))), "w_out": w_out[0].astype(BF16),
        "w_qkv": w_qkv[0].astype(BF16), "w_o": w_o[0].astype(BF16),
        "w_up": w_up.astype(BF16), "w_down": w_down.astype(BF16),
        "ln1_g": ln1_g, "ln1_b": ln1_b, "ln2_g": ln2_g, "ln2_b": ln2_b,
        "mixer": (conv_w[0], conv_b[0], rg_wa[0], rg_wx[0], rg_ba[0], rg_bx[0], rg_lambda[0],
                  ssd_dt_bias[0], ssd_a_log[0], ssd_d[0], ssd_norm_w[0]),
    }
    yp, conv_p, h_p, s_p, k_p, v_p = _trunk_prompt(x_prompt.reshape(bp * lp, D_MODEL), w, bp)
    n_phys = cache_k.shape[1]
    ys, conv_s, h_s, s_s, k_s, v_s = _trunk_sample(
        x_sample.reshape(bs, D_MODEL), state_conv[0], state_rglru[0], state_ssd[0],
        cache_k[0].reshape(n_phys, PAGE_SIZE, D_MODEL), cache_v[0].reshape(n_phys, PAGE_SIZE, D_MODEL),
        page_table, w)
    return (yp.reshape(bp, lp, D_MODEL), ys.reshape(bs, 1, D_MODEL),
            conv_p[None], h_p[None], s_p.reshape(1, bp, SSD_HEADS, SSD_HEAD_DIM, SSD_STATE),
            k_p.reshape(1, bp, lp, N_HEADS, HEAD_DIM), v_p.reshape(1, bp, lp, N_HEADS, HEAD_DIM),
            conv_s[None], h_s[None], s_s[None],
            k_s.reshape(1, bs, 1, N_HEADS, HEAD_DIM), v_s.reshape(1, bs, 1, N_HEADS, HEAD_DIM))
```

```python
import functools
import math

import jax
import jax.numpy as jnp
from jax import lax
from jax.experimental import pallas as pl
from jax.experimental.pallas import tpu as pltpu

F32 = jnp.float32
BF16 = jnp.bfloat16

D_MODEL = 2048
DEPTH = 2
PAGE_SIZE = 128
CONV_W = 4
D_RG = D_MODEL
RG_HEADS = 8
RG_BW = D_RG // RG_HEADS
RG_C = 8.0
D_SSD = D_MODEL
SSD_HEAD_DIM = 64
SSD_HEADS = D_SSD // SSD_HEAD_DIM
SSD_GROUPS = 4
SSD_STATE = 128
SSD_CHUNK = 128
SSD_GN = SSD_GROUPS * SSD_STATE
D_XBC = D_SSD + 2 * SSD_GN
D_CONV = D_RG + D_XBC
D_IN = D_CONV + D_RG + D_SSD + SSD_HEADS
D_IN_PAD = D_IN + 128 - SSD_HEADS
D_MIX = D_RG + D_SSD
N_HEADS = 16
HEAD_DIM = D_MODEL // N_HEADS
MOBA_BLOCK = 256
MOBA_TOPK = 3
D_FF = 4 * D_MODEL
ALPHA = (2 * DEPTH) ** 0.25
LN_EPS = 1e-5
RMS_EPS = 1e-6

SUBLANES = 8
LANES = 128
VMEM_LIMIT_BYTES = 56 * 1024 * 1024

NT_DIMS = (((1,), (1,)), ((), ()))
TN_DIMS = (((0,), (0,)), ((), ()))


def _params(*semantics):
    return pltpu.CompilerParams(dimension_semantics=semantics, vmem_limit_bytes=VMEM_LIMIT_BYTES)


def _bdot(a, b):
    return jnp.dot(a.astype(BF16), b.astype(BF16), preferred_element_type=F32)


def _layer_norm_rows(y, g, b):
    mu = jnp.mean(y, axis=-1, keepdims=True)
    yc = y - mu
    var = jnp.mean(yc * yc, axis=-1, keepdims=True)
    return yc * lax.rsqrt(var + LN_EPS) * g + b


def _softplus(x):
    return jnp.maximum(x, 0.0) + jnp.log1p(jnp.exp(-jnp.abs(x)))


def _sqrt_one_minus_sq(log_a, a):
    return jnp.sqrt(-jnp.tanh(log_a) * (a * a + 1.0))


def _silu(x):
    return x * jax.nn.sigmoid(x)


def _matmul_kernel(a_ref, w_ref, o_ref):
    o_ref[...] = _bdot(a_ref[...], w_ref[...]).astype(o_ref.dtype)


def matmul(a, w, *, tm, tn):
    m, k = a.shape
    n = w.shape[1]
    return pl.pallas_call(
        _matmul_kernel,
        grid=(pl.cdiv(m, tm), pl.cdiv(n, tn)),
        in_specs=[pl.BlockSpec((tm, k), lambda i, j: (i, 0)),
                  pl.BlockSpec((k, tn), lambda i, j: (0, j))],
        out_specs=pl.BlockSpec((tm, tn), lambda i, j: (i, j)),
        out_shape=jax.ShapeDtypeStruct((m, n), F32),
        compiler_params=_params("parallel", "parallel"),
        name="matmul",
    )(a, w)


def _qkv_kernel(a_ref, w_ref, q_ref, k_ref, v_ref):
    y = _bdot(a_ref[...], w_ref[...])
    for part, o_ref in enumerate((q_ref, k_ref, v_ref)):
        @pl.when(pl.program_id(1) == part)
        def _(o_ref=o_ref):
            o_ref[...] = y


def qkv_proj(a, w, *, tm):
    m, k = a.shape
    n = w.shape[1] // 3
    out_spec = pl.BlockSpec((tm, n), lambda i, j: (i, 0))
    return pl.pallas_call(
        _qkv_kernel,
        grid=(pl.cdiv(m, tm), 3),
        in_specs=[pl.BlockSpec((tm, k), lambda i, j: (i, 0)),
                  pl.BlockSpec((k, n), lambda i, j: (0, j))],
        out_specs=[out_spec] * 3,
        out_shape=[jax.ShapeDtypeStruct((m, n), F32)] * 3,
        compiler_params=_params("parallel", "arbitrary"),
        name="qkv_proj",
    )(a, w)


def _matmul_ln_kernel(a_ref, w_ref, res_ref, g_ref, b_ref, o_ref, acc_ref):
    kk = pl.program_id(1)

    @pl.when(kk == 0)
    def _():
        acc_ref[...] = jnp.zeros_like(acc_ref)

    acc_ref[...] += _bdot(a_ref[...], w_ref[...])

    @pl.when(kk == pl.num_programs(1) - 1)
    def _():
        y = ALPHA * res_ref[...] + acc_ref[...]
        o_ref[...] = _layer_norm_rows(y, g_ref[...], b_ref[...])


def matmul_ln(a, w, res, g, b, *, tm, tk):
    m, k = a.shape
    n = w.shape[1]
    return pl.pallas_call(
        _matmul_ln_kernel,
        grid=(pl.cdiv(m, tm), k // tk),
        in_specs=[pl.BlockSpec((tm, tk), lambda i, kk: (i, kk)),
                  pl.BlockSpec((tk, n), lambda i, kk: (kk, 0)),
                  pl.BlockSpec((tm, n), lambda i, kk: (i, 0)),
                  pl.BlockSpec((1, n), lambda i, kk: (0, 0)),
                  pl.BlockSpec((1, n), lambda i, kk: (0, 0))],
        out_specs=pl.BlockSpec((tm, n), lambda i, kk: (i, 0)),
        out_shape=jax.ShapeDtypeStruct((m, n), F32),
        scratch_shapes=[pltpu.VMEM((tm, n), F32)],
        compiler_params=_params("parallel", "arbitrary"),
        name="matmul_ln",
    )(a, w, res, g.reshape(1, n), b.reshape(1, n))


def _mlp_ln_kernel(x_ref, wu_ref, wd_ref, g_ref, b_ref, o_ref, xb_ref, acc_ref):
    f = pl.program_id(1)

    @pl.when(f == 0)
    def _():
        xb_ref[...] = x_ref[...].astype(BF16)
        acc_ref[...] = jnp.zeros_like(acc_ref)

    hdn = jnp.maximum(jnp.dot(xb_ref[...], wu_ref[...], preferred_element_type=F32), 0.0)
    acc_ref[...] += _bdot(hdn * hdn, wd_ref[...])

    @pl.when(f == pl.num_programs(1) - 1)
    def _():
        y = ALPHA * x_ref[...] + acc_ref[...]
        o_ref[...] = _layer_norm_rows(y, g_ref[...], b_ref[...])


def mlp_ln(x, w_up, w_down, g, b, *, tm, tf):
    m, d = x.shape
    ff = w_up.shape[1]
    return pl.pallas_call(
        _mlp_ln_kernel,
        grid=(pl.cdiv(m, tm), ff // tf),
        in_specs=[pl.BlockSpec((tm, d), lambda i, f: (i, 0)),
                  pl.BlockSpec((d, tf), lambda i, f: (0, f)),
                  pl.BlockSpec((tf, d), lambda i, f: (f, 0)),
                  pl.BlockSpec((1, d), lambda i, f: (0, 0)),
                  pl.BlockSpec((1, d), lambda i, f: (0, 0))],
        out_specs=pl.BlockSpec((tm, d), lambda i, f: (i, 0)),
        out_shape=jax.ShapeDtypeStruct((m, d), F32),
        scratch_shapes=[pltpu.VMEM((tm, d), BF16), pltpu.VMEM((tm, d), F32)],
        compiler_params=_params("parallel", "arbitrary"),
        name="mlp_ln",
    )(x, w_up, w_down, g.reshape(1, d), b.reshape(1, d))


def _mixer_prompt_kernel(xc_ref, g0_ref, g1_ref, z0_ref, z1_ref, dt_ref,
                         convw_ref, convb_ref, wa_ref, wx_ref, ba_ref, bx_ref, lam_ref,
                         dtb_ref, alog_ref, dskip_ref, normw_ref,
                         ymix_ref, hlast_ref, state_ref,
                         buf_ref, a_scr, h_scr, y_scr):
    t = SSD_CHUNK
    c = pl.program_id(1)
    head_rows = SUBLANES

    @pl.when(c == 0)
    def _():
        buf_ref[0:head_rows, :] = jnp.zeros((head_rows, D_CONV), F32)
        hlast_ref[...] = jnp.zeros_like(hlast_ref)
        state_ref[...] = jnp.zeros_like(state_ref)

    @pl.when(c > 0)
    def _():
        buf_ref[0:head_rows, :] = buf_ref[t:t + head_rows, :]

    buf_ref[head_rows:head_rows + t, :] = xc_ref[...]

    def conv(lo, width):
        acc = convb_ref[:, lo:lo + width]
        for k in range(CONV_W):
            r0 = head_rows - (CONV_W - 1) + k
            acc = acc + convw_ref[k:k + 1, lo:lo + width] * buf_ref[r0:r0 + t, lo:lo + width]
        return acc

    row = lax.broadcasted_iota(jnp.int32, (t, RG_BW), 0)
    row_in_tile = row % SUBLANES
    is_start = jnp.logical_and(c == 0, row == 0)
    for hd in range(RG_HEADS):
        lo = hd * RG_BW
        u = conv(lo, RG_BW)
        ub = u.astype(BF16)
        r = jax.nn.sigmoid(jnp.dot(ub, wa_ref[hd], preferred_element_type=F32) + ba_ref[:, lo:lo + RG_BW])
        i = jax.nn.sigmoid(jnp.dot(ub, wx_ref[hd], preferred_element_type=F32) + bx_ref[:, lo:lo + RG_BW])
        log_a = -RG_C * r * _softplus(-lam_ref[:, lo:lo + RG_BW])
        a_raw = jnp.exp(log_a)
        a = jnp.where(is_start, 0.0, a_raw)
        mult = jnp.where(is_start, 1.0, _sqrt_one_minus_sq(log_a, a_raw))
        bt = mult * i * u
        for s in (1, 2, 4):
            keep = row_in_tile >= s
            a_sh = pltpu.roll(a, s, axis=0)
            b_sh = pltpu.roll(bt, s, axis=0)
            bt = jnp.where(keep, a * b_sh + bt, bt)
            a = jnp.where(keep, a * a_sh, a)
        a_scr[:, lo:lo + RG_BW] = a
        h_scr[:, lo:lo + RG_BW] = bt

    def carry_tile(gi, carry):
        r0 = pl.multiple_of(gi * SUBLANES, SUBLANES)
        h = a_scr[pl.ds(r0, SUBLANES), :] * carry + h_scr[pl.ds(r0, SUBLANES), :]
        h_scr[pl.ds(r0, SUBLANES), :] = h
        return h[SUBLANES - 1:SUBLANES, :]

    hlast_ref[...] = lax.fori_loop(0, t // SUBLANES, carry_tile, hlast_ref[...])
    half = D_RG // 2
    ymix_ref[:, 0:half] = (h_scr[:, 0:half] * jax.nn.gelu(g0_ref[...])).astype(BF16)
    ymix_ref[:, half:D_RG] = (h_scr[:, half:D_RG] * jax.nn.gelu(g1_ref[...])).astype(BF16)

    lane = lax.broadcasted_iota(jnp.int32, (t, LANES), 1)
    rows = lax.broadcasted_iota(jnp.int32, (t, LANES), 0)
    dt = jnp.where(lane < SSD_HEADS, _softplus(dt_ref[...] + dtb_ref[...]), 0.0)
    cs = dt * (-jnp.exp(alog_ref[...]))
    s = 1
    while s < t:
        cs = cs + jnp.where(rows >= s, pltpu.roll(cs, s, axis=0), 0.0)
        s *= 2
    cs_t = cs.T
    cs_last = cs[t - 1:t, :]
    decay_to_end = jnp.exp(cs_last - cs)
    exp_cs = jnp.exp(cs)
    chunk_decay = jnp.exp(cs_last)

    ii = lax.broadcasted_iota(jnp.int32, (t, t), 0)
    jj = lax.broadcasted_iota(jnp.int32, (t, t), 1)
    causal = jj <= ii
    first_head = lane < SSD_HEAD_DIM
    first_head_rows = rows < SSD_HEAD_DIM

    for g in range(SSD_GROUPS):
        bmat = _silu(conv(D_RG + D_SSD + g * SSD_STATE, SSD_STATE)).astype(BF16)
        cmat = _silu(conv(D_RG + D_SSD + SSD_GN + g * SSD_STATE, SSD_STATE)).astype(BF16)
        cb = lax.dot_general(cmat, bmat, NT_DIMS, preferred_element_type=F32)
        for pair in range(SSD_HEADS // SSD_GROUPS // 2):
            h0 = g * (SSD_HEADS // SSD_GROUPS) + 2 * pair
            lo = h0 * SSD_HEAD_DIM

            def per_head(x2d):
                return jnp.where(first_head, x2d[:, h0:h0 + 1], x2d[:, h0 + 1:h0 + 2])

            xs = _silu(conv(D_RG + lo, LANES))
            xdt = xs * per_head(dt)
            xdt_b = xdt.astype(BF16)
            y_heads = []
            for hh in (h0, h0 + 1):
                seg = cs[:, hh:hh + 1] - cs_t[hh:hh + 1, :]
                lmat = jnp.exp(jnp.where(causal, seg, -jnp.inf))
                y_heads.append(jnp.dot((cb * lmat).astype(BF16), xdt_b, preferred_element_type=F32))
            y = jnp.where(first_head, y_heads[0], y_heads[1])
            s_prev = state_ref[lo:lo + LANES, :]
            y = y + lax.dot_general(cmat, s_prev.astype(BF16), NT_DIMS,
                                    preferred_element_type=F32) * per_head(exp_cs)
            y_scr[:, lo:lo + LANES] = y + dskip_ref[:, lo:lo + LANES] * xs
            decay_rows = jnp.where(first_head_rows, chunk_decay[:, h0:h0 + 1], chunk_decay[:, h0 + 1:h0 + 2])
            state_ref[lo:lo + LANES, :] = s_prev * decay_rows + lax.dot_general(
                (xdt * per_head(decay_to_end)).astype(BF16), bmat, TN_DIMS, preferred_element_type=F32)

    gw = D_SSD // SSD_GROUPS
    for g in range(SSD_GROUPS):
        lo = g * gw
        z_ref = z0_ref if g < SSD_GROUPS // 2 else z1_ref
        zlo = lo % (D_SSD // 2)
        yz = y_scr[:, lo:lo + gw] * _silu(z_ref[:, zlo:zlo + gw])
        ms = jnp.mean(yz * yz, axis=-1, keepdims=True)
        ymix_ref[:, D_RG + lo:D_RG + lo + gw] = (yz * lax.rsqrt(ms + RMS_EPS)
                                                 * normw_ref[:, lo:lo + gw]).astype(BF16)


def _pad_lanes(v):
    return jnp.zeros((1, LANES), F32).at[0, :v.shape[0]].set(v)


def mixer_prompt(proj, nb, conv_w, conv_b, wa, wx, ba, bx, lam, dt_bias, a_log, d_skip, norm_w):
    m = proj.shape[0]
    l = m // nb
    t = SSD_CHUNK
    nc = l // t
    wide = 1024
    row_blk = lambda bi, ci: bi * nc + ci
    full = lambda shape: pl.BlockSpec(shape, lambda bi, ci: (0,) * len(shape))
    gate_blk = D_CONV // wide
    in_specs = [pl.BlockSpec((t, D_CONV), lambda bi, ci: (row_blk(bi, ci), 0))]
    in_specs += [pl.BlockSpec((t, wide), functools.partial(lambda j, bi, ci: (bi * nc + ci, j), gate_blk + j))
                 for j in range(4)]
    in_specs += [pl.BlockSpec((t, LANES), lambda bi, ci: (row_blk(bi, ci), (D_IN - SSD_HEADS) // LANES))]
    in_specs += [full((CONV_W, D_CONV)), full((1, D_CONV)),
                 full((RG_HEADS, RG_BW, RG_BW)), full((RG_HEADS, RG_BW, RG_BW)),
                 full((1, D_RG)), full((1, D_RG)), full((1, D_RG)),
                 full((1, LANES)), full((1, LANES)), full((1, D_SSD)), full((1, D_SSD))]
    ymix, hlast, state = pl.pallas_call(
        _mixer_prompt_kernel,
        grid=(nb, nc),
        in_specs=in_specs,
        out_specs=[pl.BlockSpec((t, D_MIX), lambda bi, ci: (row_blk(bi, ci), 0)),
                   pl.BlockSpec((None, 1, D_RG), lambda bi, ci: (bi, 0, 0)),
                   pl.BlockSpec((None, D_SSD, SSD_STATE), lambda bi, ci: (bi, 0, 0))],
        out_shape=[jax.ShapeDtypeStruct((m, D_MIX), BF16),
                   jax.ShapeDtypeStruct((nb, 1, D_RG), F32),
                   jax.ShapeDtypeStruct((nb, D_SSD, SSD_STATE), F32)],
        scratch_shapes=[pltpu.VMEM((t + 2 * SUBLANES, D_CONV), F32),
                        pltpu.VMEM((t, D_RG), F32), pltpu.VMEM((t, D_RG), F32),
                        pltpu.VMEM((t, D_SSD), F32)],
        compiler_params=_params("parallel", "arbitrary"),
        name="mixer_prompt",
    )(proj, proj, proj, proj, proj, proj,
      conv_w, conv_b.reshape(1, D_CONV), wa.astype(BF16), wx.astype(BF16),
      ba.reshape(1, D_RG), bx.reshape(1, D_RG), lam.reshape(1, D_RG),
      _pad_lanes(dt_bias), _pad_lanes(a_log),
      jnp.repeat(d_skip, SSD_HEAD_DIM).reshape(1, D_SSD), norm_w.reshape(1, D_SSD))
    return ymix, hlast.reshape(nb, D_RG), state


def _moba_prompt_kernel(slopes_ref, q_ref, k_ref, v_ref, o_ref, kb_ref, vt_ref, pen_ref):
    blk = MOBA_BLOCK
    seq = k_ref.shape[0]
    nblk = seq // blk
    slope = slopes_ref[pl.program_id(1)]
    scale = HEAD_DIM ** -0.5

    kmeans = []
    for n in range(nblk):
        rows = slice(n * blk, (n + 1) * blk)
        kn = k_ref[rows, :]
        kmeans.append(jnp.mean(kn, axis=0, keepdims=True))
        kb_ref[rows, :] = kn.astype(BF16)
        vt_ref[:, rows] = v_ref[rows, :].T.astype(BF16)
    kmean = jnp.concatenate(kmeans, axis=0)
    pen_ref[...] = slope * (lax.broadcasted_iota(jnp.int32, (seq, blk), 1)
                            - lax.broadcasted_iota(jnp.int32, (seq, blk), 0)).astype(F32)
    blk_id = lax.broadcasted_iota(jnp.int32, (nblk, blk), 0)
    causal = (lax.broadcasted_iota(jnp.int32, (blk, blk), 0) <= lax.broadcasted_iota(jnp.int32, (blk, blk), 1))

    for qi in range(nblk):
        q = q_ref[qi * blk:(qi + 1) * blk, :]
        qb = q.astype(BF16)
        base = -slope * float(qi * blk)
        if qi > MOBA_TOPK:
            gate = lax.dot_general(kmean, q, NT_DIMS, precision=lax.Precision.HIGHEST,
                                   preferred_element_type=F32)
            gate = jnp.where(blk_id < qi, gate, -jnp.inf)
            ahead = jnp.zeros((nblk, blk), jnp.int32)
            for n in range(qi):
                gate_n = gate[n:n + 1, :]
                before = jnp.logical_or(gate_n > gate, jnp.logical_and(gate_n == gate, n < blk_id))
                ahead = ahead + before.astype(jnp.int32)
            row_terms = jnp.where(ahead < MOBA_TOPK, base, -jnp.inf)
        else:
            row_terms = jnp.full((nblk, blk), base, F32)

        s_blocks = []
        for n in range(qi + 1):
            rows = slice(n * blk, (n + 1) * blk)
            s = lax.dot_general(kb_ref[rows, :], qb, NT_DIMS, preferred_element_type=F32)
            s = s * scale - pen_ref[rows, :]
            if n < qi:
                s = s + row_terms[n:n + 1, :]
            else:
                s = jnp.where(causal, s + base, -jnp.inf)
            s_blocks.append(s)
        m = jnp.max(s_blocks[0], axis=0, keepdims=True)
        for s in s_blocks[1:]:
            m = jnp.maximum(m, jnp.max(s, axis=0, keepdims=True))
        denom = jnp.zeros((1, blk), F32)
        acc = jnp.zeros((HEAD_DIM, blk), F32)
        for n, s in enumerate(s_blocks):
            p = jnp.exp(s - m)
            denom = denom + jnp.sum(p, axis=0, keepdims=True)
            acc = acc + jnp.dot(vt_ref[:, n * blk:(n + 1) * blk], p.astype(BF16), preferred_element_type=F32)
        o_ref[qi * blk:(qi + 1) * blk, :] = (acc / denom).T.astype(o_ref.dtype)


def alibi_slopes():
    return jnp.exp2(-8.0 * jnp.arange(1, N_HEADS + 1, dtype=F32) / N_HEADS)


def moba_prompt(q, k, v, nb):
    m = q.shape[0]
    l = m // nb
    head_spec = pl.BlockSpec((l, HEAD_DIM), lambda bi, hi: (bi, hi))
    return pl.pallas_call(
        _moba_prompt_kernel,
        grid=(nb, N_HEADS),
        in_specs=[pl.BlockSpec(memory_space=pltpu.SMEM), head_spec, head_spec, head_spec],
        out_specs=head_spec,
        out_shape=jax.ShapeDtypeStruct((m, D_MODEL), BF16),
        scratch_shapes=[pltpu.VMEM((l, HEAD_DIM), BF16), pltpu.VMEM((HEAD_DIM, l), BF16),
                        pltpu.VMEM((l, MOBA_BLOCK), F32)],
        compiler_params=_params("parallel", "parallel"),
        name="moba_prompt",
    )(alibi_slopes(), q, k, v)


def _mixer_step_a_kernel(proj_ref, b0_ref, b1_ref, b2_ref, h0_ref,
                         convw_ref, convb_ref, wa_ref, wx_ref, ba_ref, bx_ref, lam_ref, dtb_ref,
                         yrg_ref, hnew_ref, xbc_ref, dt_ref):
    conv = (convb_ref[...] + convw_ref[0:1, :] * b0_ref[...] + convw_ref[1:2, :] * b1_ref[...]
            + convw_ref[2:3, :] * b2_ref[...] + convw_ref[3:4, :] * proj_ref[:, 0:D_CONV])
    xbc_ref[...] = _silu(conv[:, D_RG:])
    dt_ref[...] = _softplus(proj_ref[:, D_IN - SSD_HEADS:D_IN] + dtb_ref[...])
    for hd in range(RG_HEADS):
        lo = hd * RG_BW
        u = conv[:, lo:lo + RG_BW]
        ub = u.astype(BF16)
        r = jax.nn.sigmoid(jnp.dot(ub, wa_ref[hd], preferred_element_type=F32) + ba_ref[:, lo:lo + RG_BW])
        i = jax.nn.sigmoid(jnp.dot(ub, wx_ref[hd], preferred_element_type=F32) + bx_ref[:, lo:lo + RG_BW])
        log_a = -RG_C * r * _softplus(-lam_ref[:, lo:lo + RG_BW])
        a = jnp.exp(log_a)
        h = a * h0_ref[:, lo:lo + RG_BW] + _sqrt_one_minus_sq(log_a, a) * i * u
        hnew_ref[:, lo:lo + RG_BW] = h
        gate = proj_ref[:, D_CONV + lo:D_CONV + lo + RG_BW]
        yrg_ref[:, lo:lo + RG_BW] = h * jax.nn.gelu(gate)


def _ssd_step_kernel(s0_ref, xs_ref, dt_ref, alog_ref, b_ref, c_ref, snew_ref, y_ref):
    dt = dt_ref[...]
    decay = jnp.exp(dt * (-jnp.exp(alog_ref[...])))
    s_new = s0_ref[...] * decay + (xs_ref[...] * dt) * b_ref[...]
    snew_ref[...] = s_new
    y_ref[...] = jnp.sum(s_new * c_ref[...], axis=-1, keepdims=True)


def _mixer_step_b_kernel(yrg_ref, y_ref, xs_ref, z_ref, dskip_ref, normw_ref, o_ref):
    o_ref[:, 0:D_RG] = yrg_ref[...].astype(BF16)
    yz = (y_ref[...] + dskip_ref[...] * xs_ref[...]) * _silu(z_ref[...])
    gw = D_SSD // SSD_GROUPS
    for g in range(SSD_GROUPS):
        blk = yz[:, g * gw:(g + 1) * gw]
        ms = jnp.mean(blk * blk, axis=-1, keepdims=True)
        o_ref[:, D_RG + g * gw:D_RG + (g + 1) * gw] = (
            blk * lax.rsqrt(ms + RMS_EPS) * normw_ref[:, g * gw:(g + 1) * gw]).astype(BF16)


def _whole(shape):
    return pl.BlockSpec(shape, lambda *_: (0,) * len(shape))


def mixer_sample(proj, conv_buf, h0, s0, conv_w, conv_b, wa, wx, ba, bx, lam, dt_bias, a_log, d_skip, norm_w):
    nb = proj.shape[0]
    hpg = SSD_HEADS // SSD_GROUPS
    a_ins = (proj, conv_buf[:, 0], conv_buf[:, 1], conv_buf[:, 2], h0,
             conv_w, conv_b.reshape(1, D_CONV), wa.astype(BF16), wx.astype(BF16),
             ba.reshape(1, D_RG), bx.reshape(1, D_RG), lam.reshape(1, D_RG), dt_bias.reshape(1, SSD_HEADS))
    yrg, hnew, xbc, dt = pl.pallas_call(
        _mixer_step_a_kernel,
        grid=(1,),
        in_specs=[_whole(a.shape) for a in a_ins],
        out_specs=[_whole((nb, D_RG)), _whole((nb, D_RG)), _whole((nb, D_XBC)), _whole((nb, SSD_HEADS))],
        out_shape=[jax.ShapeDtypeStruct((nb, D_RG), F32), jax.ShapeDtypeStruct((nb, D_RG), F32),
                   jax.ShapeDtypeStruct((nb, D_XBC), F32), jax.ShapeDtypeStruct((nb, SSD_HEADS), F32)],
        compiler_params=_params("arbitrary"),
        name="mixer_step_a",
    )(*a_ins)
    xs = xbc[:, :D_SSD]
    snew, y = pl.pallas_call(
        _ssd_step_kernel,
        grid=(nb, SSD_GROUPS),
        in_specs=[pl.BlockSpec((None, hpg, SSD_HEAD_DIM, SSD_STATE), lambda bi, gi: (bi, gi, 0, 0)),
                  pl.BlockSpec((None, hpg, SSD_HEAD_DIM, 1), lambda bi, gi: (bi, gi, 0, 0)),
                  pl.BlockSpec((None, hpg, 1, 1), lambda bi, gi: (bi, gi, 0, 0)),
                  pl.BlockSpec((hpg, 1, 1), lambda bi, gi: (gi, 0, 0)),
                  pl.BlockSpec((None, None, 1, SSD_STATE), lambda bi, gi: (bi, gi, 0, 0)),
                  pl.BlockSpec((None, None, 1, SSD_STATE), lambda bi, gi: (bi, gi, 0, 0))],
        out_specs=[pl.BlockSpec((None, hpg, SSD_HEAD_DIM, SSD_STATE), lambda bi, gi: (bi, gi, 0, 0)),
                   pl.BlockSpec((None, hpg, SSD_HEAD_DIM, 1), lambda bi, gi: (bi, gi, 0, 0))],
        out_shape=[jax.ShapeDtypeStruct((nb, SSD_HEADS, SSD_HEAD_DIM, SSD_STATE), F32),
                   jax.ShapeDtypeStruct((nb, SSD_HEADS, SSD_HEAD_DIM, 1), F32)],
        compiler_params=_params("parallel", "parallel"),
        name="ssd_step",
    )(s0, xs.reshape(nb, SSD_HEADS, SSD_HEAD_DIM, 1), dt.reshape(nb, SSD_HEADS, 1, 1),
      a_log.reshape(SSD_HEADS, 1, 1),
      xbc[:, D_SSD:D_SSD + SSD_GN].reshape(nb, SSD_GROUPS, 1, SSD_STATE),
      xbc[:, D_SSD + SSD_GN:].reshape(nb, SSD_GROUPS, 1, SSD_STATE))
    b_ins = (yrg, y.reshape(nb, D_SSD), xs, proj[:, D_CONV + D_RG:D_CONV + D_RG + D_SSD],
             jnp.repeat(d_skip, SSD_HEAD_DIM).reshape(1, D_SSD), norm_w.reshape(1, D_SSD))
    ymix = pl.pallas_call(
        _mixer_step_b_kernel,
        grid=(1,),
        in_specs=[_whole(a.shape) for a in b_ins],
        out_specs=_whole((nb, D_MIX)),
        out_shape=jax.ShapeDtypeStruct((nb, D_MIX), BF16),
        compiler_params=_params("arbitrary"),
        name="mixer_step_b",
    )(*b_ins)
    return ymix, hnew, snew


PAGES_PER_BLOCK = MOBA_BLOCK // PAGE_SIZE
KSUM_BLOCKS_PER_STEP = 2


def _block_ksum_kernel(pt_ref, *refs):
    page_refs, o_ref = refs[:-1], refs[-1]
    for blk in range(KSUM_BLOCKS_PER_STEP):
        acc = jnp.sum(page_refs[blk * PAGES_PER_BLOCK][...], axis=0)
        for p in range(1, PAGES_PER_BLOCK):
            acc = acc + jnp.sum(page_refs[blk * PAGES_PER_BLOCK + p][...], axis=0)
        o_ref[blk] = acc


def block_key_sums(cache_k, page_table):
    nb, n_pages = page_table.shape
    nblk = n_pages // PAGES_PER_BLOCK
    pages_per_step = KSUM_BLOCKS_PER_STEP * PAGES_PER_BLOCK

    def page_spec(s):
        return pl.BlockSpec((None, None, PAGE_SIZE, N_HEADS, HEAD_DIM),
                            lambda bi, ni, pt: (0, pt[bi, ni * pages_per_step + s], 0, 0, 0))

    return pl.pallas_call(
        _block_ksum_kernel,
        grid_spec=pltpu.PrefetchScalarGridSpec(
            num_scalar_prefetch=1,
            grid=(nb, nblk // KSUM_BLOCKS_PER_STEP),
            in_specs=[page_spec(s) for s in range(pages_per_step)],
            out_specs=pl.BlockSpec((None, KSUM_BLOCKS_PER_STEP, N_HEADS, HEAD_DIM),
                                   lambda bi, ni, pt: (bi, ni, 0, 0))),
        out_shape=jax.ShapeDtypeStruct((nb, nblk, N_HEADS, HEAD_DIM), F32),
        compiler_params=_params("parallel", "parallel"),
        name="block_key_sums",
    )(page_table, *([cache_k] * pages_per_step))


def _moba_topk_kernel(q_ref, ksum_ref, idx_ref):
    nblk = ksum_ref.shape[0]
    qh = q_ref[...] * (1.0 / MOBA_BLOCK)
    lane = lax.broadcasted_iota(jnp.int32, (N_HEADS, LANES), 1)
    gate = jnp.full((N_HEADS, LANES), -jnp.inf, F32)
    for n in range(nblk):
        gate = jnp.where(lane == n, jnp.sum(ksum_ref[n] * qh, axis=-1, keepdims=True), gate)
    out = jnp.zeros((N_HEADS, LANES), jnp.int32)
    for kk in range(MOBA_TOPK):
        best = jnp.max(gate, axis=-1, keepdims=True)
        pick = jnp.min(jnp.where(gate == best, lane, LANES), axis=-1, keepdims=True)
        out = jnp.where(lane == kk, pick, out)
        gate = jnp.where(lane == pick, -jnp.inf, gate)
    idx_ref[...] = out


def moba_topk(q, ksum):
    nb, nblk = ksum.shape[:2]
    assert MOBA_TOPK <= nblk <= LANES
    return pl.pallas_call(
        _moba_topk_kernel,
        grid=(nb,),
        in_specs=[pl.BlockSpec((None, N_HEADS, HEAD_DIM), lambda bi: (bi, 0, 0)),
                  pl.BlockSpec((None, nblk, N_HEADS, HEAD_DIM), lambda bi: (bi, 0, 0, 0))],
        out_specs=pl.BlockSpec((None, N_HEADS, LANES), lambda bi: (bi, 0, 0)),
        out_shape=jax.ShapeDtypeStruct((nb, N_HEADS, LANES), jnp.int32),
        compiler_params=_params("parallel"),
        name="moba_topk",
    )(q, ksum)


def _moba_decode_kernel(pt_ref, top_ref, slopes_ref, q_ref, kn_ref, vn_ref, ck_hbm, cv_hbm, o_ref,
                        kbuf, vbuf, sem, *, past_len):
    bi = pl.program_id(0)
    scale = HEAD_DIM ** -0.5
    slots = MOBA_TOPK * PAGES_PER_BLOCK

    def head_copies(hd):
        copies = []
        for kk in range(MOBA_TOPK):
            blk_id = top_ref[bi, hd, kk]
            for half in range(PAGES_PER_BLOCK):
                page = pt_ref[bi, PAGES_PER_BLOCK * blk_id + half]
                slot = kk * PAGES_PER_BLOCK + half
                copies.append(pltpu.make_async_copy(ck_hbm.at[0, page, :, hd, :], kbuf.at[hd, slot],
                                                    sem.at[0, hd, slot]))
                copies.append(pltpu.make_async_copy(cv_hbm.at[0, page, :, hd, :], vbuf.at[hd, slot],
                                                    sem.at[1, hd, slot]))
        return copies

    all_copies = [head_copies(hd) for hd in range(N_HEADS)]
    for copies in all_copies:
        for cp in copies:
            cp.start()

    key_off = lax.broadcasted_iota(jnp.int32, (1, MOBA_BLOCK), 1)
    for hd in range(N_HEADS):
        for cp in all_copies[hd]:
            cp.wait()
        lo = hd * HEAD_DIM
        slope = slopes_ref[hd]
        q = q_ref[:, lo:lo + HEAD_DIM]
        q8 = jnp.broadcast_to(q, (SUBLANES, HEAD_DIM)).astype(BF16)
        kh = kbuf[hd].reshape(slots * PAGE_SIZE, HEAD_DIM).astype(BF16)
        vh = vbuf[hd].reshape(slots * PAGE_SIZE, HEAD_DIM).astype(BF16)
        dist = jnp.concatenate(
            [(past_len - top_ref[bi, hd, kk] * MOBA_BLOCK - key_off).astype(F32) for kk in range(MOBA_TOPK)],
            axis=1)
        s = lax.dot_general(q8, kh, NT_DIMS, preferred_element_type=F32)[0:1, :] * scale - slope * dist
        s_own = jnp.sum(q * kn_ref[:, lo:lo + HEAD_DIM], axis=-1, keepdims=True) * scale
        m = jnp.maximum(jnp.max(s, axis=-1, keepdims=True), s_own)
        p = jnp.exp(s - m)
        p_own = jnp.exp(s_own - m)
        denom = jnp.sum(p, axis=-1, keepdims=True) + p_own
        p8 = jnp.broadcast_to(p, (SUBLANES, slots * PAGE_SIZE)).astype(BF16)
        acc = jnp.dot(p8, vh, preferred_element_type=F32)[0:1, :] + p_own * vn_ref[:, lo:lo + HEAD_DIM]
        o_ref[:, lo:lo + HEAD_DIM] = acc / denom


def moba_decode(q, k_new, v_new, cache_k, cache_v, page_table, top_idx):
    nb, n_pages = page_table.shape
    slots = MOBA_TOPK * PAGES_PER_BLOCK
    row_spec = pl.BlockSpec((None, 1, D_MODEL), lambda bi, pt, top: (bi, 0, 0))
    return pl.pallas_call(
        functools.partial(_moba_decode_kernel, past_len=n_pages * PAGE_SIZE),
        grid_spec=pltpu.PrefetchScalarGridSpec(
            num_scalar_prefetch=2,
            grid=(nb,),
            in_specs=[pl.BlockSpec(memory_space=pltpu.SMEM), row_spec, row_spec, row_spec,
                      pl.BlockSpec(memory_space=pl.ANY), pl.BlockSpec(memory_space=pl.ANY)],
            out_specs=row_spec,
            scratch_shapes=[pltpu.VMEM((N_HEADS, slots, PAGE_SIZE, HEAD_DIM), F32),
                            pltpu.VMEM((N_HEADS, slots, PAGE_SIZE, HEAD_DIM), F32),
                            pltpu.SemaphoreType.DMA((2, N_HEADS, slots))]),
        out_shape=jax.ShapeDtypeStruct((nb, 1, D_MODEL), F32),
        compiler_params=_params("arbitrary"),
        name="moba_decode",
    )(page_table, top_idx, alibi_slopes(), q, k_new, v_new, cache_k, cache_v)


def _trunk_prompt(x, w, nb):
    tm = 512
    proj = matmul(x, w["w_in"], tm=tm, tn=1024)
    l = x.shape[0] // nb
    conv_tail = proj.reshape(nb, l, D_IN_PAD)[:, l - (CONV_W - 1):, :D_CONV]
    ymix, h_last, s_last = mixer_prompt(proj, nb, *w["mixer"])
    x = matmul_ln(ymix, w["w_out"], x, w["ln1_g"][0], w["ln1_b"][0], tm=tm, tk=2048)
    x = mlp_ln(x, w["w_up"][0], w["w_down"][0], w["ln2_g"][0], w["ln2_b"][0], tm=tm, tf=1024)
    q, k, v = qkv_proj(x, w["w_qkv"], tm=tm)
    o = moba_prompt(q, k, v, nb)
    x = matmul_ln(o, w["w_o"], x, w["ln1_g"][1], w["ln1_b"][1], tm=tm, tk=1024)
    x = mlp_ln(x, w["w_up"][1], w["w_down"][1], w["ln2_g"][1], w["ln2_b"][1], tm=tm, tf=1024)
    return x, conv_tail, h_last, s_last, k, v


def _trunk_sample(x, conv_buf, h0, s0, cache_k, cache_v, page_table, w):
    nb = x.shape[0]
    proj = matmul(x, w["w_in"], tm=nb, tn=1024)
    new_buf = jnp.concatenate([conv_buf[:, 1:], proj[:, None, :D_CONV]], axis=1)
    ymix, h_new, s_new = mixer_sample(proj, conv_buf, h0, s0, *w["mixer"])
    x = matmul_ln(ymix, w["w_out"], x, w["ln1_g"][0], w["ln1_b"][0], tm=nb, tk=2048)
    x = mlp_ln(x, w["w_up"][0], w["w_down"][0], w["ln2_g"][0], w["ln2_b"][0], tm=nb, tf=1024)
    q, k, v = qkv_proj(x, w["w_qkv"], tm=nb)
    ksum = block_key_sums(cache_k, page_table)
    top_idx = moba_topk(q.reshape(nb, N_HEADS, HEAD_DIM), ksum)[:, :, :MOBA_TOPK]
    o = moba_decode(q.reshape(nb, 1, D_MODEL), k.reshape(nb, 1, D_MODEL), v.reshape(nb, 1, D_MODEL),
                    cache_k, cache_v, page_table, top_idx).reshape(nb, D_MODEL)
    x = matmul_ln(o, w["w_o"], x, w["ln1_g"][1], w["ln1_b"][1], tm=nb, tk=2048)
    x = mlp_ln(x, w["w_up"][1], w["w_down"][1], w["ln2_g"][1], w["ln2_b"][1], tm=nb, tf=1024)
    return x, new_buf, h_new, s_new, k, v


def kernel(x_prompt, x_sample, state_conv, state_rglru, state_ssd, cache_k, cache_v, page_table, w_in, conv_w, conv_b, rg_wa, rg_ba, rg_wx, rg_bx, rg_lambda, ssd_dt_bias, ssd_a_log, ssd_d, ssd_norm_w, w_out, w_qkv, w_o, ln1_g, ln1_b, ln2_g, ln2_b, w_up, w_down):
    bp, lp, _ = x_prompt.shape
    bs = x_sample.shape[0]
    w = {
        "w_in": jnp.pad(w_in[0], ((0, 0), (0, D_IN_PAD - D_IN))).astype(BF16), "w_out": w_out[0].astype(BF16),
        "w_qkv": w_qkv[0].astype(BF16), "w_o": w_o[0].astype(BF16),
        "w_up": [w_up[i].astype(BF16) for i in range(DEPTH)],
        "w_down": [w_down[i].astype(BF16) for i in range(DEPTH)],
        "ln1_g": ln1_g, "ln1_b": ln1_b, "ln2_g": ln2_g, "ln2_b": ln2_b,
        "mixer": (conv_w[0], conv_b[0], rg_wa[0], rg_wx[0], rg_ba[0], rg_bx[0], rg_lambda[0],
                  ssd_dt_bias[0], ssd_a_log[0], ssd_d[0], ssd_norm_w[0]),
    }
    yp, conv_p, h_p, s_p, k_p, v_p = _trunk_prompt(x_prompt.reshape(bp * lp, D_MODEL), w, bp)
    ys, conv_s, h_s, s_s, k_s, v_s = _trunk_sample(
        x_sample.reshape(bs, D_MODEL), state_conv[0], state_rglru[0], state_ssd[0],
        cache_k, cache_v, page_table, w)
    return (yp.reshape(bp, lp, D_MODEL), ys.reshape(bs, 1, D_MODEL),
            conv_p[None], h_p[None], s_p.reshape(1, bp, SSD_HEADS, SSD_HEAD_DIM, SSD_STATE),
            k_p.reshape(1, bp, lp, N_HEADS, HEAD_DIM), v_p.reshape(1, bp, lp, N_HEADS, HEAD_DIM),
            conv_s[None], h_s[None], s_s[None],
            k_s.reshape(1, bs, 1, N_HEADS, HEAD_DIM), v_s.reshape(1, bs, 1, N_HEADS, HEAD_DIM))
```

```python
import functools
import math

import jax
import jax.numpy as jnp
from jax import lax
from jax.experimental import pallas as pl
from jax.experimental.pallas import tpu as pltpu

F32 = jnp.float32
BF16 = jnp.bfloat16

D_MODEL = 2048
DEPTH = 2
PAGE_SIZE = 128
CONV_W = 4
D_RG = D_MODEL
RG_HEADS = 8
RG_BW = D_RG // RG_HEADS
RG_C = 8.0
D_SSD = D_MODEL
SSD_HEAD_DIM = 64
SSD_HEADS = D_SSD // SSD_HEAD_DIM
SSD_GROUPS = 4
SSD_STATE = 128
SSD_CHUNK = 128
SSD_GN = SSD_GROUPS * SSD_STATE
D_XBC = D_SSD + 2 * SSD_GN
D_CONV = D_RG + D_XBC
D_IN = D_CONV + D_RG + D_SSD + SSD_HEADS
D_IN_PAD = D_IN + 128 - SSD_HEADS
D_MIX = D_RG + D_SSD
N_HEADS = 16
HEAD_DIM = D_MODEL // N_HEADS
MOBA_BLOCK = 256
MOBA_TOPK = 3
PAGES_PER_BLOCK = MOBA_BLOCK // PAGE_SIZE
KSUM_BLOCKS_PER_STEP = 2
D_FF = 4 * D_MODEL
ALPHA = (2 * DEPTH) ** 0.25
LN_EPS = 1e-5
RMS_EPS = 1e-6

SUBLANES = 8
LANES = 128
VMEM_LIMIT_BYTES = 56 * 1024 * 1024

NT_DIMS = (((1,), (1,)), ((), ()))
TN_DIMS = (((0,), (0,)), ((), ()))


def _params(*semantics):
    return pltpu.CompilerParams(dimension_semantics=semantics, vmem_limit_bytes=VMEM_LIMIT_BYTES)


def _bdot(a, b):
    return jnp.dot(a.astype(BF16), b.astype(BF16), preferred_element_type=F32)


def _layer_norm_rows(y, g, b):
    mu = jnp.mean(y, axis=-1, keepdims=True)
    yc = y - mu
    var = jnp.mean(yc * yc, axis=-1, keepdims=True)
    return yc * lax.rsqrt(var + LN_EPS) * g + b


def _softplus(x):
    u = jnp.exp(-jnp.abs(x))
    w = 1.0 + u
    log1p_u = jnp.where(w == 1.0, u, jnp.log(w) * (u / (w - 1.0)))
    return jnp.maximum(x, 0.0) + log1p_u


def _sqrt_one_minus_sq(log_a, a):
    return jnp.sqrt(-jnp.tanh(log_a) * (a * a + 1.0))


def _silu(x):
    return x * jax.nn.sigmoid(x)


def _matmul_kernel(a_ref, w_ref, o_ref, ab_ref):
    @pl.when(pl.program_id(1) == 0)
    def _():
        ab_ref[...] = a_ref[...].astype(BF16)

    o_ref[...] = jnp.dot(ab_ref[...], w_ref[...], preferred_element_type=F32)


def matmul(a, w, *, tm, tn):
    m, k = a.shape
    n = w.shape[1]
    return pl.pallas_call(
        _matmul_kernel,
        grid=(pl.cdiv(m, tm), pl.cdiv(n, tn)),
        in_specs=[pl.BlockSpec((tm, k), lambda i, j: (i, 0)),
                  pl.BlockSpec((k, tn), lambda i, j: (0, j))],
        out_specs=pl.BlockSpec((tm, tn), lambda i, j: (i, j)),
        out_shape=jax.ShapeDtypeStruct((m, n), F32),
        scratch_shapes=[pltpu.VMEM((tm, k), BF16)],
        compiler_params=_params("parallel", "arbitrary"),
        name="matmul",
    )(a, w)


def _qkv_kernel(a_ref, w_ref, q_ref, k_ref, v_ref, ab_ref):
    @pl.when(pl.program_id(1) == 0)
    def _():
        ab_ref[...] = a_ref[...].astype(BF16)

    y = jnp.dot(ab_ref[...], w_ref[...], preferred_element_type=F32)
    for part, o_ref in enumerate((q_ref, k_ref, v_ref)):
        @pl.when(pl.program_id(1) == part)
        def _(o_ref=o_ref):
            o_ref[...] = y


def qkv_proj(a, w, *, tm):
    m, k = a.shape
    n = w.shape[1] // 3
    out_spec = pl.BlockSpec((tm, n), lambda i, j: (i, 0))
    return pl.pallas_call(
        _qkv_kernel,
        grid=(pl.cdiv(m, tm), 3),
        in_specs=[pl.BlockSpec((tm, k), lambda i, j: (i, 0)),
                  pl.BlockSpec((k, n), lambda i, j: (0, j))],
        out_specs=[out_spec] * 3,
        out_shape=[jax.ShapeDtypeStruct((m, n), F32)] * 3,
        scratch_shapes=[pltpu.VMEM((tm, k), BF16)],
        compiler_params=_params("parallel", "arbitrary"),
        name="qkv_proj",
    )(a, w)


def _matmul_ln_kernel(a_ref, w_ref, res_ref, g_ref, b_ref, o_ref, acc_ref):
    kk = pl.program_id(1)

    @pl.when(kk == 0)
    def _():
        acc_ref[...] = jnp.zeros_like(acc_ref)

    acc_ref[...] += _bdot(a_ref[...], w_ref[...])

    @pl.when(kk == pl.num_programs(1) - 1)
    def _():
        y = ALPHA * res_ref[...] + acc_ref[...]
        o_ref[...] = _layer_norm_rows(y, g_ref[...], b_ref[...])


def _matmul_ln_full_k_kernel(a_ref, w_ref, res_ref, g_ref, b_ref, o_ref):
    y = ALPHA * res_ref[...] + _bdot(a_ref[...], w_ref[...])
    o_ref[...] = _layer_norm_rows(y, g_ref[...], b_ref[...])


def matmul_ln(a, w, res, g, b, *, tm, tk):
    m, k = a.shape
    n = w.shape[1]
    if tk == k:
        return pl.pallas_call(
            _matmul_ln_full_k_kernel,
            grid=(pl.cdiv(m, tm),),
            in_specs=[pl.BlockSpec((tm, k), lambda i: (i, 0)),
                      pl.BlockSpec((k, n), lambda i: (0, 0)),
                      pl.BlockSpec((tm, n), lambda i: (i, 0)),
                      pl.BlockSpec((1, n), lambda i: (0, 0)),
                      pl.BlockSpec((1, n), lambda i: (0, 0))],
            out_specs=pl.BlockSpec((tm, n), lambda i: (i, 0)),
            out_shape=jax.ShapeDtypeStruct((m, n), F32),
            compiler_params=_params("parallel"),
            name="matmul_ln_full_k",
        )(a, w, res, g.reshape(1, n), b.reshape(1, n))
    return pl.pallas_call(
        _matmul_ln_kernel,
        grid=(pl.cdiv(m, tm), k // tk),
        in_specs=[pl.BlockSpec((tm, tk), lambda i, kk: (i, kk)),
                  pl.BlockSpec((tk, n), lambda i, kk: (kk, 0)),
                  pl.BlockSpec((tm, n), lambda i, kk: (i, 0)),
                  pl.BlockSpec((1, n), lambda i, kk: (0, 0)),
                  pl.BlockSpec((1, n), lambda i, kk: (0, 0))],
        out_specs=pl.BlockSpec((tm, n), lambda i, kk: (i, 0)),
        out_shape=jax.ShapeDtypeStruct((m, n), F32),
        scratch_shapes=[pltpu.VMEM((tm, n), F32)],
        compiler_params=_params("parallel", "arbitrary"),
        name="matmul_ln",
    )(a, w, res, g.reshape(1, n), b.reshape(1, n))


def _mlp_ln_body(x_ref, wu_ref, wd_ref, g_ref, b_ref, o_ref, xb_ref, acc_ref):
    f = pl.program_id(1)

    @pl.when(f == 0)
    def _():
        xb_ref[...] = x_ref[...].astype(BF16)
        acc_ref[...] = jnp.zeros_like(acc_ref)

    hdn = jnp.maximum(jnp.dot(xb_ref[...], wu_ref[...], preferred_element_type=F32), 0.0)
    acc_ref[...] += _bdot(hdn * hdn, wd_ref[...])

    @pl.when(f == pl.num_programs(1) - 1)
    def _():
        y = ALPHA * x_ref[...] + acc_ref[...]
        o_ref[...] = _layer_norm_rows(y, g_ref[...], b_ref[...])


def _mlp_ln_kernel(x_ref, wu_ref, wd_ref, g_ref, b_ref, o_ref, xb_ref, acc_ref):
    _mlp_ln_body(x_ref, wu_ref, wd_ref, g_ref, b_ref, o_ref, xb_ref, acc_ref)


def _mlp_ln_ksum_kernel(pt_ref, x_ref, wu_ref, wd_ref, g_ref, b_ref, ck_hbm, o_ref, ksum_ref,
                        xb_ref, acc_ref, kbuf, sem, *, seq0, nblk):
    step = pl.program_id(0) * pl.num_programs(1) + pl.program_id(1)
    n_steps = pl.num_programs(0) * pl.num_programs(1)
    pages_per_step = KSUM_BLOCKS_PER_STEP * PAGES_PER_BLOCK

    def page_copies(at_step, slot):
        copies = []
        for pg in range(pages_per_step):
            flat = at_step * KSUM_BLOCKS_PER_STEP + pg // PAGES_PER_BLOCK
            page = pt_ref[seq0 + flat // nblk, (flat % nblk) * PAGES_PER_BLOCK + pg % PAGES_PER_BLOCK]
            copies.append(pltpu.make_async_copy(ck_hbm.at[0, page], kbuf.at[slot, pg], sem.at[slot, pg]))
        return copies

    @pl.when(step == 0)
    def _():
        for cp in page_copies(0, 0):
            cp.start()

    @pl.when(step + 1 < n_steps)
    def _():
        for cp in page_copies(step + 1, (step + 1) % 2):
            cp.start()

    _mlp_ln_body(x_ref, wu_ref, wd_ref, g_ref, b_ref, o_ref, xb_ref, acc_ref)

    slot = step % 2
    for cp in page_copies(step, slot):
        cp.wait()
    for j in range(KSUM_BLOCKS_PER_STEP):
        acc = jnp.sum(kbuf[slot, j * PAGES_PER_BLOCK], axis=0)
        for p in range(1, PAGES_PER_BLOCK):
            acc = acc + jnp.sum(kbuf[slot, j * PAGES_PER_BLOCK + p], axis=0)
        ksum_ref[j] = acc


def mlp_ln(x, w_up, w_down, layer, g, b, *, tm, tf):
    m, d = x.shape
    ff = w_up.shape[2]
    return pl.pallas_call(
        _mlp_ln_kernel,
        grid=(pl.cdiv(m, tm), ff // tf),
        in_specs=[pl.BlockSpec((tm, d), lambda i, f: (i, 0)),
                  pl.BlockSpec((None, d, tf), lambda i, f: (layer, 0, f)),
                  pl.BlockSpec((None, tf, d), lambda i, f: (layer, f, 0)),
                  pl.BlockSpec((1, d), lambda i, f: (0, 0)),
                  pl.BlockSpec((1, d), lambda i, f: (0, 0))],
        out_specs=pl.BlockSpec((tm, d), lambda i, f: (i, 0)),
        out_shape=jax.ShapeDtypeStruct((m, d), F32),
        scratch_shapes=[pltpu.VMEM((tm, d), BF16), pltpu.VMEM((tm, d), F32)],
        compiler_params=_params("parallel", "arbitrary"),
        name="mlp_ln",
    )(x, w_up, w_down, g.reshape(1, d), b.reshape(1, d))


def mlp_ln_ksum(x, w_up, w_down, layer, g, b, cache_k, page_table, seq0, n_seq, *, tm, tf):
    m, d = x.shape
    ff = w_up.shape[2]
    grid = (m // tm, ff // tf)
    nblk = page_table.shape[1] // PAGES_PER_BLOCK
    assert n_seq * nblk == grid[0] * grid[1] * KSUM_BLOCKS_PER_STEP and nblk % KSUM_BLOCKS_PER_STEP == 0
    steps_per_seq = nblk // KSUM_BLOCKS_PER_STEP

    def ksum_index(i, f, pt):
        step = i * grid[1] + f
        return (step // steps_per_seq, step % steps_per_seq, 0, 0)

    return pl.pallas_call(
        functools.partial(_mlp_ln_ksum_kernel, seq0=seq0, nblk=nblk),
        grid_spec=pltpu.PrefetchScalarGridSpec(
            num_scalar_prefetch=1,
            grid=grid,
            in_specs=[pl.BlockSpec((tm, d), lambda i, f, pt: (i, 0)),
                      pl.BlockSpec((None, d, tf), lambda i, f, pt: (layer, 0, f)),
                      pl.BlockSpec((None, tf, d), lambda i, f, pt: (layer, f, 0)),
                      pl.BlockSpec((1, d), lambda i, f, pt: (0, 0)),
                      pl.BlockSpec((1, d), lambda i, f, pt: (0, 0)),
                      pl.BlockSpec(memory_space=pl.ANY)],
            out_specs=[pl.BlockSpec((tm, d), lambda i, f, pt: (i, 0)),
                       pl.BlockSpec((None, KSUM_BLOCKS_PER_STEP, N_HEADS, HEAD_DIM), ksum_index)],
            scratch_shapes=[pltpu.VMEM((tm, d), BF16), pltpu.VMEM((tm, d), F32),
                            pltpu.VMEM((2, KSUM_BLOCKS_PER_STEP * PAGES_PER_BLOCK, PAGE_SIZE, N_HEADS, HEAD_DIM), F32),
                            pltpu.SemaphoreType.DMA((2, KSUM_BLOCKS_PER_STEP * PAGES_PER_BLOCK))]),
        out_shape=[jax.ShapeDtypeStruct((m, d), F32),
                   jax.ShapeDtypeStruct((n_seq, nblk, N_HEADS, HEAD_DIM), F32)],
        compiler_params=_params("arbitrary", "arbitrary"),
        name="mlp_ln_ksum",
    )(page_table, x, w_up, w_down, g.reshape(1, d), b.reshape(1, d), cache_k)


def _mixer_prompt_kernel(xc_ref, g0_ref, g1_ref, z0_ref, z1_ref, dt_ref,
                         convw_ref, convb_ref, wa_ref, wx_ref, ba_ref, bx_ref, lam_ref,
                         dtb_ref, alog_ref, dskip_ref, normw_ref,
                         ymix_ref, hlast_ref, state_ref,
                         buf_ref, a_scr, h_scr, y_scr, decay_rate_scr):
    t = SSD_CHUNK
    c = pl.program_id(1)
    head_rows = SUBLANES

    @pl.when(c == 0)
    def _():
        buf_ref[0:head_rows, :] = jnp.zeros((head_rows, D_CONV), F32)
        hlast_ref[...] = jnp.zeros_like(hlast_ref)
        state_ref[...] = jnp.zeros_like(state_ref)
        decay_rate_scr[...] = -RG_C * _softplus(-lam_ref[...])

    @pl.when(c > 0)
    def _():
        buf_ref[0:head_rows, :] = buf_ref[t:t + head_rows, :]

    buf_ref[head_rows:head_rows + t, :] = xc_ref[...]

    def conv(lo, width):
        acc = convb_ref[:, lo:lo + width]
        for k in range(CONV_W):
            r0 = head_rows - (CONV_W - 1) + k
            acc = acc + convw_ref[k:k + 1, lo:lo + width] * buf_ref[r0:r0 + t, lo:lo + width]
        return acc

    row = lax.broadcasted_iota(jnp.int32, (t, RG_BW), 0)
    row_in_tile = row % SUBLANES
    is_start = jnp.logical_and(c == 0, row == 0)
    for hd in range(RG_HEADS):
        lo = hd * RG_BW
        u = conv(lo, RG_BW)
        ub = u.astype(BF16)
        r = jax.nn.sigmoid(jnp.dot(ub, wa_ref[hd], preferred_element_type=F32) + ba_ref[:, lo:lo + RG_BW])
        i = jax.nn.sigmoid(jnp.dot(ub, wx_ref[hd], preferred_element_type=F32) + bx_ref[:, lo:lo + RG_BW])
        log_a = decay_rate_scr[:, lo:lo + RG_BW] * r
        a_raw = jnp.exp(log_a)
        a = jnp.where(is_start, 0.0, a_raw)
        mult = jnp.where(is_start, 1.0, _sqrt_one_minus_sq(log_a, a_raw))
        bt = mult * i * u
        for s in (1, 2, 4):
            keep = row_in_tile >= s
            a_sh = pltpu.roll(a, s, axis=0)
            b_sh = pltpu.roll(bt, s, axis=0)
            bt = jnp.where(keep, a * b_sh + bt, bt)
            a = jnp.where(keep, a * a_sh, a)
        a_scr[:, lo:lo + RG_BW] = a
        h_scr[:, lo:lo + RG_BW] = bt

    def carry_tile(gi, carry):
        r0 = pl.multiple_of(gi * SUBLANES, SUBLANES)
        h = a_scr[pl.ds(r0, SUBLANES), :] * carry + h_scr[pl.ds(r0, SUBLANES), :]
        h_scr[pl.ds(r0, SUBLANES), :] = h
        return h[SUBLANES - 1:SUBLANES, :]

    hlast_ref[...] = lax.fori_loop(0, t // SUBLANES, carry_tile, hlast_ref[...])
    half = D_RG // 2
    ymix_ref[:, 0:half] = (h_scr[:, 0:half] * jax.nn.gelu(g0_ref[...])).astype(BF16)
    ymix_ref[:, half:D_RG] = (h_scr[:, half:D_RG] * jax.nn.gelu(g1_ref[...])).astype(BF16)

    lane = lax.broadcasted_iota(jnp.int32, (t, LANES), 1)
    rows = lax.broadcasted_iota(jnp.int32, (t, LANES), 0)
    dt = jnp.where(lane < SSD_HEADS, _softplus(dt_ref[...] + dtb_ref[...]), 0.0)
    cs = dt * (-jnp.exp(alog_ref[...]))
    s = 1
    while s < t:
        cs = cs + jnp.where(rows >= s, pltpu.roll(cs, s, axis=0), 0.0)
        s *= 2
    cs_t = cs.T
    cs_last = cs[t - 1:t, :]
    decay_to_end = jnp.exp(cs_last - cs)
    exp_cs = jnp.exp(cs)
    chunk_decay = jnp.exp(cs_last)

    ii = lax.broadcasted_iota(jnp.int32, (t, t), 0)
    jj = lax.broadcasted_iota(jnp.int32, (t, t), 1)
    causal = jj <= ii
    first_head = lane < SSD_HEAD_DIM
    first_head_rows = rows < SSD_HEAD_DIM

    for g in range(SSD_GROUPS):
        bmat = _silu(conv(D_RG + D_SSD + g * SSD_STATE, SSD_STATE)).astype(BF16)
        cmat = _silu(conv(D_RG + D_SSD + SSD_GN + g * SSD_STATE, SSD_STATE)).astype(BF16)
        cb = lax.dot_general(cmat, bmat, NT_DIMS, preferred_element_type=F32)
        for pair in range(SSD_HEADS // SSD_GROUPS // 2):
            h0 = g * (SSD_HEADS // SSD_GROUPS) + 2 * pair
            lo = h0 * SSD_HEAD_DIM

            def per_head(x2d):
                return jnp.where(first_head, x2d[:, h0:h0 + 1], x2d[:, h0 + 1:h0 + 2])

            xs = _silu(conv(D_RG + lo, LANES))
            xdt = xs * per_head(dt)
            xdt_b = xdt.astype(BF16)
            y_heads = []
            for hh in (h0, h0 + 1):
                seg = cs[:, hh:hh + 1] - cs_t[hh:hh + 1, :]
                lmat = jnp.exp(jnp.where(causal, seg, -jnp.inf))
                y_heads.append(jnp.dot((cb * lmat).astype(BF16), xdt_b, preferred_element_type=F32))
            y = jnp.where(first_head, y_heads[0], y_heads[1])
            s_prev = state_ref[lo:lo + LANES, :]
            y = y + lax.dot_general(cmat, s_prev.astype(BF16), NT_DIMS,
                                    preferred_element_type=F32) * per_head(exp_cs)
            y_scr[:, lo:lo + LANES] = y + dskip_ref[:, lo:lo + LANES] * xs
            decay_rows = jnp.where(first_head_rows, chunk_decay[:, h0:h0 + 1], chunk_decay[:, h0 + 1:h0 + 2])
            state_ref[lo:lo + LANES, :] = s_prev * decay_rows + lax.dot_general(
                (xdt * per_head(decay_to_end)).astype(BF16), bmat, TN_DIMS, preferred_element_type=F32)

    gw = D_SSD // SSD_GROUPS
    for g in range(SSD_GROUPS):
        lo = g * gw
        z_ref = z0_ref if g < SSD_GROUPS // 2 else z1_ref
        zlo = lo % (D_SSD // 2)
        yz = y_scr[:, lo:lo + gw] * _silu(z_ref[:, zlo:zlo + gw])
        ms = jnp.mean(yz * yz, axis=-1, keepdims=True)
        ymix_ref[:, D_RG + lo:D_RG + lo + gw] = (yz * lax.rsqrt(ms + RMS_EPS)
                                                 * normw_ref[:, lo:lo + gw]).astype(BF16)


def _pad_lanes(v):
    return jnp.zeros((1, LANES), F32).at[0, :v.shape[0]].set(v)


def mixer_prompt(proj, nb, conv_w, conv_b, wa, wx, ba, bx, lam, dt_bias, a_log, d_skip, norm_w):
    m = proj.shape[0]
    l = m // nb
    t = SSD_CHUNK
    nc = l // t
    wide = 1024
    row_blk = lambda bi, ci: bi * nc + ci
    full = lambda shape: pl.BlockSpec(shape, lambda bi, ci: (0,) * len(shape))
    gate_blk = D_CONV // wide
    in_specs = [pl.BlockSpec((t, D_CONV), lambda bi, ci: (row_blk(bi, ci), 0))]
    in_specs += [pl.BlockSpec((t, wide), functools.partial(lambda j, bi, ci: (bi * nc + ci, j), gate_blk + j))
                 for j in range(4)]
    in_specs += [pl.BlockSpec((t, LANES), lambda bi, ci: (row_blk(bi, ci), (D_IN - SSD_HEADS) // LANES))]
    in_specs += [full((CONV_W, D_CONV)), full((1, D_CONV)),
                 full((RG_HEADS, RG_BW, RG_BW)), full((RG_HEADS, RG_BW, RG_BW)),
                 full((1, D_RG)), full((1, D_RG)), full((1, D_RG)),
                 full((1, LANES)), full((1, LANES)), full((1, D_SSD)), full((1, D_SSD))]
    ymix, hlast, state = pl.pallas_call(
        _mixer_prompt_kernel,
        grid=(nb, nc),
        in_specs=in_specs,
        out_specs=[pl.BlockSpec((t, D_MIX), lambda bi, ci: (row_blk(bi, ci), 0)),
                   pl.BlockSpec((None, 1, D_RG), lambda bi, ci: (bi, 0, 0)),
                   pl.BlockSpec((None, D_SSD, SSD_STATE), lambda bi, ci: (bi, 0, 0))],
        out_shape=[jax.ShapeDtypeStruct((m, D_MIX), BF16),
                   jax.ShapeDtypeStruct((nb, 1, D_RG), F32),
                   jax.ShapeDtypeStruct((nb, D_SSD, SSD_STATE), F32)],
        scratch_shapes=[pltpu.VMEM((t + 2 * SUBLANES, D_CONV), F32),
                        pltpu.VMEM((t, D_RG), F32), pltpu.VMEM((t, D_RG), F32),
                        pltpu.VMEM((t, D_SSD), F32), pltpu.VMEM((1, D_RG), F32)],
        compiler_params=_params("parallel", "arbitrary"),
        name="mixer_prompt",
    )(proj, proj, proj, proj, proj, proj,
      conv_w, conv_b.reshape(1, D_CONV), wa.astype(BF16), wx.astype(BF16),
      ba.reshape(1, D_RG), bx.reshape(1, D_RG), lam.reshape(1, D_RG),
      _pad_lanes(dt_bias), _pad_lanes(a_log),
      jnp.repeat(d_skip, SSD_HEAD_DIM).reshape(1, D_SSD), norm_w.reshape(1, D_SSD))
    return ymix, hlast.reshape(nb, D_RG), state


def _moba_prompt_kernel(slopes_ref, q_ref, k_ref, v_ref, o_ref, kb_ref, vt_ref, pen_ref):
    blk = MOBA_BLOCK
    seq = k_ref.shape[0]
    nblk = seq // blk
    slope = slopes_ref[pl.program_id(1)]
    scale = HEAD_DIM ** -0.5

    kmeans = []
    for n in range(nblk):
        rows = slice(n * blk, (n + 1) * blk)
        kn = k_ref[rows, :]
        kmeans.append(jnp.mean(kn, axis=0, keepdims=True))
        kb_ref[rows, :] = kn.astype(BF16)
        vt_ref[:, rows] = v_ref[rows, :].T.astype(BF16)
    kmean = jnp.concatenate(kmeans, axis=0)
    pen_ref[...] = slope * (lax.broadcasted_iota(jnp.int32, (seq, blk), 1)
                            - lax.broadcasted_iota(jnp.int32, (seq, blk), 0)).astype(F32)
    blk_id = lax.broadcasted_iota(jnp.int32, (nblk, blk), 0)
    causal = (lax.broadcasted_iota(jnp.int32, (blk, blk), 0) <= lax.broadcasted_iota(jnp.int32, (blk, blk), 1))

    for qi in range(nblk):
        q = q_ref[qi * blk:(qi + 1) * blk, :]
        qb = q.astype(BF16)
        base = -slope * float(qi * blk)
        if qi > MOBA_TOPK:
            gate = lax.dot_general(kmean, q, NT_DIMS, precision=lax.Precision.HIGHEST,
                                   preferred_element_type=F32)
            gate = jnp.where(blk_id < qi, gate, -jnp.inf)
            ahead = jnp.zeros((nblk, blk), jnp.int32)
            for n in range(qi):
                gate_n = gate[n:n + 1, :]
                before = jnp.logical_or(gate_n > gate, jnp.logical_and(gate_n == gate, n < blk_id))
                ahead = ahead + before.astype(jnp.int32)
            row_terms = jnp.where(ahead < MOBA_TOPK, base, -jnp.inf)
        else:
            row_terms = jnp.full((nblk, blk), base, F32)

        s_blocks = []
        for n in range(qi + 1):
            rows = slice(n * blk, (n + 1) * blk)
            s = lax.dot_general(kb_ref[rows, :], qb, NT_DIMS, preferred_element_type=F32)
            s = s * scale - pen_ref[rows, :]
            if n < qi:
                s = s + row_terms[n:n + 1, :]
            else:
                s = jnp.where(causal, s + base, -jnp.inf)
            s_blocks.append(s)
        m = jnp.max(s_blocks[0], axis=0, keepdims=True)
        for s in s_blocks[1:]:
            m = jnp.maximum(m, jnp.max(s, axis=0, keepdims=True))
        denom = jnp.zeros((1, blk), F32)
        acc = jnp.zeros((HEAD_DIM, blk), F32)
        for n, s in enumerate(s_blocks):
            p = jnp.exp(s - m)
            denom = denom + jnp.sum(p, axis=0, keepdims=True)
            acc = acc + jnp.dot(vt_ref[:, n * blk:(n + 1) * blk], p.astype(BF16), preferred_element_type=F32)
        o_ref[qi * blk:(qi + 1) * blk, :] = (acc / denom).T.astype(o_ref.dtype)


def alibi_slopes():
    return jnp.exp2(-8.0 * jnp.arange(1, N_HEADS + 1, dtype=F32) / N_HEADS)


def moba_prompt(q, k, v, nb):
    m = q.shape[0]
    l = m // nb
    head_spec = pl.BlockSpec((l, HEAD_DIM), lambda bi, hi: (bi, hi))
    return pl.pallas_call(
        _moba_prompt_kernel,
        grid=(nb, N_HEADS),
        in_specs=[pl.BlockSpec(memory_space=pltpu.SMEM), head_spec, head_spec, head_spec],
        out_specs=head_spec,
        out_shape=jax.ShapeDtypeStruct((m, D_MODEL), BF16),
        scratch_shapes=[pltpu.VMEM((l, HEAD_DIM), BF16), pltpu.VMEM((HEAD_DIM, l), BF16),
                        pltpu.VMEM((l, MOBA_BLOCK), F32)],
        compiler_params=_params("parallel", "parallel"),
        name="moba_prompt",
    )(alibi_slopes(), q, k, v)


def _mixer_step_a_kernel(proj_ref, b0_ref, b1_ref, b2_ref, h0_ref,
                         convw_ref, convb_ref, wa_ref, wx_ref, ba_ref, bx_ref, lam_ref, dtb_ref,
                         yrg_ref, hnew_ref, xbc_ref, dt_ref):
    conv = (convb_ref[...] + convw_ref[0:1, :] * b0_ref[...] + convw_ref[1:2, :] * b1_ref[...]
            + convw_ref[2:3, :] * b2_ref[...] + convw_ref[3:4, :] * proj_ref[:, 0:D_CONV])
    xbc_ref[...] = _silu(conv[:, D_RG:])
    dt_ref[...] = _softplus(proj_ref[:, D_IN - SSD_HEADS:D_IN] + dtb_ref[...])
    for hd in range(RG_HEADS):
        lo = hd * RG_BW
        u = conv[:, lo:lo + RG_BW]
        ub = u.astype(BF16)
        r = jax.nn.sigmoid(jnp.dot(ub, wa_ref[hd], preferred_element_type=F32) + ba_ref[:, lo:lo + RG_BW])
        i = jax.nn.sigmoid(jnp.dot(ub, wx_ref[hd], preferred_element_type=F32) + bx_ref[:, lo:lo + RG_BW])
        log_a = -RG_C * r * _softplus(-lam_ref[:, lo:lo + RG_BW])
        a = jnp.exp(log_a)
        h = a * h0_ref[:, lo:lo + RG_BW] + _sqrt_one_minus_sq(log_a, a) * i * u
        hnew_ref[:, lo:lo + RG_BW] = h
        gate = proj_ref[:, D_CONV + lo:D_CONV + lo + RG_BW]
        yrg_ref[:, lo:lo + RG_BW] = h * jax.nn.gelu(gate)


def _ssd_step_kernel(s0_ref, xs_ref, dt_ref, alog_ref, b_ref, c_ref, snew_ref, y_ref):
    dt = dt_ref[...]
    decay = jnp.exp(dt * (-jnp.exp(alog_ref[...])))
    s_new = s0_ref[...] * decay + (xs_ref[...] * dt) * b_ref[...]
    snew_ref[...] = s_new
    y_ref[...] = jnp.sum(s_new * c_ref[...], axis=-1, keepdims=True)


def _mixer_step_b_kernel(yrg_ref, y_ref, xs_ref, z_ref, dskip_ref, normw_ref, o_ref):
    o_ref[:, 0:D_RG] = yrg_ref[...].astype(BF16)
    yz = (y_ref[...] + dskip_ref[...] * xs_ref[...]) * _silu(z_ref[...])
    gw = D_SSD // SSD_GROUPS
    for g in range(SSD_GROUPS):
        blk = yz[:, g * gw:(g + 1) * gw]
        ms = jnp.mean(blk * blk, axis=-1, keepdims=True)
        o_ref[:, D_RG + g * gw:D_RG + (g + 1) * gw] = (
            blk * lax.rsqrt(ms + RMS_EPS) * normw_ref[:, g * gw:(g + 1) * gw]).astype(BF16)


def _whole(shape):
    return pl.BlockSpec(shape, lambda *_: (0,) * len(shape))


def mixer_sample(proj, conv_buf, h0, s0, conv_w, conv_b, wa, wx, ba, bx, lam, dt_bias, a_log, d_skip, norm_w):
    nb = proj.shape[0]
    hpg = SSD_HEADS // SSD_GROUPS
    a_ins = (proj, conv_buf[:, 0], conv_buf[:, 1], conv_buf[:, 2], h0,
             conv_w, conv_b.reshape(1, D_CONV), wa.astype(BF16), wx.astype(BF16),
             ba.reshape(1, D_RG), bx.reshape(1, D_RG), lam.reshape(1, D_RG), dt_bias.reshape(1, SSD_HEADS))
    yrg, hnew, xbc, dt = pl.pallas_call(
        _mixer_step_a_kernel,
        grid=(1,),
        in_specs=[_whole(a.shape) for a in a_ins],
        out_specs=[_whole((nb, D_RG)), _whole((nb, D_RG)), _whole((nb, D_XBC)), _whole((nb, SSD_HEADS))],
        out_shape=[jax.ShapeDtypeStruct((nb, D_RG), F32), jax.ShapeDtypeStruct((nb, D_RG), F32),
                   jax.ShapeDtypeStruct((nb, D_XBC), F32), jax.ShapeDtypeStruct((nb, SSD_HEADS), F32)],
        compiler_params=_params("arbitrary"),
        name="mixer_step_a",
    )(*a_ins)
    xs = xbc[:, :D_SSD]
    snew, y = pl.pallas_call(
        _ssd_step_kernel,
        grid=(nb, SSD_GROUPS),
        in_specs=[pl.BlockSpec((None, hpg, SSD_HEAD_DIM, SSD_STATE), lambda bi, gi: (bi, gi, 0, 0)),
                  pl.BlockSpec((None, hpg, SSD_HEAD_DIM, 1), lambda bi, gi: (bi, gi, 0, 0)),
                  pl.BlockSpec((None, hpg, 1, 1), lambda bi, gi: (bi, gi, 0, 0)),
                  pl.BlockSpec((hpg, 1, 1), lambda bi, gi: (gi, 0, 0)),
                  pl.BlockSpec((None, None, 1, SSD_STATE), lambda bi, gi: (bi, gi, 0, 0)),
                  pl.BlockSpec((None, None, 1, SSD_STATE), lambda bi, gi: (bi, gi, 0, 0))],
        out_specs=[pl.BlockSpec((None, hpg, SSD_HEAD_DIM, SSD_STATE), lambda bi, gi: (bi, gi, 0, 0)),
                   pl.BlockSpec((None, hpg, SSD_HEAD_DIM, 1), lambda bi, gi: (bi, gi, 0, 0))],
        out_shape=[jax.ShapeDtypeStruct((nb, SSD_HEADS, SSD_HEAD_DIM, SSD_STATE), F32),
                   jax.ShapeDtypeStruct((nb, SSD_HEADS, SSD_HEAD_DIM, 1), F32)],
        compiler_params=_params("parallel", "parallel"),
        name="ssd_step",
    )(s0, xs.reshape(nb, SSD_HEADS, SSD_HEAD_DIM, 1), dt.reshape(nb, SSD_HEADS, 1, 1),
      a_log.reshape(SSD_HEADS, 1, 1),
      xbc[:, D_SSD:D_SSD + SSD_GN].reshape(nb, SSD_GROUPS, 1, SSD_STATE),
      xbc[:, D_SSD + SSD_GN:].reshape(nb, SSD_GROUPS, 1, SSD_STATE))
    b_ins = (yrg, y.reshape(nb, D_SSD), xs, proj[:, D_CONV + D_RG:D_CONV + D_RG + D_SSD],
             jnp.repeat(d_skip, SSD_HEAD_DIM).reshape(1, D_SSD), norm_w.reshape(1, D_SSD))
    ymix = pl.pallas_call(
        _mixer_step_b_kernel,
        grid=(1,),
        in_specs=[_whole(a.shape) for a in b_ins],
        out_specs=_whole((nb, D_MIX)),
        out_shape=jax.ShapeDtypeStruct((nb, D_MIX), BF16),
        compiler_params=_params("arbitrary"),
        name="mixer_step_b",
    )(*b_ins)
    return ymix, hnew, snew


def _moba_topk_kernel(q_ref, ksum_ref, idx_ref):
    nblk = ksum_ref.shape[0]
    qh = q_ref[...] * (1.0 / MOBA_BLOCK)
    lane = lax.broadcasted_iota(jnp.int32, (N_HEADS, LANES), 1)
    gate = jnp.full((N_HEADS, LANES), -jnp.inf, F32)
    for n in range(nblk):
        gate = jnp.where(lane == n, jnp.sum(ksum_ref[n] * qh, axis=-1, keepdims=True), gate)
    out = jnp.zeros((N_HEADS, LANES), jnp.int32)
    for kk in range(MOBA_TOPK):
        best = jnp.max(gate, axis=-1, keepdims=True)
        pick = jnp.min(jnp.where(gate == best, lane, LANES), axis=-1, keepdims=True)
        out = jnp.where(lane == kk, pick, out)
        gate = jnp.where(lane == pick, -jnp.inf, gate)
    idx_ref[...] = out


def moba_topk(q, ksum):
    nb, nblk = ksum.shape[:2]
    assert MOBA_TOPK <= nblk <= LANES
    return pl.pallas_call(
        _moba_topk_kernel,
        grid=(nb,),
        in_specs=[pl.BlockSpec((None, N_HEADS, HEAD_DIM), lambda bi: (bi, 0, 0)),
                  pl.BlockSpec((None, nblk, N_HEADS, HEAD_DIM), lambda bi: (bi, 0, 0, 0))],
        out_specs=pl.BlockSpec((None, N_HEADS, LANES), lambda bi: (bi, 0, 0)),
        out_shape=jax.ShapeDtypeStruct((nb, N_HEADS, LANES), jnp.int32),
        compiler_params=_params("parallel"),
        name="moba_topk",
    )(q, ksum)


def _moba_decode_kernel(pt_ref, top_ref, slopes_ref, q_ref, kn_ref, vn_ref, ck_hbm, cv_hbm, o_ref,
                        kbuf, vbuf, sem, *, past_len):
    bi = pl.program_id(0)
    scale = HEAD_DIM ** -0.5
    slots = MOBA_TOPK * PAGES_PER_BLOCK

    def head_copies(hd):
        copies = []
        for kk in range(MOBA_TOPK):
            blk_id = top_ref[bi, hd, kk]
            for half in range(PAGES_PER_BLOCK):
                page = pt_ref[bi, PAGES_PER_BLOCK * blk_id + half]
                slot = kk * PAGES_PER_BLOCK + half
                copies.append(pltpu.make_async_copy(ck_hbm.at[0, page, :, hd, :], kbuf.at[hd, slot],
                                                    sem.at[0, hd, slot]))
                copies.append(pltpu.make_async_copy(cv_hbm.at[0, page, :, hd, :], vbuf.at[hd, slot],
                                                    sem.at[1, hd, slot]))
        return copies

    all_copies = [head_copies(hd) for hd in range(N_HEADS)]
    for copies in all_copies:
        for cp in copies:
            cp.start()

    key_off = lax.broadcasted_iota(jnp.int32, (1, MOBA_BLOCK), 1)
    for hd in range(N_HEADS):
        for cp in all_copies[hd]:
            cp.wait()
        lo = hd * HEAD_DIM
        slope = slopes_ref[hd]
        q = q_ref[:, lo:lo + HEAD_DIM]
        q8 = jnp.broadcast_to(q, (SUBLANES, HEAD_DIM)).astype(BF16)
        kh = kbuf[hd].reshape(slots * PAGE_SIZE, HEAD_DIM).astype(BF16)
        vh = vbuf[hd].reshape(slots * PAGE_SIZE, HEAD_DIM).astype(BF16)
        dist = jnp.concatenate(
            [(past_len - top_ref[bi, hd, kk] * MOBA_BLOCK - key_off).astype(F32) for kk in range(MOBA_TOPK)],
            axis=1)
        s = lax.dot_general(q8, kh, NT_DIMS, preferred_element_type=F32)[0:1, :] * scale - slope * dist
        s_own = jnp.sum(q * kn_ref[:, lo:lo + HEAD_DIM], axis=-1, keepdims=True) * scale
        m = jnp.maximum(jnp.max(s, axis=-1, keepdims=True), s_own)
        p = jnp.exp(s - m)
        p_own = jnp.exp(s_own - m)
        denom = jnp.sum(p, axis=-1, keepdims=True) + p_own
        p8 = jnp.broadcast_to(p, (SUBLANES, slots * PAGE_SIZE)).astype(BF16)
        acc = jnp.dot(p8, vh, preferred_element_type=F32)[0:1, :] + p_own * vn_ref[:, lo:lo + HEAD_DIM]
        o_ref[:, lo:lo + HEAD_DIM] = acc / denom


def moba_decode(q, k_new, v_new, cache_k, cache_v, page_table, top_idx):
    nb, n_pages = page_table.shape
    slots = MOBA_TOPK * PAGES_PER_BLOCK
    row_spec = pl.BlockSpec((None, 1, D_MODEL), lambda bi, pt, top: (bi, 0, 0))
    return pl.pallas_call(
        functools.partial(_moba_decode_kernel, past_len=n_pages * PAGE_SIZE),
        grid_spec=pltpu.PrefetchScalarGridSpec(
            num_scalar_prefetch=2,
            grid=(nb,),
            in_specs=[pl.BlockSpec(memory_space=pltpu.SMEM), row_spec, row_spec, row_spec,
                      pl.BlockSpec(memory_space=pl.ANY), pl.BlockSpec(memory_space=pl.ANY)],
            out_specs=row_spec,
            scratch_shapes=[pltpu.VMEM((N_HEADS, slots, PAGE_SIZE, HEAD_DIM), F32),
                            pltpu.VMEM((N_HEADS, slots, PAGE_SIZE, HEAD_DIM), F32),
                            pltpu.SemaphoreType.DMA((2, N_HEADS, slots))]),
        out_shape=jax.ShapeDtypeStruct((nb, 1, D_MODEL), F32),
        compiler_params=_params("arbitrary"),
        name="moba_decode",
    )(page_table, top_idx, alibi_slopes(), q, k_new, v_new, cache_k, cache_v)


def _trunk_prompt(x, w, nb, cache_k, page_table):
    tm = 512
    n_seq = page_table.shape[0]
    half = n_seq // DEPTH
    proj = matmul(x, w["w_in"], tm=1024, tn=1024)
    l = x.shape[0] // nb
    conv_tail = proj.reshape(nb, l, D_IN_PAD)[:, l - (CONV_W - 1):, :D_CONV]
    ymix, h_last, s_last = mixer_prompt(proj, nb, *w["mixer"])
    x = matmul_ln(ymix, w["w_out"], x, w["ln1_g"][0], w["ln1_b"][0], tm=tm, tk=2048)
    x, ksum0 = mlp_ln_ksum(x, w["w_up"], w["w_down"], 0, w["ln2_g"][0], w["ln2_b"][0],
                           cache_k, page_table, 0, half, tm=tm, tf=1024)
    q, k, v = qkv_proj(x, w["w_qkv"], tm=tm)
    o = moba_prompt(q, k, v, nb)
    x = matmul_ln(o, w["w_o"], x, w["ln1_g"][1], w["ln1_b"][1], tm=tm, tk=2048)
    x, ksum1 = mlp_ln_ksum(x, w["w_up"], w["w_down"], 1, w["ln2_g"][1], w["ln2_b"][1],
                           cache_k, page_table, half, n_seq - half, tm=tm, tf=1024)
    return x, conv_tail, h_last, s_last, k, v, jnp.concatenate([ksum0, ksum1], axis=0)


def _trunk_sample(x, conv_buf, h0, s0, cache_k, cache_v, page_table, ksum, w):
    nb = x.shape[0]
    proj = matmul(x, w["w_in"], tm=nb, tn=1024)
    new_buf = jnp.concatenate([conv_buf[:, 1:], proj[:, None, :D_CONV]], axis=1)
    ymix, h_new, s_new = mixer_sample(proj, conv_buf, h0, s0, *w["mixer"])
    x = matmul_ln(ymix, w["w_out"], x, w["ln1_g"][0], w["ln1_b"][0], tm=nb, tk=2048)
    x = mlp_ln(x, w["w_up"], w["w_down"], 0, w["ln2_g"][0], w["ln2_b"][0], tm=nb, tf=1024)
    q, k, v = qkv_proj(x, w["w_qkv"], tm=nb)
    top_idx = moba_topk(q.reshape(nb, N_HEADS, HEAD_DIM), ksum)[:, :, :MOBA_TOPK]
    o = moba_decode(q.reshape(nb, 1, D_MODEL), k.reshape(nb, 1, D_MODEL), v.reshape(nb, 1, D_MODEL),
                    cache_k, cache_v, page_table, top_idx).reshape(nb, D_MODEL)
    x = matmul_ln(o, w["w_o"], x, w["ln1_g"][1], w["ln1_b"][1], tm=nb, tk=2048)
    x = mlp_ln(x, w["w_up"], w["w_down"], 1, w["ln2_g"][1], w["ln2_b"][1], tm=nb, tf=1024)
    return x, new_buf, h_new, s_new, k, v


def kernel(x_prompt, x_sample, state_conv, state_rglru, state_ssd, cache_k, cache_v, page_table, w_in, conv_w, conv_b, rg_wa, rg_ba, rg_wx, rg_bx, rg_lambda, ssd_dt_bias, ssd_a_log, ssd_d, ssd_norm_w, w_out, w_qkv, w_o, ln1_g, ln1_b, ln2_g, ln2_b, w_up, w_down):
    bp, lp, _ = x_prompt.shape
    bs = x_sample.shape[0]
    w = {
        "w_in": jnp.concatenate([w_in[0].astype(BF16), jnp.zeros((D_MODEL, D_IN_PAD - D_IN), BF16)], axis=1),
        "w_out": w_out[0].astype(BF16), "w_qkv": w_qkv[0].astype(BF16), "w_o": w_o[0].astype(BF16),
        "w_up": w_up.astype(BF16), "w_down": w_down.astype(BF16),
        "ln1_g": ln1_g, "ln1_b": ln1_b, "ln2_g": ln2_g, "ln2_b": ln2_b,
        "mixer": (conv_w[0], conv_b[0], rg_wa[0], rg_wx[0], rg_ba[0], rg_bx[0], rg_lambda[0],
                  ssd_dt_bias[0], ssd_a_log[0], ssd_d[0], ssd_norm_w[0]),
    }
    yp, conv_p, h_p, s_p, k_p, v_p, ksum = _trunk_prompt(x_prompt.reshape(bp * lp, D_MODEL), w, bp,
                                                         cache_k, page_table)
    ys, conv_s, h_s, s_s, k_s, v_s = _trunk_sample(
        x_sample.reshape(bs, D_MODEL), state_conv[0], state_rglru[0], state_ssd[0],
        cache_k, cache_v, page_table, ksum, w)
    return (yp.reshape(bp, lp, D_MODEL), ys.reshape(bs, 1, D_MODEL),
            conv_p[None], h_p[None], s_p.reshape(1, bp, SSD_HEADS, SSD_HEAD_DIM, SSD_STATE),
            k_p.reshape(1, bp, lp, N_HEADS, HEAD_DIM), v_p.reshape(1, bp, lp, N_HEADS, HEAD_DIM),
            conv_s[None], h_s[None], s_s[None],
            k_s.reshape(1, bs, 1, N_HEADS, HEAD_DIM), v_s.reshape(1, bs, 1, N_HEADS, HEAD_DIM))
```

```python
import functools
import math

import jax
import jax.numpy as jnp
from jax import lax
from jax.experimental import pallas as pl
from jax.experimental.pallas import tpu as pltpu

F32 = jnp.float32
BF16 = jnp.bfloat16

D_MODEL = 2048
DEPTH = 2
PAGE_SIZE = 128
CONV_W = 4
D_RG = D_MODEL
RG_HEADS = 8
RG_BW = D_RG // RG_HEADS
RG_C = 8.0
D_SSD = D_MODEL
SSD_HEAD_DIM = 64
SSD_HEADS = D_SSD // SSD_HEAD_DIM
SSD_GROUPS = 4
SSD_STATE = 128
SSD_CHUNK = 128
SSD_GN = SSD_GROUPS * SSD_STATE
D_XBC = D_SSD + 2 * SSD_GN
D_CONV = D_RG + D_XBC
D_IN = D_CONV + D_RG + D_SSD + SSD_HEADS
D_MIX = D_RG + D_SSD
N_HEADS = 16
HEAD_DIM = D_MODEL // N_HEADS
MOBA_BLOCK = 256
MOBA_TOPK = 3
PAGES_PER_BLOCK = MOBA_BLOCK // PAGE_SIZE
KSUM_BLOCKS_PER_STEP = 2
D_FF = 4 * D_MODEL
ALPHA = (2 * DEPTH) ** 0.25
LN_EPS = 1e-5
RMS_EPS = 1e-6

SUBLANES = 8
LANES = 128
VMEM_LIMIT_BYTES = 56 * 1024 * 1024

NT_DIMS = (((1,), (1,)), ((), ()))
TN_DIMS = (((0,), (0,)), ((), ()))


def _params(*semantics):
    return pltpu.CompilerParams(dimension_semantics=semantics, vmem_limit_bytes=VMEM_LIMIT_BYTES)


def _bdot(a, b):
    return jnp.dot(a.astype(BF16), b.astype(BF16), preferred_element_type=F32)


def _layer_norm_rows(y, g, b):
    mu = jnp.mean(y, axis=-1, keepdims=True)
    yc = y - mu
    var = jnp.mean(yc * yc, axis=-1, keepdims=True)
    return yc * lax.rsqrt(var + LN_EPS) * g + b


def _softplus(x):
    u = jnp.exp(-jnp.abs(x))
    w = 1.0 + u
    log1p_u = jnp.where(w == 1.0, u, jnp.log(w) * (u / (w - 1.0)))
    return jnp.maximum(x, 0.0) + log1p_u


def _sqrt_one_minus_sq(log_a, a):
    return jnp.sqrt(-jnp.tanh(log_a) * (a * a + 1.0))


def _silu(x):
    return x * jax.nn.sigmoid(x)


def _matmul_kernel(a_ref, w_ref, o_ref, ab_ref):
    @pl.when(pl.program_id(1) == 0)
    def _():
        ab_ref[...] = a_ref[...].astype(BF16)

    o_ref[...] = jnp.dot(ab_ref[...], w_ref[...], preferred_element_type=F32)


IN_PROJ_COL_CHUNK = 256
IN_PROJ_ROW_CHUNK = 64


def _in_proj_kernel(*refs, kind, blocks_per_seq):
    has_conv = kind in ("conv", "conv_silu")
    if has_conv:
        x_ref, w_ref, cw_ref, cb_ref, o_ref, tail_ref, xb_ref, ybuf, carry = refs
    else:
        x_ref, w_ref, o_ref, xb_ref = refs
    i = pl.program_id(0)
    j = pl.program_id(1)
    tm, tn = o_ref.shape
    front = SUBLANES

    @pl.when(j == 0)
    def _():
        xb_ref[...] = x_ref[...].astype(BF16)

    if has_conv:
        @pl.when(i % blocks_per_seq == 0)
        def _():
            ybuf[0:front, :] = jnp.zeros((front, tn), F32)

        @pl.when(i % blocks_per_seq != 0)
        def _():
            ybuf[0:front, :] = carry[j]

    for c0 in range(0, tn, IN_PROJ_COL_CHUNK):
        cols = slice(c0, c0 + IN_PROJ_COL_CHUNK)
        y = jnp.dot(xb_ref[...], w_ref[:, cols], preferred_element_type=F32)
        if has_conv:
            ybuf[front:front + tm, cols] = y
        for r0 in range(0, tm, IN_PROJ_ROW_CHUNK):
            rows = slice(r0, r0 + IN_PROJ_ROW_CHUNK)
            if has_conv:
                ext = ybuf[r0:r0 + front + IN_PROJ_ROW_CHUNK, cols]
                acc = cb_ref[:, cols] + cw_ref[CONV_W - 1:CONV_W, cols] * ext[front:, :]
                for k in range(CONV_W - 1):
                    shifted = pltpu.roll(ext, CONV_W - 1 - k, axis=0)[front:, :]
                    acc = acc + cw_ref[k:k + 1, cols] * shifted
                o_ref[rows, cols] = _silu(acc) if kind == "conv_silu" else acc
            elif kind == "gelu":
                o_ref[rows, cols] = jax.nn.gelu(y[rows, :])
            else:
                o_ref[rows, cols] = _silu(y[rows, :])

    if has_conv:
        last = ybuf[tm:tm + front, :]
        carry[j] = last
        tail_ref[...] = last


def in_proj(x, w, col0, n_out, kind, nb, conv_w=None, conv_b=None, *, tm, tn):
    m, k = x.shape
    off = col0 // tn
    grid = (m // tm, n_out // tn)
    blocks_per_seq = (m // nb) // tm
    has_conv = kind in ("conv", "conv_silu")
    in_specs = [pl.BlockSpec((tm, k), lambda i, j: (i, 0)),
                pl.BlockSpec((k, tn), lambda i, j: (0, j + off))]
    out_specs = [pl.BlockSpec((tm, tn), lambda i, j: (i, j))]
    out_shape = [jax.ShapeDtypeStruct((m, n_out), F32)]
    scratch = [pltpu.VMEM((tm, k), BF16)]
    operands = [x, w]
    if has_conv:
        in_specs += [pl.BlockSpec((CONV_W, tn), lambda i, j: (0, j + off)),
                     pl.BlockSpec((1, tn), lambda i, j: (0, j + off))]
        out_specs += [pl.BlockSpec((None, SUBLANES, tn), lambda i, j: (i, 0, j))]
        out_shape += [jax.ShapeDtypeStruct((grid[0], SUBLANES, n_out), F32)]
        scratch += [pltpu.VMEM((tm + SUBLANES, tn), F32), pltpu.VMEM((grid[1], SUBLANES, tn), F32)]
        operands += [conv_w, conv_b]
    outs = pl.pallas_call(
        functools.partial(_in_proj_kernel, kind=kind, blocks_per_seq=blocks_per_seq),
        grid=grid,
        in_specs=in_specs,
        out_specs=out_specs,
        out_shape=out_shape,
        scratch_shapes=scratch,
        compiler_params=_params("arbitrary", "arbitrary"),
        name="in_proj_" + kind,
    )(*operands)
    if not has_conv:
        return outs[0]
    return outs[0], outs[1][blocks_per_seq - 1::blocks_per_seq]


def matmul(a, w, *, tm, tn, n_out=None):
    m, k = a.shape
    n = w.shape[1] if n_out is None else n_out
    return pl.pallas_call(
        _matmul_kernel,
        grid=(pl.cdiv(m, tm), pl.cdiv(n, tn)),
        in_specs=[pl.BlockSpec((tm, k), lambda i, j: (i, 0)),
                  pl.BlockSpec((k, tn), lambda i, j: (0, j))],
        out_specs=pl.BlockSpec((tm, tn), lambda i, j: (i, j)),
        out_shape=jax.ShapeDtypeStruct((m, n), F32),
        scratch_shapes=[pltpu.VMEM((tm, k), BF16)],
        compiler_params=_params("parallel", "arbitrary"),
        name="matmul",
    )(a, w)


def _qkv_kernel(a_ref, w_ref, q_ref, k_ref, v_ref, ab_ref):
    @pl.when(pl.program_id(1) == 0)
    def _():
        ab_ref[...] = a_ref[...].astype(BF16)

    y = jnp.dot(ab_ref[...], w_ref[...], preferred_element_type=F32)
    for part, o_ref in enumerate((q_ref, k_ref, v_ref)):
        @pl.when(pl.program_id(1) == part)
        def _(o_ref=o_ref):
            o_ref[...] = y


def qkv_proj(a, w, *, tm):
    m, k = a.shape
    n = w.shape[1] // 3
    out_spec = pl.BlockSpec((tm, n), lambda i, j: (i, 0))
    return pl.pallas_call(
        _qkv_kernel,
        grid=(pl.cdiv(m, tm), 3),
        in_specs=[pl.BlockSpec((tm, k), lambda i, j: (i, 0)),
                  pl.BlockSpec((k, n), lambda i, j: (0, j))],
        out_specs=[out_spec] * 3,
        out_shape=[jax.ShapeDtypeStruct((m, n), F32)] * 3,
        scratch_shapes=[pltpu.VMEM((tm, k), BF16)],
        compiler_params=_params("parallel", "arbitrary"),
        name="qkv_proj",
    )(a, w)


def _matmul_ln_kernel(a_ref, w_ref, res_ref, g_ref, b_ref, o_ref, acc_ref):
    kk = pl.program_id(1)

    @pl.when(kk == 0)
    def _():
        acc_ref[...] = jnp.zeros_like(acc_ref)

    acc_ref[...] += _bdot(a_ref[...], w_ref[...])

    @pl.when(kk == pl.num_programs(1) - 1)
    def _():
        y = ALPHA * res_ref[...] + acc_ref[...]
        o_ref[...] = _layer_norm_rows(y, g_ref[...], b_ref[...])


def _matmul_ln_full_k_kernel(a_ref, w_ref, res_ref, g_ref, b_ref, o_ref):
    y = ALPHA * res_ref[...] + _bdot(a_ref[...], w_ref[...])
    o_ref[...] = _layer_norm_rows(y, g_ref[...], b_ref[...])


def matmul_ln(a, w, res, g, b, *, tm, tk):
    m, k = a.shape
    n = w.shape[1]
    if tk == k:
        return pl.pallas_call(
            _matmul_ln_full_k_kernel,
            grid=(pl.cdiv(m, tm),),
            in_specs=[pl.BlockSpec((tm, k), lambda i: (i, 0)),
                      pl.BlockSpec((k, n), lambda i: (0, 0)),
                      pl.BlockSpec((tm, n), lambda i: (i, 0)),
                      pl.BlockSpec((1, n), lambda i: (0, 0)),
                      pl.BlockSpec((1, n), lambda i: (0, 0))],
            out_specs=pl.BlockSpec((tm, n), lambda i: (i, 0)),
            out_shape=jax.ShapeDtypeStruct((m, n), F32),
            compiler_params=_params("parallel"),
            name="matmul_ln_full_k",
        )(a, w, res, g.reshape(1, n), b.reshape(1, n))
    return pl.pallas_call(
        _matmul_ln_kernel,
        grid=(pl.cdiv(m, tm), k // tk),
        in_specs=[pl.BlockSpec((tm, tk), lambda i, kk: (i, kk)),
                  pl.BlockSpec((tk, n), lambda i, kk: (kk, 0)),
                  pl.BlockSpec((tm, n), lambda i, kk: (i, 0)),
                  pl.BlockSpec((1, n), lambda i, kk: (0, 0)),
                  pl.BlockSpec((1, n), lambda i, kk: (0, 0))],
        out_specs=pl.BlockSpec((tm, n), lambda i, kk: (i, 0)),
        out_shape=jax.ShapeDtypeStruct((m, n), F32),
        scratch_shapes=[pltpu.VMEM((tm, n), F32)],
        compiler_params=_params("parallel", "arbitrary"),
        name="matmul_ln",
    )(a, w, res, g.reshape(1, n), b.reshape(1, n))


def _mlp_ln_body(x_ref, wu_ref, wd_ref, g_ref, b_ref, o_ref, xb_ref, acc_ref):
    f = pl.program_id(1)

    @pl.when(f == 0)
    def _():
        xb_ref[...] = x_ref[...].astype(BF16)
        acc_ref[...] = jnp.zeros_like(acc_ref)

    hdn = jnp.maximum(jnp.dot(xb_ref[...], wu_ref[...], preferred_element_type=F32), 0.0)
    acc_ref[...] += _bdot(hdn * hdn, wd_ref[...])

    @pl.when(f == pl.num_programs(1) - 1)
    def _():
        y = ALPHA * x_ref[...] + acc_ref[...]
        o_ref[...] = _layer_norm_rows(y, g_ref[...], b_ref[...])


def _mlp_ln_kernel(x_ref, wu_ref, wd_ref, g_ref, b_ref, o_ref, xb_ref, acc_ref):
    _mlp_ln_body(x_ref, wu_ref, wd_ref, g_ref, b_ref, o_ref, xb_ref, acc_ref)


def _mlp_ln_ksum_kernel(pt_ref, x_ref, wu_ref, wd_ref, g_ref, b_ref, ck_hbm, o_ref, ksum_ref,
                        xb_ref, acc_ref, kbuf, sem, *, seq0, nblk):
    step = pl.program_id(0) * pl.num_programs(1) + pl.program_id(1)
    n_steps = pl.num_programs(0) * pl.num_programs(1)
    pages_per_step = KSUM_BLOCKS_PER_STEP * PAGES_PER_BLOCK

    def page_copies(at_step, slot):
        copies = []
        for pg in range(pages_per_step):
            flat = at_step * KSUM_BLOCKS_PER_STEP + pg // PAGES_PER_BLOCK
            page = pt_ref[seq0 + flat // nblk, (flat % nblk) * PAGES_PER_BLOCK + pg % PAGES_PER_BLOCK]
            copies.append(pltpu.make_async_copy(ck_hbm.at[0, page], kbuf.at[slot, pg], sem.at[slot, pg]))
        return copies

    @pl.when(step == 0)
    def _():
        for cp in page_copies(0, 0):
            cp.start()

    @pl.when(step + 1 < n_steps)
    def _():
        for cp in page_copies(step + 1, (step + 1) % 2):
            cp.start()

    _mlp_ln_body(x_ref, wu_ref, wd_ref, g_ref, b_ref, o_ref, xb_ref, acc_ref)

    slot = step % 2
    for cp in page_copies(step, slot):
        cp.wait()
    for j in range(KSUM_BLOCKS_PER_STEP):
        acc = jnp.sum(kbuf[slot, j * PAGES_PER_BLOCK], axis=0)
        for p in range(1, PAGES_PER_BLOCK):
            acc = acc + jnp.sum(kbuf[slot, j * PAGES_PER_BLOCK + p], axis=0)
        ksum_ref[j] = acc


def mlp_ln(x, w_up, w_down, layer, g, b, *, tm, tf):
    m, d = x.shape
    ff = w_up.shape[2]
    return pl.pallas_call(
        _mlp_ln_kernel,
        grid=(pl.cdiv(m, tm), ff // tf),
        in_specs=[pl.BlockSpec((tm, d), lambda i, f: (i, 0)),
                  pl.BlockSpec((None, d, tf), lambda i, f: (layer, 0, f)),
                  pl.BlockSpec((None, tf, d), lambda i, f: (layer, f, 0)),
                  pl.BlockSpec((1, d), lambda i, f: (0, 0)),
                  pl.BlockSpec((1, d), lambda i, f: (0, 0))],
        out_specs=pl.BlockSpec((tm, d), lambda i, f: (i, 0)),
        out_shape=jax.ShapeDtypeStruct((m, d), F32),
        scratch_shapes=[pltpu.VMEM((tm, d), BF16), pltpu.VMEM((tm, d), F32)],
        compiler_params=_params("parallel", "arbitrary"),
        name="mlp_ln",
    )(x, w_up, w_down, g.reshape(1, d), b.reshape(1, d))


def mlp_ln_ksum(x, w_up, w_down, layer, g, b, cache_k, page_table, seq0, n_seq, *, tm, tf):
    m, d = x.shape
    ff = w_up.shape[2]
    grid = (m // tm, ff // tf)
    nblk = page_table.shape[1] // PAGES_PER_BLOCK
    assert n_seq * nblk == grid[0] * grid[1] * KSUM_BLOCKS_PER_STEP and nblk % KSUM_BLOCKS_PER_STEP == 0
    steps_per_seq = nblk // KSUM_BLOCKS_PER_STEP

    def ksum_index(i, f, pt):
        step = i * grid[1] + f
        return (step // steps_per_seq, step % steps_per_seq, 0, 0)

    return pl.pallas_call(
        functools.partial(_mlp_ln_ksum_kernel, seq0=seq0, nblk=nblk),
        grid_spec=pltpu.PrefetchScalarGridSpec(
            num_scalar_prefetch=1,
            grid=grid,
            in_specs=[pl.BlockSpec((tm, d), lambda i, f, pt: (i, 0)),
                      pl.BlockSpec((None, d, tf), lambda i, f, pt: (layer, 0, f)),
                      pl.BlockSpec((None, tf, d), lambda i, f, pt: (layer, f, 0)),
                      pl.BlockSpec((1, d), lambda i, f, pt: (0, 0)),
                      pl.BlockSpec((1, d), lambda i, f, pt: (0, 0)),
                      pl.BlockSpec(memory_space=pl.ANY)],
            out_specs=[pl.BlockSpec((tm, d), lambda i, f, pt: (i, 0)),
                       pl.BlockSpec((None, KSUM_BLOCKS_PER_STEP, N_HEADS, HEAD_DIM), ksum_index)],
            scratch_shapes=[pltpu.VMEM((tm, d), BF16), pltpu.VMEM((tm, d), F32),
                            pltpu.VMEM((2, KSUM_BLOCKS_PER_STEP * PAGES_PER_BLOCK, PAGE_SIZE, N_HEADS, HEAD_DIM), F32),
                            pltpu.SemaphoreType.DMA((2, KSUM_BLOCKS_PER_STEP * PAGES_PER_BLOCK))]),
        out_shape=[jax.ShapeDtypeStruct((m, d), F32),
                   jax.ShapeDtypeStruct((n_seq, nblk, N_HEADS, HEAD_DIM), F32)],
        compiler_params=_params("arbitrary", "arbitrary"),
        name="mlp_ln_ksum",
    )(page_table, x, w_up, w_down, g.reshape(1, d), b.reshape(1, d), cache_k)


def _mixer_prompt_kernel(u_ref, xbc_ref, gate_ref, z_ref, x_ref,
                         wdt_ref, wa_ref, wx_ref, ba_ref, bx_ref, lam_ref,
                         dtb_ref, alog_ref, dskip_ref, normw_ref,
                         ymix_ref, hlast_ref, state_ref,
                         a_scr, h_scr, y_scr, decay_rate_scr):
    t = SSD_CHUNK
    c = pl.program_id(1)

    @pl.when(c == 0)
    def _():
        hlast_ref[...] = jnp.zeros_like(hlast_ref)
        state_ref[...] = jnp.zeros_like(state_ref)
        decay_rate_scr[...] = -RG_C * _softplus(-lam_ref[...])

    row = lax.broadcasted_iota(jnp.int32, (t, RG_BW), 0)
    row_in_tile = row % SUBLANES
    is_start = jnp.logical_and(c == 0, row == 0)
    for hd in range(RG_HEADS):
        lo = hd * RG_BW
        u = u_ref[:, lo:lo + RG_BW]
        ub = u.astype(BF16)
        r = jax.nn.sigmoid(jnp.dot(ub, wa_ref[hd], preferred_element_type=F32) + ba_ref[:, lo:lo + RG_BW])
        i = jax.nn.sigmoid(jnp.dot(ub, wx_ref[hd], preferred_element_type=F32) + bx_ref[:, lo:lo + RG_BW])
        log_a = decay_rate_scr[:, lo:lo + RG_BW] * r
        a_raw = jnp.exp(log_a)
        a = jnp.where(is_start, 0.0, a_raw)
        mult = jnp.where(is_start, 1.0, _sqrt_one_minus_sq(log_a, a_raw))
        bt = mult * i * u
        for s in (1, 2, 4):
            keep = row_in_tile >= s
            a_sh = pltpu.roll(a, s, axis=0)
            b_sh = pltpu.roll(bt, s, axis=0)
            bt = jnp.where(keep, a * b_sh + bt, bt)
            a = jnp.where(keep, a * a_sh, a)
        a_scr[:, lo:lo + RG_BW] = a
        h_scr[:, lo:lo + RG_BW] = bt

    def carry_tile(gi, carry):
        r0 = pl.multiple_of(gi * SUBLANES, SUBLANES)
        h = a_scr[pl.ds(r0, SUBLANES), :] * carry + h_scr[pl.ds(r0, SUBLANES), :]
        h_scr[pl.ds(r0, SUBLANES), :] = h
        return h[SUBLANES - 1:SUBLANES, :]

    hlast_ref[...] = lax.fori_loop(0, t // SUBLANES, carry_tile, hlast_ref[...])
    ymix_ref[:, 0:D_RG] = (h_scr[...] * gate_ref[...]).astype(BF16)

    lane = lax.broadcasted_iota(jnp.int32, (t, LANES), 1)
    rows = lax.broadcasted_iota(jnp.int32, (t, LANES), 0)
    dt_raw = jnp.dot(x_ref[...].astype(BF16), wdt_ref[...], preferred_element_type=F32)
    dt = jnp.where(lane < SSD_HEADS, _softplus(dt_raw + dtb_ref[...]), 0.0)
    cs = dt * (-jnp.exp(alog_ref[...]))
    s = 1
    while s < t:
        cs = cs + jnp.where(rows >= s, pltpu.roll(cs, s, axis=0), 0.0)
        s *= 2
    cs_t = cs.T
    cs_last = cs[t - 1:t, :]
    decay_to_end = jnp.exp(cs_last - cs)
    exp_cs = jnp.exp(cs)
    chunk_decay = jnp.exp(cs_last)

    ii = lax.broadcasted_iota(jnp.int32, (t, t), 0)
    jj = lax.broadcasted_iota(jnp.int32, (t, t), 1)
    causal = jj <= ii
    first_head = lane < SSD_HEAD_DIM
    first_head_rows = rows < SSD_HEAD_DIM

    for g in range(SSD_GROUPS):
        b_lo = D_SSD + g * SSD_STATE
        bmat = xbc_ref[:, b_lo:b_lo + SSD_STATE].astype(BF16)
        cmat = xbc_ref[:, b_lo + SSD_GN:b_lo + SSD_GN + SSD_STATE].astype(BF16)
        cb = lax.dot_general(cmat, bmat, NT_DIMS, preferred_element_type=F32)
        for pair in range(SSD_HEADS // SSD_GROUPS // 2):
            h0 = g * (SSD_HEADS // SSD_GROUPS) + 2 * pair
            lo = h0 * SSD_HEAD_DIM

            def per_head(x2d):
                return jnp.where(first_head, x2d[:, h0:h0 + 1], x2d[:, h0 + 1:h0 + 2])

            xs = xbc_ref[:, lo:lo + LANES]
            xdt = xs * per_head(dt)
            xdt_b = xdt.astype(BF16)
            y_heads = []
            for hh in (h0, h0 + 1):
                seg = cs[:, hh:hh + 1] - cs_t[hh:hh + 1, :]
                lmat = jnp.exp(jnp.where(causal, seg, -jnp.inf))
                y_heads.append(jnp.dot((cb * lmat).astype(BF16), xdt_b, preferred_element_type=F32))
            y = jnp.where(first_head, y_heads[0], y_heads[1])
            s_prev = state_ref[lo:lo + LANES, :]
            y = y + lax.dot_general(cmat, s_prev.astype(BF16), NT_DIMS,
                                    preferred_element_type=F32) * per_head(exp_cs)
            y_scr[:, lo:lo + LANES] = y + dskip_ref[:, lo:lo + LANES] * xs
            decay_rows = jnp.where(first_head_rows, chunk_decay[:, h0:h0 + 1], chunk_decay[:, h0 + 1:h0 + 2])
            state_ref[lo:lo + LANES, :] = s_prev * decay_rows + lax.dot_general(
                (xdt * per_head(decay_to_end)).astype(BF16), bmat, TN_DIMS, preferred_element_type=F32)

    gw = D_SSD // SSD_GROUPS
    for g in range(SSD_GROUPS):
        lo = g * gw
        yz = y_scr[:, lo:lo + gw] * z_ref[:, lo:lo + gw]
        ms = jnp.mean(yz * yz, axis=-1, keepdims=True)
        ymix_ref[:, D_RG + lo:D_RG + lo + gw] = (yz * lax.rsqrt(ms + RMS_EPS)
                                                 * normw_ref[:, lo:lo + gw]).astype(BF16)


def _pad_lanes(v):
    return jnp.zeros((1, LANES), F32).at[0, :v.shape[0]].set(v)


def mixer_prompt(u, xbc, gate, z, x, nb, w_dt, wa, wx, ba, bx, lam, dt_bias, a_log, d_skip, norm_w):
    m = u.shape[0]
    l = m // nb
    t = SSD_CHUNK
    nc = l // t
    row_blk = lambda bi, ci: bi * nc + ci
    rows = lambda width: pl.BlockSpec((t, width), lambda bi, ci: (row_blk(bi, ci), 0))
    full = lambda shape: pl.BlockSpec(shape, lambda bi, ci: (0,) * len(shape))
    in_specs = [rows(D_RG), rows(D_XBC), rows(D_RG), rows(D_SSD), rows(D_MODEL)]
    in_specs += [full((D_MODEL, LANES)),
                 full((RG_HEADS, RG_BW, RG_BW)), full((RG_HEADS, RG_BW, RG_BW)),
                 full((1, D_RG)), full((1, D_RG)), full((1, D_RG)),
                 full((1, LANES)), full((1, LANES)), full((1, D_SSD)), full((1, D_SSD))]
    ymix, hlast, state = pl.pallas_call(
        _mixer_prompt_kernel,
        grid=(nb, nc),
        in_specs=in_specs,
        out_specs=[pl.BlockSpec((t, D_MIX), lambda bi, ci: (row_blk(bi, ci), 0)),
                   pl.BlockSpec((None, 1, D_RG), lambda bi, ci: (bi, 0, 0)),
                   pl.BlockSpec((None, D_SSD, SSD_STATE), lambda bi, ci: (bi, 0, 0))],
        out_shape=[jax.ShapeDtypeStruct((m, D_MIX), BF16),
                   jax.ShapeDtypeStruct((nb, 1, D_RG), F32),
                   jax.ShapeDtypeStruct((nb, D_SSD, SSD_STATE), F32)],
        scratch_shapes=[pltpu.VMEM((t, D_RG), F32), pltpu.VMEM((t, D_RG), F32),
                        pltpu.VMEM((t, D_SSD), F32), pltpu.VMEM((1, D_RG), F32)],
        compiler_params=_params("parallel", "arbitrary"),
        name="mixer_prompt",
    )(u, xbc, gate, z, x, w_dt, wa.astype(BF16), wx.astype(BF16),
      ba.reshape(1, D_RG), bx.reshape(1, D_RG), lam.reshape(1, D_RG),
      _pad_lanes(dt_bias), _pad_lanes(a_log),
      jnp.repeat(d_skip, SSD_HEAD_DIM).reshape(1, D_SSD), norm_w.reshape(1, D_SSD))
    return ymix, hlast.reshape(nb, D_RG), state


def _moba_prompt_kernel(slopes_ref, q_ref, k_ref, v_ref, o_ref, kb_ref, vt_ref, pen_ref):
    blk = MOBA_BLOCK
    seq = k_ref.shape[0]
    nblk = seq // blk
    slope = slopes_ref[pl.program_id(1)]
    scale = HEAD_DIM ** -0.5

    kmeans = []
    for n in range(nblk):
        rows = slice(n * blk, (n + 1) * blk)
        kn = k_ref[rows, :]
        kmeans.append(jnp.mean(kn, axis=0, keepdims=True))
        kb_ref[rows, :] = kn.astype(BF16)
        vt_ref[:, rows] = v_ref[rows, :].T.astype(BF16)
    kmean = jnp.concatenate(kmeans, axis=0)
    pen_ref[...] = slope * (lax.broadcasted_iota(jnp.int32, (seq, blk), 1)
                            - lax.broadcasted_iota(jnp.int32, (seq, blk), 0)).astype(F32)
    blk_id = lax.broadcasted_iota(jnp.int32, (nblk, blk), 0)
    causal = (lax.broadcasted_iota(jnp.int32, (blk, blk), 0) <= lax.broadcasted_iota(jnp.int32, (blk, blk), 1))

    for qi in range(nblk):
        q = q_ref[qi * blk:(qi + 1) * blk, :]
        qb = q.astype(BF16)
        base = -slope * float(qi * blk)
        if qi > MOBA_TOPK:
            gate = lax.dot_general(kmean, q, NT_DIMS, precision=lax.Precision.HIGHEST,
                                   preferred_element_type=F32)
            gate = jnp.where(blk_id < qi, gate, -jnp.inf)
            ahead = jnp.zeros((nblk, blk), jnp.int32)
            for n in range(qi):
                gate_n = gate[n:n + 1, :]
                before = jnp.logical_or(gate_n > gate, jnp.logical_and(gate_n == gate, n < blk_id))
                ahead = ahead + before.astype(jnp.int32)
            row_terms = jnp.where(ahead < MOBA_TOPK, base, -jnp.inf)
        else:
            row_terms = jnp.full((nblk, blk), base, F32)

        s_blocks = []
        for n in range(qi + 1):
            rows = slice(n * blk, (n + 1) * blk)
            s = lax.dot_general(kb_ref[rows, :], qb, NT_DIMS, preferred_element_type=F32)
            s = s * scale - pen_ref[rows, :]
            if n < qi:
                s = s + row_terms[n:n + 1, :]
            else:
                s = jnp.where(causal, s + base, -jnp.inf)
            s_blocks.append(s)
        m = jnp.max(s_blocks[0], axis=0, keepdims=True)
        for s in s_blocks[1:]:
            m = jnp.maximum(m, jnp.max(s, axis=0, keepdims=True))
        denom = jnp.zeros((1, blk), F32)
        acc = jnp.zeros((HEAD_DIM, blk), F32)
        for n, s in enumerate(s_blocks):
            p = jnp.exp(s - m)
            denom = denom + jnp.sum(p, axis=0, keepdims=True)
            acc = acc + jnp.dot(vt_ref[:, n * blk:(n + 1) * blk], p.astype(BF16), preferred_element_type=F32)
        o_ref[qi * blk:(qi + 1) * blk, :] = (acc / denom).T.astype(o_ref.dtype)


def alibi_slopes():
    return jnp.exp2(-8.0 * jnp.arange(1, N_HEADS + 1, dtype=F32) / N_HEADS)


def moba_prompt(q, k, v, nb):
    m = q.shape[0]
    l = m // nb
    head_spec = pl.BlockSpec((l, HEAD_DIM), lambda bi, hi: (bi, hi))
    return pl.pallas_call(
        _moba_prompt_kernel,
        grid=(nb, N_HEADS),
        in_specs=[pl.BlockSpec(memory_space=pltpu.SMEM), head_spec, head_spec, head_spec],
        out_specs=head_spec,
        out_shape=jax.ShapeDtypeStruct((m, D_MODEL), BF16),
        scratch_shapes=[pltpu.VMEM((l, HEAD_DIM), BF16), pltpu.VMEM((HEAD_DIM, l), BF16),
                        pltpu.VMEM((l, MOBA_BLOCK), F32)],
        compiler_params=_params("parallel", "parallel"),
        name="moba_prompt",
    )(alibi_slopes(), q, k, v)


def _mixer_step_a_kernel(proj_ref, x_ref, b0_ref, b1_ref, b2_ref, h0_ref,
                         wdt_ref, convw_ref, convb_ref, wa_ref, wx_ref, ba_ref, bx_ref, lam_ref, dtb_ref,
                         yrg_ref, hnew_ref, xbc_ref, dt_ref):
    conv = (convb_ref[...] + convw_ref[0:1, :] * b0_ref[...] + convw_ref[1:2, :] * b1_ref[...]
            + convw_ref[2:3, :] * b2_ref[...] + convw_ref[3:4, :] * proj_ref[:, 0:D_CONV])
    xbc_ref[...] = _silu(conv[:, D_RG:])
    dt_raw = jnp.dot(x_ref[...].astype(BF16), wdt_ref[...], preferred_element_type=F32)
    dt_ref[...] = _softplus(dt_raw[:, 0:SSD_HEADS] + dtb_ref[...])
    for hd in range(RG_HEADS):
        lo = hd * RG_BW
        u = conv[:, lo:lo + RG_BW]
        ub = u.astype(BF16)
        r = jax.nn.sigmoid(jnp.dot(ub, wa_ref[hd], preferred_element_type=F32) + ba_ref[:, lo:lo + RG_BW])
        i = jax.nn.sigmoid(jnp.dot(ub, wx_ref[hd], preferred_element_type=F32) + bx_ref[:, lo:lo + RG_BW])
        log_a = -RG_C * r * _softplus(-lam_ref[:, lo:lo + RG_BW])
        a = jnp.exp(log_a)
        h = a * h0_ref[:, lo:lo + RG_BW] + _sqrt_one_minus_sq(log_a, a) * i * u
        hnew_ref[:, lo:lo + RG_BW] = h
        gate = proj_ref[:, D_CONV + lo:D_CONV + lo + RG_BW]
        yrg_ref[:, lo:lo + RG_BW] = h * jax.nn.gelu(gate)


def _ssd_step_kernel(s0_ref, xs_ref, dt_ref, alog_ref, b_ref, c_ref, snew_ref, y_ref):
    dt = dt_ref[...]
    decay = jnp.exp(dt * (-jnp.exp(alog_ref[...])))
    s_new = s0_ref[...] * decay + (xs_ref[...] * dt) * b_ref[...]
    snew_ref[...] = s_new
    y_ref[...] = jnp.sum(s_new * c_ref[...], axis=-1, keepdims=True)


def _mixer_step_b_kernel(yrg_ref, y_ref, xs_ref, z_ref, dskip_ref, normw_ref, o_ref):
    o_ref[:, 0:D_RG] = yrg_ref[...].astype(BF16)
    yz = (y_ref[...] + dskip_ref[...] * xs_ref[...]) * _silu(z_ref[...])
    gw = D_SSD // SSD_GROUPS
    for g in range(SSD_GROUPS):
        blk = yz[:, g * gw:(g + 1) * gw]
        ms = jnp.mean(blk * blk, axis=-1, keepdims=True)
        o_ref[:, D_RG + g * gw:D_RG + (g + 1) * gw] = (
            blk * lax.rsqrt(ms + RMS_EPS) * normw_ref[:, g * gw:(g + 1) * gw]).astype(BF16)


def _whole(shape):
    return pl.BlockSpec(shape, lambda *_: (0,) * len(shape))


def mixer_sample(proj, x, conv_buf, h0, s0, w_dt, conv_w, conv_b, wa, wx, ba, bx, lam, dt_bias, a_log, d_skip,
                 norm_w):
    nb = proj.shape[0]
    hpg = SSD_HEADS // SSD_GROUPS
    a_ins = (proj, x, conv_buf[:, 0], conv_buf[:, 1], conv_buf[:, 2], h0,
             w_dt, conv_w, conv_b.reshape(1, D_CONV), wa.astype(BF16), wx.astype(BF16),
             ba.reshape(1, D_RG), bx.reshape(1, D_RG), lam.reshape(1, D_RG), dt_bias.reshape(1, SSD_HEADS))
    yrg, hnew, xbc, dt = pl.pallas_call(
        _mixer_step_a_kernel,
        grid=(1,),
        in_specs=[_whole(a.shape) for a in a_ins],
        out_specs=[_whole((nb, D_RG)), _whole((nb, D_RG)), _whole((nb, D_XBC)), _whole((nb, SSD_HEADS))],
        out_shape=[jax.ShapeDtypeStruct((nb, D_RG), F32), jax.ShapeDtypeStruct((nb, D_RG), F32),
                   jax.ShapeDtypeStruct((nb, D_XBC), F32), jax.ShapeDtypeStruct((nb, SSD_HEADS), F32)],
        compiler_params=_params("arbitrary"),
        name="mixer_step_a",
    )(*a_ins)
    xs = xbc[:, :D_SSD]
    snew, y = pl.pallas_call(
        _ssd_step_kernel,
        grid=(nb, SSD_GROUPS),
        in_specs=[pl.BlockSpec((None, hpg, SSD_HEAD_DIM, SSD_STATE), lambda bi, gi: (bi, gi, 0, 0)),
                  pl.BlockSpec((None, hpg, SSD_HEAD_DIM, 1), lambda bi, gi: (bi, gi, 0, 0)),
                  pl.BlockSpec((None, hpg, 1, 1), lambda bi, gi: (bi, gi, 0, 0)),
                  pl.BlockSpec((hpg, 1, 1), lambda bi, gi: (gi, 0, 0)),
                  pl.BlockSpec((None, None, 1, SSD_STATE), lambda bi, gi: (bi, gi, 0, 0)),
                  pl.BlockSpec((None, None, 1, SSD_STATE), lambda bi, gi: (bi, gi, 0, 0))],
        out_specs=[pl.BlockSpec((None, hpg, SSD_HEAD_DIM, SSD_STATE), lambda bi, gi: (bi, gi, 0, 0)),
                   pl.BlockSpec((None, hpg, SSD_HEAD_DIM, 1), lambda bi, gi: (bi, gi, 0, 0))],
        out_shape=[jax.ShapeDtypeStruct((nb, SSD_HEADS, SSD_HEAD_DIM, SSD_STATE), F32),
                   jax.ShapeDtypeStruct((nb, SSD_HEADS, SSD_HEAD_DIM, 1), F32)],
        compiler_params=_params("parallel", "parallel"),
        name="ssd_step",
    )(s0, xs.reshape(nb, SSD_HEADS, SSD_HEAD_DIM, 1), dt.reshape(nb, SSD_HEADS, 1, 1),
      a_log.reshape(SSD_HEADS, 1, 1),
      xbc[:, D_SSD:D_SSD + SSD_GN].reshape(nb, SSD_GROUPS, 1, SSD_STATE),
      xbc[:, D_SSD + SSD_GN:].reshape(nb, SSD_GROUPS, 1, SSD_STATE))
    b_ins = (yrg, y.reshape(nb, D_SSD), xs, proj[:, D_CONV + D_RG:D_CONV + D_RG + D_SSD],
             jnp.repeat(d_skip, SSD_HEAD_DIM).reshape(1, D_SSD), norm_w.reshape(1, D_SSD))
    ymix = pl.pallas_call(
        _mixer_step_b_kernel,
        grid=(1,),
        in_specs=[_whole(a.shape) for a in b_ins],
        out_specs=_whole((nb, D_MIX)),
        out_shape=jax.ShapeDtypeStruct((nb, D_MIX), BF16),
        compiler_params=_params("arbitrary"),
        name="mixer_step_b",
    )(*b_ins)
    return ymix, hnew, snew


def _moba_topk_kernel(q_ref, ksum_ref, idx_ref):
    nblk = ksum_ref.shape[0]
    qh = q_ref[...] * (1.0 / MOBA_BLOCK)
    lane = lax.broadcasted_iota(jnp.int32, (N_HEADS, LANES), 1)
    gate = jnp.full((N_HEADS, LANES), -jnp.inf, F32)
    for n in range(nblk):
        gate = jnp.where(lane == n, jnp.sum(ksum_ref[n] * qh, axis=-1, keepdims=True), gate)
    out = jnp.zeros((N_HEADS, LANES), jnp.int32)
    for kk in range(MOBA_TOPK):
        best = jnp.max(gate, axis=-1, keepdims=True)
        pick = jnp.min(jnp.where(gate == best, lane, LANES), axis=-1, keepdims=True)
        out = jnp.where(lane == kk, pick, out)
        gate = jnp.where(lane == pick, -jnp.inf, gate)
    idx_ref[...] = out


def moba_topk(q, ksum):
    nb, nblk = ksum.shape[:2]
    assert MOBA_TOPK <= nblk <= LANES
    return pl.pallas_call(
        _moba_topk_kernel,
        grid=(nb,),
        in_specs=[pl.BlockSpec((None, N_HEADS, HEAD_DIM), lambda bi: (bi, 0, 0)),
                  pl.BlockSpec((None, nblk, N_HEADS, HEAD_DIM), lambda bi: (bi, 0, 0, 0))],
        out_specs=pl.BlockSpec((None, N_HEADS, LANES), lambda bi: (bi, 0, 0)),
        out_shape=jax.ShapeDtypeStruct((nb, N_HEADS, LANES), jnp.int32),
        compiler_params=_params("parallel"),
        name="moba_topk",
    )(q, ksum)


def _moba_decode_kernel(pt_ref, top_ref, slopes_ref, q_ref, kn_ref, vn_ref, ck_hbm, cv_hbm, o_ref,
                        kbuf, vbuf, sem, *, past_len):
    bi = pl.program_id(0)
    scale = HEAD_DIM ** -0.5
    slots = MOBA_TOPK * PAGES_PER_BLOCK

    def head_copies(hd):
        copies = []
        for kk in range(MOBA_TOPK):
            blk_id = top_ref[bi, hd, kk]
            for half in range(PAGES_PER_BLOCK):
                page = pt_ref[bi, PAGES_PER_BLOCK * blk_id + half]
                slot = kk * PAGES_PER_BLOCK + half
                copies.append(pltpu.make_async_copy(ck_hbm.at[0, page, :, hd, :], kbuf.at[hd, slot],
                                                    sem.at[0, hd, slot]))
                copies.append(pltpu.make_async_copy(cv_hbm.at[0, page, :, hd, :], vbuf.at[hd, slot],
                                                    sem.at[1, hd, slot]))
        return copies

    all_copies = [head_copies(hd) for hd in range(N_HEADS)]
    for copies in all_copies:
        for cp in copies:
            cp.start()

    key_off = lax.broadcasted_iota(jnp.int32, (1, MOBA_BLOCK), 1)
    for hd in range(N_HEADS):
        for cp in all_copies[hd]:
            cp.wait()
        lo = hd * HEAD_DIM
        slope = slopes_ref[hd]
        q = q_ref[:, lo:lo + HEAD_DIM]
        q8 = jnp.broadcast_to(q, (SUBLANES, HEAD_DIM)).astype(BF16)
        kh = kbuf[hd].reshape(slots * PAGE_SIZE, HEAD_DIM).astype(BF16)
        vh = vbuf[hd].reshape(slots * PAGE_SIZE, HEAD_DIM).astype(BF16)
        dist = jnp.concatenate(
            [(past_len - top_ref[bi, hd, kk] * MOBA_BLOCK - key_off).astype(F32) for kk in range(MOBA_TOPK)],
            axis=1)
        s = lax.dot_general(q8, kh, NT_DIMS, preferred_element_type=F32)[0:1, :] * scale - slope * dist
        s_own = jnp.sum(q * kn_ref[:, lo:lo + HEAD_DIM], axis=-1, keepdims=True) * scale
        m = jnp.maximum(jnp.max(s, axis=-1, keepdims=True), s_own)
        p = jnp.exp(s - m)
        p_own = jnp.exp(s_own - m)
        denom = jnp.sum(p, axis=-1, keepdims=True) + p_own
        p8 = jnp.broadcast_to(p, (SUBLANES, slots * PAGE_SIZE)).astype(BF16)
        acc = jnp.dot(p8, vh, preferred_element_type=F32)[0:1, :] + p_own * vn_ref[:, lo:lo + HEAD_DIM]
        o_ref[:, lo:lo + HEAD_DIM] = acc / denom


def moba_decode(q, k_new, v_new, cache_k, cache_v, page_table, top_idx):
    nb, n_pages = page_table.shape
    slots = MOBA_TOPK * PAGES_PER_BLOCK
    row_spec = pl.BlockSpec((None, 1, D_MODEL), lambda bi, pt, top: (bi, 0, 0))
    return pl.pallas_call(
        functools.partial(_moba_decode_kernel, past_len=n_pages * PAGE_SIZE),
        grid_spec=pltpu.PrefetchScalarGridSpec(
            num_scalar_prefetch=2,
            grid=(nb,),
            in_specs=[pl.BlockSpec(memory_space=pltpu.SMEM), row_spec, row_spec, row_spec,
                      pl.BlockSpec(memory_space=pl.ANY), pl.BlockSpec(memory_space=pl.ANY)],
            out_specs=row_spec,
            scratch_shapes=[pltpu.VMEM((N_HEADS, slots, PAGE_SIZE, HEAD_DIM), F32),
                            pltpu.VMEM((N_HEADS, slots, PAGE_SIZE, HEAD_DIM), F32),
                            pltpu.SemaphoreType.DMA((2, N_HEADS, slots))]),
        out_shape=jax.ShapeDtypeStruct((nb, 1, D_MODEL), F32),
        compiler_params=_params("arbitrary"),
        name="moba_decode",
    )(page_table, top_idx, alibi_slopes(), q, k_new, v_new, cache_k, cache_v)


def _trunk_prompt(x, w, nb, cache_k, page_table):
    tm = 512
    n_seq = page_table.shape[0]
    half = n_seq // DEPTH
    conv_w, conv_b, *mixer_w = w["mixer"]
    proj_tiles = dict(tm=1024, tn=1024)
    cw, cb = conv_w, conv_b.reshape(1, D_CONV)
    u, tail_u = in_proj(x, w["w_in"], 0, D_RG, "conv", nb, cw, cb, **proj_tiles)
    xbc, tail_xbc = in_proj(x, w["w_in"], D_RG, D_XBC, "conv_silu", nb, cw, cb, **proj_tiles)
    gate = in_proj(x, w["w_in"], D_CONV, D_RG, "gelu", nb, **proj_tiles)
    z = in_proj(x, w["w_in"], D_CONV + D_RG, D_SSD, "silu", nb, **proj_tiles)
    conv_tail = jnp.concatenate([tail_u, tail_xbc], axis=2)[:, SUBLANES - (CONV_W - 1):, :]
    ymix, h_last, s_last = mixer_prompt(u, xbc, gate, z, x, nb, w["w_dt"], *mixer_w)
    x = matmul_ln(ymix, w["w_out"], x, w["ln1_g"][0], w["ln1_b"][0], tm=tm, tk=2048)
    x, ksum0 = mlp_ln_ksum(x, w["w_up"], w["w_down"], 0, w["ln2_g"][0], w["ln2_b"][0],
                           cache_k, page_table, 0, half, tm=tm, tf=1024)
    q, k, v = qkv_proj(x, w["w_qkv"], tm=tm)
    o = moba_prompt(q, k, v, nb)
    x = matmul_ln(o, w["w_o"], x, w["ln1_g"][1], w["ln1_b"][1], tm=tm, tk=2048)
    x, ksum1 = mlp_ln_ksum(x, w["w_up"], w["w_down"], 1, w["ln2_g"][1], w["ln2_b"][1],
                           cache_k, page_table, half, n_seq - half, tm=tm, tf=1024)
    return x, conv_tail, h_last, s_last, k, v, jnp.concatenate([ksum0, ksum1], axis=0)


def _trunk_sample(x, conv_buf, h0, s0, cache_k, cache_v, page_table, ksum, w):
    nb = x.shape[0]
    proj = matmul(x, w["w_in"], tm=nb, tn=1024, n_out=D_IN - SSD_HEADS)
    new_buf = jnp.concatenate([conv_buf[:, 1:], proj[:, None, :D_CONV]], axis=1)
    ymix, h_new, s_new = mixer_sample(proj, x, conv_buf, h0, s0, w["w_dt"], *w["mixer"])
    x = matmul_ln(ymix, w["w_out"], x, w["ln1_g"][0], w["ln1_b"][0], tm=nb, tk=2048)
    x = mlp_ln(x, w["w_up"], w["w_down"], 0, w["ln2_g"][0], w["ln2_b"][0], tm=nb, tf=1024)
    q, k, v = qkv_proj(x, w["w_qkv"], tm=nb)
    top_idx = moba_topk(q.reshape(nb, N_HEADS, HEAD_DIM), ksum)[:, :, :MOBA_TOPK]
    o = moba_decode(q.reshape(nb, 1, D_MODEL), k.reshape(nb, 1, D_MODEL), v.reshape(nb, 1, D_MODEL),
                    cache_k, cache_v, page_table, top_idx).reshape(nb, D_MODEL)
    x = matmul_ln(o, w["w_o"], x, w["ln1_g"][1], w["ln1_b"][1], tm=nb, tk=2048)
    x = mlp_ln(x, w["w_up"], w["w_down"], 1, w["ln2_g"][1], w["ln2_b"][1], tm=nb, tf=1024)
    return x, new_buf, h_new, s_new, k, v


def kernel(x_prompt, x_sample, state_conv, state_rglru, state_ssd, cache_k, cache_v, page_table, w_in, conv_w, conv_b, rg_wa, rg_ba, rg_wx, rg_bx, rg_lambda, ssd_dt_bias, ssd_a_log, ssd_d, ssd_norm_w, w_out, w_qkv, w_o, ln1_g, ln1_b, ln2_g, ln2_b, w_up, w_down):
    bp, lp, _ = x_prompt.shape
    bs = x_sample.shape[0]
    w = {
        "w_in": w_in[0].astype(BF16),
        "w_dt": jnp.pad(w_in[0][:, D_IN - SSD_HEADS:], ((0, 0), (0, LANES - SSD_HEADS))).astype(BF16),
        "w_out": w_out[0].astype(BF16), "w_qkv": w_qkv[0].astype(BF16), "w_o": w_o[0].astype(BF16),
        "w_up": w_up.astype(BF16), "w_down": w_down.astype(BF16),
        "ln1_g": ln1_g, "ln1_b": ln1_b, "ln2_g": ln2_g, "ln2_b": ln2_b,
        "mixer": (conv_w[0], conv_b[0], rg_wa[0], rg_wx[0], rg_ba[0], rg_bx[0], rg_lambda[0],
                  ssd_dt_bias[0], ssd_a_log[0], ssd_d[0], ssd_norm_w[0]),
    }
    yp, conv_p, h_p, s_p, k_p, v_p, ksum = _trunk_prompt(x_prompt.reshape(bp * lp, D_MODEL), w, bp,
                                                         cache_k, page_table)
    ys, conv_s, h_s, s_s, k_s, v_s = _trunk_sample(
        x_sample.reshape(bs, D_MODEL), state_conv[0], state_rglru[0], state_ssd[0],
        cache_k, cache_v, page_table, ksum, w)
    return (yp.reshape(bp, lp, D_MODEL), ys.reshape(bs, 1, D_MODEL),
            conv_p[None], h_p[None], s_p.reshape(1, bp, SSD_HEADS, SSD_HEAD_DIM, SSD_STATE),
            k_p.reshape(1, bp, lp, N_HEADS, HEAD_DIM), v_p.reshape(1, bp, lp, N_HEADS, HEAD_DIM),
            conv_s[None], h_s[None], s_s[None],
            k_s.reshape(1, bs, 1, N_HEADS, HEAD_DIM), v_s.reshape(1, bs, 1, N_HEADS, HEAD_DIM))
```

```python
import functools
import math

import jax
import jax.numpy as jnp
from jax import lax
from jax.experimental import pallas as pl
from jax.experimental.pallas import tpu as pltpu

F32 = jnp.float32
BF16 = jnp.bfloat16

D_MODEL = 2048
DEPTH = 2
PAGE_SIZE = 128
CONV_W = 4
D_RG = D_MODEL
RG_HEADS = 8
RG_BW = D_RG // RG_HEADS
RG_C = 8.0
D_SSD = D_MODEL
SSD_HEAD_DIM = 64
SSD_HEADS = D_SSD // SSD_HEAD_DIM
SSD_GROUPS = 4
SSD_STATE = 128
SSD_CHUNK = 128
SSD_GN = SSD_GROUPS * SSD_STATE
D_XBC = D_SSD + 2 * SSD_GN
D_CONV = D_RG + D_XBC
D_IN = D_CONV + D_RG + D_SSD + SSD_HEADS
D_MIX = D_RG + D_SSD
N_HEADS = 16
HEAD_DIM = D_MODEL // N_HEADS
MOBA_BLOCK = 256
MOBA_TOPK = 3
PAGES_PER_BLOCK = MOBA_BLOCK // PAGE_SIZE
KSUM_BLOCKS_PER_STEP = 2
D_FF = 4 * D_MODEL
ALPHA = (2 * DEPTH) ** 0.25
LN_EPS = 1e-5
RMS_EPS = 1e-6

SUBLANES = 8
BF16_SUBLANES = 16
LANES = 128
LOG2_E = 1.4426950408889634
VMEM_LIMIT_BYTES = 56 * 1024 * 1024

NT_DIMS = (((1,), (1,)), ((), ()))
TN_DIMS = (((0,), (0,)), ((), ()))


def _params(*semantics):
    return pltpu.CompilerParams(dimension_semantics=semantics, vmem_limit_bytes=VMEM_LIMIT_BYTES)


def _bdot(a, b):
    return jnp.dot(a.astype(BF16), b.astype(BF16), preferred_element_type=F32)


def _layer_norm_rows(y, g, b):
    mu = jnp.mean(y, axis=-1, keepdims=True)
    yc = y - mu
    var = jnp.mean(yc * yc, axis=-1, keepdims=True)
    return yc * lax.rsqrt(var + LN_EPS) * g + b


def _softplus(x):
    u = jnp.exp(-jnp.abs(x))
    w = 1.0 + u
    log1p_u = jnp.where(w == 1.0, u, jnp.log(w) * (u / (w - 1.0)))
    return jnp.maximum(x, 0.0) + log1p_u


def _sqrt_one_minus_sq(log_a, a):
    return jnp.sqrt(-jnp.tanh(log_a) * (a * a + 1.0))


def _silu(x):
    return x * jax.nn.sigmoid(x)


def _matmul_kernel(a_ref, w_ref, o_ref, ab_ref):
    @pl.when(pl.program_id(1) == 0)
    def _():
        ab_ref[...] = a_ref[...].astype(BF16)

    o_ref[...] = jnp.dot(ab_ref[...], w_ref[...], preferred_element_type=F32)


IN_PROJ_COL_CHUNK = 256
IN_PROJ_ROW_CHUNK = 64


def _in_proj_kernel(*refs, kind, blocks_per_seq):
    has_conv = kind in ("conv", "conv_silu")
    if has_conv:
        x_ref, w_ref, cw_ref, cb_ref, o_ref, tail_ref, xb_ref, ybuf, carry = refs
    else:
        x_ref, w_ref, o_ref, xb_ref = refs
    i = pl.program_id(0)
    j = pl.program_id(1)
    tm, tn = o_ref.shape
    front = SUBLANES

    @pl.when(j == 0)
    def _():
        xb_ref[...] = x_ref[...].astype(BF16)

    if has_conv:
        @pl.when(i % blocks_per_seq == 0)
        def _():
            ybuf[0:front, :] = jnp.zeros((front, tn), F32)

        @pl.when(i % blocks_per_seq != 0)
        def _():
            ybuf[0:front, :] = carry[j]

    for c0 in range(0, tn, IN_PROJ_COL_CHUNK):
        cols = slice(c0, c0 + IN_PROJ_COL_CHUNK)
        y = jnp.dot(xb_ref[...], w_ref[:, cols], preferred_element_type=F32)
        if has_conv:
            ybuf[front:front + tm, cols] = y
        for r0 in range(0, tm, IN_PROJ_ROW_CHUNK):
            rows = slice(r0, r0 + IN_PROJ_ROW_CHUNK)
            if has_conv:
                ext = ybuf[r0:r0 + front + IN_PROJ_ROW_CHUNK, cols]
                acc = cb_ref[:, cols] + cw_ref[CONV_W - 1:CONV_W, cols] * ext[front:, :]
                for k in range(CONV_W - 1):
                    shifted = pltpu.roll(ext, CONV_W - 1 - k, axis=0)[front:, :]
                    acc = acc + cw_ref[k:k + 1, cols] * shifted
                o_ref[rows, cols] = _silu(acc) if kind == "conv_silu" else acc
            elif kind == "gelu":
                o_ref[rows, cols] = jax.nn.gelu(y[rows, :])
            else:
                o_ref[rows, cols] = _silu(y[rows, :])

    if has_conv:
        last = ybuf[tm:tm + front, :]
        carry[j] = last
        tail_ref[...] = last


def in_proj(x, w, col0, n_out, kind, nb, conv_w=None, conv_b=None, *, tm, tn):
    m, k = x.shape
    off = col0 // tn
    grid = (m // tm, n_out // tn)
    blocks_per_seq = (m // nb) // tm
    has_conv = kind in ("conv", "conv_silu")
    in_specs = [pl.BlockSpec((tm, k), lambda i, j: (i, 0)),
                pl.BlockSpec((k, tn), lambda i, j: (0, j + off))]
    out_specs = [pl.BlockSpec((tm, tn), lambda i, j: (i, j))]
    out_shape = [jax.ShapeDtypeStruct((m, n_out), F32)]
    scratch = [pltpu.VMEM((tm, k), BF16)]
    operands = [x, w]
    if has_conv:
        in_specs += [pl.BlockSpec((CONV_W, tn), lambda i, j: (0, j + off)),
                     pl.BlockSpec((1, tn), lambda i, j: (0, j + off))]
        out_specs += [pl.BlockSpec((None, SUBLANES, tn), lambda i, j: (i, 0, j))]
        out_shape += [jax.ShapeDtypeStruct((grid[0], SUBLANES, n_out), F32)]
        scratch += [pltpu.VMEM((tm + SUBLANES, tn), F32), pltpu.VMEM((grid[1], SUBLANES, tn), F32)]
        operands += [conv_w, conv_b]
    outs = pl.pallas_call(
        functools.partial(_in_proj_kernel, kind=kind, blocks_per_seq=blocks_per_seq),
        grid=grid,
        in_specs=in_specs,
        out_specs=out_specs,
        out_shape=out_shape,
        scratch_shapes=scratch,
        compiler_params=_params("arbitrary", "arbitrary"),
        name="in_proj_" + kind,
    )(*operands)
    if not has_conv:
        return outs[0]
    return outs[0], outs[1][blocks_per_seq - 1::blocks_per_seq]


def matmul(a, w, *, tm, tn, n_out=None):
    m, k = a.shape
    n = w.shape[1] if n_out is None else n_out
    return pl.pallas_call(
        _matmul_kernel,
        grid=(pl.cdiv(m, tm), pl.cdiv(n, tn)),
        in_specs=[pl.BlockSpec((tm, k), lambda i, j: (i, 0)),
                  pl.BlockSpec((k, tn), lambda i, j: (0, j))],
        out_specs=pl.BlockSpec((tm, tn), lambda i, j: (i, j)),
        out_shape=jax.ShapeDtypeStruct((m, n), F32),
        scratch_shapes=[pltpu.VMEM((tm, k), BF16)],
        compiler_params=_params("parallel", "arbitrary"),
        name="matmul",
    )(a, w)


def _qkv_kernel(a_ref, w_ref, q_ref, k_ref, v_ref, ab_ref):
    @pl.when(pl.program_id(1) == 0)
    def _():
        ab_ref[...] = a_ref[...].astype(BF16)

    for part, o_ref in enumerate((q_ref, k_ref, v_ref)):
        @pl.when(pl.program_id(1) == part)
        def _(o_ref=o_ref):
            o_ref[...] = jnp.dot(ab_ref[...], w_ref[...], preferred_element_type=F32)


def qkv_proj(a, w, *, tm):
    m, k = a.shape
    n = w.shape[1] // 3
    out_spec = pl.BlockSpec((tm, n), lambda i, j: (i, 0))
    return pl.pallas_call(
        _qkv_kernel,
        grid=(pl.cdiv(m, tm), 3),
        in_specs=[pl.BlockSpec((tm, k), lambda i, j: (i, 0)),
                  pl.BlockSpec((k, n), lambda i, j: (0, j))],
        out_specs=[out_spec] * 3,
        out_shape=[jax.ShapeDtypeStruct((m, n), F32)] * 3,
        scratch_shapes=[pltpu.VMEM((tm, k), BF16)],
        compiler_params=_params("parallel", "arbitrary"),
        name="qkv_proj",
    )(a, w)


def _matmul_ln_kernel(a_ref, w_ref, res_ref, g_ref, b_ref, o_ref, acc_ref):
    kk = pl.program_id(1)

    @pl.when(kk == 0)
    def _():
        acc_ref[...] = jnp.zeros_like(acc_ref)

    acc_ref[...] += _bdot(a_ref[...], w_ref[...])

    @pl.when(kk == pl.num_programs(1) - 1)
    def _():
        y = ALPHA * res_ref[...] + acc_ref[...]
        o_ref[...] = _layer_norm_rows(y, g_ref[...], b_ref[...])


def _matmul_ln_full_k_kernel(a_ref, w_ref, res_ref, g_ref, b_ref, o_ref):
    y = ALPHA * res_ref[...] + _bdot(a_ref[...], w_ref[...])
    o_ref[...] = _layer_norm_rows(y, g_ref[...], b_ref[...])


def matmul_ln(a, w, res, g, b, *, tm, tk):
    m, k = a.shape
    n = w.shape[1]
    if tk == k:
        return pl.pallas_call(
            _matmul_ln_full_k_kernel,
            grid=(pl.cdiv(m, tm),),
            in_specs=[pl.BlockSpec((tm, k), lambda i: (i, 0)),
                      pl.BlockSpec((k, n), lambda i: (0, 0)),
                      pl.BlockSpec((tm, n), lambda i: (i, 0)),
                      pl.BlockSpec((1, n), lambda i: (0, 0)),
                      pl.BlockSpec((1, n), lambda i: (0, 0))],
            out_specs=pl.BlockSpec((tm, n), lambda i: (i, 0)),
            out_shape=jax.ShapeDtypeStruct((m, n), F32),
            compiler_params=_params("parallel"),
            name="matmul_ln_full_k",
        )(a, w, res, g.reshape(1, n), b.reshape(1, n))
    return pl.pallas_call(
        _matmul_ln_kernel,
        grid=(pl.cdiv(m, tm), k // tk),
        in_specs=[pl.BlockSpec((tm, tk), lambda i, kk: (i, kk)),
                  pl.BlockSpec((tk, n), lambda i, kk: (kk, 0)),
                  pl.BlockSpec((tm, n), lambda i, kk: (i, 0)),
                  pl.BlockSpec((1, n), lambda i, kk: (0, 0)),
                  pl.BlockSpec((1, n), lambda i, kk: (0, 0))],
        out_specs=pl.BlockSpec((tm, n), lambda i, kk: (i, 0)),
        out_shape=jax.ShapeDtypeStruct((m, n), F32),
        scratch_shapes=[pltpu.VMEM((tm, n), F32)],
        compiler_params=_params("parallel", "arbitrary"),
        name="matmul_ln",
    )(a, w, res, g.reshape(1, n), b.reshape(1, n))


def _mlp_ln_body(x_ref, wu_ref, wd_ref, g_ref, b_ref, o_ref, xb_ref, shadow_work=None):
    f = pl.program_id(1)

    @pl.when(f == 0)
    def _():
        xb_ref[...] = x_ref[...].astype(BF16)
        o_ref[...] = jnp.zeros_like(o_ref)

    if shadow_work is not None:
        shadow_work()
    hdn = jnp.maximum(jnp.dot(xb_ref[...], wu_ref[...], preferred_element_type=F32), 0.0)
    o_ref[...] += _bdot(hdn * hdn, wd_ref[...])

    @pl.when(f == pl.num_programs(1) - 1)
    def _():
        y = ALPHA * x_ref[...] + o_ref[...]
        o_ref[...] = _layer_norm_rows(y, g_ref[...], b_ref[...])


def _mlp_ln_kernel(x_ref, wu_ref, wd_ref, g_ref, b_ref, o_ref, xb_ref):
    _mlp_ln_body(x_ref, wu_ref, wd_ref, g_ref, b_ref, o_ref, xb_ref)


def _mlp_ln_ksum_kernel(pt_ref, x_ref, wu_ref, wd_ref, g_ref, b_ref, ck_hbm, o_ref, ksum_ref,
                        xb_ref, kbuf, sem, *, seq0, nblk):
    step = pl.program_id(0) * pl.num_programs(1) + pl.program_id(1)
    n_steps = pl.num_programs(0) * pl.num_programs(1)
    pages_per_step = KSUM_BLOCKS_PER_STEP * PAGES_PER_BLOCK
    slot = step % 2

    def page_copies(at_step, at_slot):
        copies = []
        for pg in range(pages_per_step):
            flat = at_step * KSUM_BLOCKS_PER_STEP + pg // PAGES_PER_BLOCK
            page = pt_ref[seq0 + flat // nblk, (flat % nblk) * PAGES_PER_BLOCK + pg % PAGES_PER_BLOCK]
            copies.append(pltpu.make_async_copy(ck_hbm.at[0, page], kbuf.at[at_slot, pg], sem.at[at_slot, pg]))
        return copies

    @pl.when(step == 0)
    def _():
        for cp in page_copies(0, 0):
            cp.start()

    def sum_pages_and_prefetch():
        for cp in page_copies(step, slot):
            cp.wait()
        for j in range(KSUM_BLOCKS_PER_STEP):
            acc = jnp.sum(kbuf[slot, j * PAGES_PER_BLOCK], axis=0)
            for p in range(1, PAGES_PER_BLOCK):
                acc = acc + jnp.sum(kbuf[slot, j * PAGES_PER_BLOCK + p], axis=0)
            ksum_ref[j] = acc
        for cp in page_copies((step + 1) % n_steps, 1 - slot):
            cp.start()

    _mlp_ln_body(x_ref, wu_ref, wd_ref, g_ref, b_ref, o_ref, xb_ref, sum_pages_and_prefetch)

    @pl.when(step == n_steps - 1)
    def _():
        for cp in page_copies(0, 1 - slot):
            cp.wait()


def mlp_ln(x, w_up, w_down, layer, g, b, *, tm, tf):
    m, d = x.shape
    ff = w_up.shape[2]
    return pl.pallas_call(
        _mlp_ln_kernel,
        grid=(pl.cdiv(m, tm), ff // tf),
        in_specs=[pl.BlockSpec((tm, d), lambda i, f: (i, 0)),
                  pl.BlockSpec((None, d, tf), lambda i, f: (layer, 0, f)),
                  pl.BlockSpec((None, tf, d), lambda i, f: (layer, f, 0)),
                  pl.BlockSpec((1, d), lambda i, f: (0, 0)),
                  pl.BlockSpec((1, d), lambda i, f: (0, 0))],
        out_specs=pl.BlockSpec((tm, d), lambda i, f: (i, 0)),
        out_shape=jax.ShapeDtypeStruct((m, d), F32),
        scratch_shapes=[pltpu.VMEM((tm, d), BF16)],
        compiler_params=_params("parallel", "arbitrary"),
        name="mlp_ln",
    )(x, w_up, w_down, g.reshape(1, d), b.reshape(1, d))


def mlp_ln_ksum(x, w_up, w_down, layer, g, b, cache_k, page_table, seq0, n_seq, *, tm, tf):
    m, d = x.shape
    ff = w_up.shape[2]
    grid = (m // tm, ff // tf)
    nblk = page_table.shape[1] // PAGES_PER_BLOCK
    assert n_seq * nblk == grid[0] * grid[1] * KSUM_BLOCKS_PER_STEP and nblk % KSUM_BLOCKS_PER_STEP == 0
    steps_per_seq = nblk // KSUM_BLOCKS_PER_STEP

    def ksum_index(i, f, pt):
        step = i * grid[1] + f
        return (step // steps_per_seq, step % steps_per_seq, 0, 0)

    return pl.pallas_call(
        functools.partial(_mlp_ln_ksum_kernel, seq0=seq0, nblk=nblk),
        grid_spec=pltpu.PrefetchScalarGridSpec(
            num_scalar_prefetch=1,
            grid=grid,
            in_specs=[pl.BlockSpec((tm, d), lambda i, f, pt: (i, 0)),
                      pl.BlockSpec((None, d, tf), lambda i, f, pt: (layer, 0, f)),
                      pl.BlockSpec((None, tf, d), lambda i, f, pt: (layer, f, 0)),
                      pl.BlockSpec((1, d), lambda i, f, pt: (0, 0)),
                      pl.BlockSpec((1, d), lambda i, f, pt: (0, 0)),
                      pl.BlockSpec(memory_space=pl.ANY)],
            out_specs=[pl.BlockSpec((tm, d), lambda i, f, pt: (i, 0)),
                       pl.BlockSpec((None, KSUM_BLOCKS_PER_STEP, N_HEADS, HEAD_DIM), ksum_index)],
            scratch_shapes=[pltpu.VMEM((tm, d), BF16),
                            pltpu.VMEM((2, KSUM_BLOCKS_PER_STEP * PAGES_PER_BLOCK, PAGE_SIZE, N_HEADS, HEAD_DIM), F32),
                            pltpu.SemaphoreType.DMA((2, KSUM_BLOCKS_PER_STEP * PAGES_PER_BLOCK))]),
        out_shape=[jax.ShapeDtypeStruct((m, d), F32),
                   jax.ShapeDtypeStruct((n_seq, nblk, N_HEADS, HEAD_DIM), F32)],
        compiler_params=_params("arbitrary", "arbitrary"),
        name="mlp_ln_ksum",
    )(page_table, x, w_up, w_down, g.reshape(1, d), b.reshape(1, d), cache_k)


def _mixer_prompt_kernel(u_ref, xbc_ref, gate_ref, z_ref, x_ref,
                         wdt_ref, wa_ref, wx_ref, ba_ref, bx_ref, lam_ref,
                         dtb_ref, alog_ref, dskip_ref, normw_ref,
                         ymix_ref, hlast_ref, state_ref,
                         a_scr, h_scr, y_scr, decay_rate_scr):
    t = SSD_CHUNK
    c = pl.program_id(1)

    @pl.when(c == 0)
    def _():
        hlast_ref[...] = jnp.zeros_like(hlast_ref)
        state_ref[...] = jnp.zeros_like(state_ref)
        decay_rate_scr[...] = -RG_C * _softplus(-lam_ref[...])

    row = lax.broadcasted_iota(jnp.int32, (t, RG_BW), 0)
    row_in_tile = row % SUBLANES
    is_start = jnp.logical_and(c == 0, row == 0)
    for hd in range(RG_HEADS):
        lo = hd * RG_BW
        u = u_ref[:, lo:lo + RG_BW]
        ub = u.astype(BF16)
        r = jax.nn.sigmoid(jnp.dot(ub, wa_ref[hd], preferred_element_type=F32) + ba_ref[:, lo:lo + RG_BW])
        i = jax.nn.sigmoid(jnp.dot(ub, wx_ref[hd], preferred_element_type=F32) + bx_ref[:, lo:lo + RG_BW])
        log_a = decay_rate_scr[:, lo:lo + RG_BW] * r
        a_raw = jnp.exp(log_a)
        a = jnp.where(is_start, 0.0, a_raw)
        mult = jnp.where(is_start, 1.0, _sqrt_one_minus_sq(log_a, a_raw))
        bt = mult * i * u
        for s in (1, 2, 4):
            keep = row_in_tile >= s
            a_sh = pltpu.roll(a, s, axis=0)
            b_sh = pltpu.roll(bt, s, axis=0)
            bt = jnp.where(keep, a * b_sh + bt, bt)
            a = jnp.where(keep, a * a_sh, a)
        a_scr[:, lo:lo + RG_BW] = a
        h_scr[:, lo:lo + RG_BW] = bt

    def carry_tile(gi, carry):
        r0 = pl.multiple_of(gi * SUBLANES, SUBLANES)
        h = a_scr[pl.ds(r0, SUBLANES), :] * carry + h_scr[pl.ds(r0, SUBLANES), :]
        h_scr[pl.ds(r0, SUBLANES), :] = h
        return h[SUBLANES - 1:SUBLANES, :]

    hlast_ref[...] = lax.fori_loop(0, t // SUBLANES, carry_tile, hlast_ref[...])
    ymix_ref[:, 0:D_RG] = (h_scr[...] * gate_ref[...]).astype(BF16)

    lane = lax.broadcasted_iota(jnp.int32, (t, LANES), 1)
    rows = lax.broadcasted_iota(jnp.int32, (t, LANES), 0)
    dt_raw = jnp.dot(x_ref[...].astype(BF16), wdt_ref[...], preferred_element_type=F32)
    dt = jnp.where(lane < SSD_HEADS, _softplus(dt_raw + dtb_ref[...]), 0.0)
    cs = dt * (-jnp.exp(alog_ref[...]))
    s = 1
    while s < t:
        cs = cs + jnp.where(rows >= s, pltpu.roll(cs, s, axis=0), 0.0)
        s *= 2
    cs_t = cs.T
    cs_last = cs[t - 1:t, :]
    decay_to_end = jnp.exp(cs_last - cs)
    exp_cs = jnp.exp(cs)
    chunk_decay = jnp.exp(cs_last)

    ii = lax.broadcasted_iota(jnp.int32, (t, t), 0)
    jj = lax.broadcasted_iota(jnp.int32, (t, t), 1)
    causal = jj <= ii
    first_head = lane < SSD_HEAD_DIM
    first_head_rows = rows < SSD_HEAD_DIM

    for g in range(SSD_GROUPS):
        b_lo = D_SSD + g * SSD_STATE
        bmat = xbc_ref[:, b_lo:b_lo + SSD_STATE].astype(BF16)
        cmat = xbc_ref[:, b_lo + SSD_GN:b_lo + SSD_GN + SSD_STATE].astype(BF16)
        cb = lax.dot_general(cmat, bmat, NT_DIMS, preferred_element_type=F32)
        for pair in range(SSD_HEADS // SSD_GROUPS // 2):
            h0 = g * (SSD_HEADS // SSD_GROUPS) + 2 * pair
            lo = h0 * SSD_HEAD_DIM

            def per_head(x2d):
                return jnp.where(first_head, x2d[:, h0:h0 + 1], x2d[:, h0 + 1:h0 + 2])

            xs = xbc_ref[:, lo:lo + LANES]
            xdt = xs * per_head(dt)
            xdt_b = xdt.astype(BF16)
            y_heads = []
            for hh in (h0, h0 + 1):
                seg = cs[:, hh:hh + 1] - cs_t[hh:hh + 1, :]
                lmat = jnp.exp(jnp.where(causal, seg, -jnp.inf))
                y_heads.append(jnp.dot((cb * lmat).astype(BF16), xdt_b, preferred_element_type=F32))
            y = jnp.where(first_head, y_heads[0], y_heads[1])
            s_prev = state_ref[lo:lo + LANES, :]
            y = y + lax.dot_general(cmat, s_prev.astype(BF16), NT_DIMS,
                                    preferred_element_type=F32) * per_head(exp_cs)
            y_scr[:, lo:lo + LANES] = y + dskip_ref[:, lo:lo + LANES] * xs
            decay_rows = jnp.where(first_head_rows, chunk_decay[:, h0:h0 + 1], chunk_decay[:, h0 + 1:h0 + 2])
            state_ref[lo:lo + LANES, :] = s_prev * decay_rows + lax.dot_general(
                (xdt * per_head(decay_to_end)).astype(BF16), bmat, TN_DIMS, preferred_element_type=F32)

    gw = D_SSD // SSD_GROUPS
    for g in range(SSD_GROUPS):
        lo = g * gw
        yz = y_scr[:, lo:lo + gw] * z_ref[:, lo:lo + gw]
        ms = jnp.mean(yz * yz, axis=-1, keepdims=True)
        ymix_ref[:, D_RG + lo:D_RG + lo + gw] = (yz * lax.rsqrt(ms + RMS_EPS)
                                                 * normw_ref[:, lo:lo + gw]).astype(BF16)


def _pad_lanes(v):
    return jnp.zeros((1, LANES), F32).at[0, :v.shape[0]].set(v)


def mixer_prompt(u, xbc, gate, z, x, nb, w_dt, wa, wx, ba, bx, lam, dt_bias, a_log, d_skip, norm_w):
    m = u.shape[0]
    l = m // nb
    t = SSD_CHUNK
    nc = l // t
    row_blk = lambda bi, ci: bi * nc + ci
    rows = lambda width: pl.BlockSpec((t, width), lambda bi, ci: (row_blk(bi, ci), 0))
    full = lambda shape: pl.BlockSpec(shape, lambda bi, ci: (0,) * len(shape))
    in_specs = [rows(D_RG), rows(D_XBC), rows(D_RG), rows(D_SSD), rows(D_MODEL)]
    in_specs += [full((D_MODEL, LANES)),
                 full((RG_HEADS, RG_BW, RG_BW)), full((RG_HEADS, RG_BW, RG_BW)),
                 full((1, D_RG)), full((1, D_RG)), full((1, D_RG)),
                 full((1, LANES)), full((1, LANES)), full((1, D_SSD)), full((1, D_SSD))]
    ymix, hlast, state = pl.pallas_call(
        _mixer_prompt_kernel,
        grid=(nb, nc),
        in_specs=in_specs,
        out_specs=[pl.BlockSpec((t, D_MIX), lambda bi, ci: (row_blk(bi, ci), 0)),
                   pl.BlockSpec((None, 1, D_RG), lambda bi, ci: (bi, 0, 0)),
                   pl.BlockSpec((None, D_SSD, SSD_STATE), lambda bi, ci: (bi, 0, 0))],
        out_shape=[jax.ShapeDtypeStruct((m, D_MIX), BF16),
                   jax.ShapeDtypeStruct((nb, 1, D_RG), F32),
                   jax.ShapeDtypeStruct((nb, D_SSD, SSD_STATE), F32)],
        scratch_shapes=[pltpu.VMEM((t, D_RG), F32), pltpu.VMEM((t, D_RG), F32),
                        pltpu.VMEM((t, D_SSD), F32), pltpu.VMEM((1, D_RG), F32)],
        compiler_params=_params("parallel", "arbitrary"),
        name="mixer_prompt",
    )(u, xbc, gate, z, x, w_dt, wa.astype(BF16), wx.astype(BF16),
      ba.reshape(1, D_RG), bx.reshape(1, D_RG), lam.reshape(1, D_RG),
      _pad_lanes(dt_bias), _pad_lanes(a_log),
      jnp.repeat(d_skip, SSD_HEAD_DIM).reshape(1, D_SSD), norm_w.reshape(1, D_SSD))
    return ymix, hlast.reshape(nb, D_RG), state


def _moba_prompt_kernel(slopes_ref, q_ref, k_ref, v_ref, o_ref, kb_ref, vt_ref, pen_ref):
    blk = MOBA_BLOCK
    seq = k_ref.shape[0]
    nblk = seq // blk
    slope = slopes_ref[pl.program_id(1)] * LOG2_E
    scale = HEAD_DIM ** -0.5 * LOG2_E

    kmeans = []
    for n in range(nblk):
        rows = slice(n * blk, (n + 1) * blk)
        kn = k_ref[rows, :]
        kmeans.append(jnp.mean(kn, axis=0, keepdims=True))
        kb_ref[rows, :] = kn.astype(BF16)
        vt_ref[0:HEAD_DIM, rows] = v_ref[rows, :].T.astype(BF16)
    vt_ref[HEAD_DIM:, :] = jnp.ones((vt_ref.shape[0] - HEAD_DIM, seq), BF16)
    kmean = jnp.concatenate(kmeans, axis=0)
    pen_ref[...] = slope * (lax.broadcasted_iota(jnp.int32, (seq, blk), 1)
                            - lax.broadcasted_iota(jnp.int32, (seq, blk), 0)).astype(F32)
    blk_id = lax.broadcasted_iota(jnp.int32, (nblk, blk), 0)
    causal = (lax.broadcasted_iota(jnp.int32, (blk, blk), 0) <= lax.broadcasted_iota(jnp.int32, (blk, blk), 1))

    for qi in range(nblk):
        q = q_ref[qi * blk:(qi + 1) * blk, :]
        qb = q.astype(BF16)
        base = -slope * float(qi * blk)
        if qi > MOBA_TOPK:
            gate = lax.dot_general(kmean, q, NT_DIMS, precision=lax.Precision.HIGHEST,
                                   preferred_element_type=F32)
            gate = jnp.where(blk_id < qi, gate, -jnp.inf)
            ahead = jnp.zeros((nblk, blk), jnp.int32)
            for n in range(qi):
                gate_n = gate[n:n + 1, :]
                before = jnp.logical_or(gate_n > gate, jnp.logical_and(gate_n == gate, n < blk_id))
                ahead = ahead + before.astype(jnp.int32)
            row_terms = jnp.where(ahead < MOBA_TOPK, base, -jnp.inf)
        else:
            row_terms = jnp.full((nblk, blk), base, F32)

        s_blocks = []
        for n in range(qi + 1):
            rows = slice(n * blk, (n + 1) * blk)
            s = lax.dot_general(kb_ref[rows, :], qb, NT_DIMS, preferred_element_type=F32)
            s = s * scale - pen_ref[rows, :]
            if n < qi:
                s = s + row_terms[n:n + 1, :]
            else:
                s = jnp.where(causal, s + base, -jnp.inf)
            s_blocks.append(s)
        m = jnp.max(s_blocks[0], axis=0, keepdims=True)
        for s in s_blocks[1:]:
            m = jnp.maximum(m, jnp.max(s, axis=0, keepdims=True))
        acc = jnp.zeros((vt_ref.shape[0], blk), F32)
        for n, s in enumerate(s_blocks):
            p = jnp.exp2(s - m)
            acc = acc + jnp.dot(vt_ref[:, n * blk:(n + 1) * blk], p.astype(BF16), preferred_element_type=F32)
        denom = acc[HEAD_DIM:HEAD_DIM + 1, :]
        o_ref[qi * blk:(qi + 1) * blk, :] = (acc[0:HEAD_DIM, :] / denom).T.astype(o_ref.dtype)


def alibi_slopes():
    return jnp.exp2(-8.0 * jnp.arange(1, N_HEADS + 1, dtype=F32) / N_HEADS)


def moba_prompt(q, k, v, nb):
    m = q.shape[0]
    l = m // nb
    head_spec = pl.BlockSpec((l, HEAD_DIM), lambda bi, hi: (bi, hi))
    return pl.pallas_call(
        _moba_prompt_kernel,
        grid=(nb, N_HEADS),
        in_specs=[pl.BlockSpec(memory_space=pltpu.SMEM), head_spec, head_spec, head_spec],
        out_specs=head_spec,
        out_shape=jax.ShapeDtypeStruct((m, D_MODEL), BF16),
        scratch_shapes=[pltpu.VMEM((l, HEAD_DIM), BF16), pltpu.VMEM((HEAD_DIM + BF16_SUBLANES, l), BF16),
                        pltpu.VMEM((l, MOBA_BLOCK), F32)],
        compiler_params=_params("parallel", "parallel"),
        name="moba_prompt",
    )(alibi_slopes(), q, k, v)


def _mixer_step_a_kernel(proj_ref, x_ref, b0_ref, b1_ref, b2_ref, h0_ref,
                         wdt_ref, convw_ref, convb_ref, wa_ref, wx_ref, ba_ref, bx_ref, lam_ref, dtb_ref,
                         yrg_ref, hnew_ref, xbc_ref, dt_ref):
    conv = (convb_ref[...] + convw_ref[0:1, :] * b0_ref[...] + convw_ref[1:2, :] * b1_ref[...]
            + convw_ref[2:3, :] * b2_ref[...] + convw_ref[3:4, :] * proj_ref[:, 0:D_CONV])
    xbc_ref[...] = _silu(conv[:, D_RG:])
    dt_raw = jnp.dot(x_ref[...].astype(BF16), wdt_ref[...], preferred_element_type=F32)
    dt_ref[...] = _softplus(dt_raw[:, 0:SSD_HEADS] + dtb_ref[...])
    for hd in range(RG_HEADS):
        lo = hd * RG_BW
        u = conv[:, lo:lo + RG_BW]
        ub = u.astype(BF16)
        r = jax.nn.sigmoid(jnp.dot(ub, wa_ref[hd], preferred_element_type=F32) + ba_ref[:, lo:lo + RG_BW])
        i = jax.nn.sigmoid(jnp.dot(ub, wx_ref[hd], preferred_element_type=F32) + bx_ref[:, lo:lo + RG_BW])
        log_a = -RG_C * r * _softplus(-lam_ref[:, lo:lo + RG_BW])
        a = jnp.exp(log_a)
        h = a * h0_ref[:, lo:lo + RG_BW] + _sqrt_one_minus_sq(log_a, a) * i * u
        hnew_ref[:, lo:lo + RG_BW] = h
        gate = proj_ref[:, D_CONV + lo:D_CONV + lo + RG_BW]
        yrg_ref[:, lo:lo + RG_BW] = h * jax.nn.gelu(gate)


def _ssd_step_kernel(s0_ref, xs_ref, dt_ref, alog_ref, b_ref, c_ref, snew_ref, y_ref):
    dt = dt_ref[...]
    decay = jnp.exp(dt * (-jnp.exp(alog_ref[...])))
    s_new = s0_ref[...] * decay + (xs_ref[...] * dt) * b_ref[...]
    snew_ref[...] = s_new
    y_ref[...] = jnp.sum(s_new * c_ref[...], axis=-1, keepdims=True)


def _mixer_step_b_kernel(yrg_ref, y_ref, xs_ref, z_ref, dskip_ref, normw_ref, o_ref):
    o_ref[:, 0:D_RG] = yrg_ref[...].astype(BF16)
    yz = (y_ref[...] + dskip_ref[...] * xs_ref[...]) * _silu(z_ref[...])
    gw = D_SSD // SSD_GROUPS
    for g in range(SSD_GROUPS):
        blk = yz[:, g * gw:(g + 1) * gw]
        ms = jnp.mean(blk * blk, axis=-1, keepdims=True)
        o_ref[:, D_RG + g * gw:D_RG + (g + 1) * gw] = (
            blk * lax.rsqrt(ms + RMS_EPS) * normw_ref[:, g * gw:(g + 1) * gw]).astype(BF16)


def _whole(shape):
    return pl.BlockSpec(shape, lambda *_: (0,) * len(shape))


def mixer_sample(proj, x, conv_buf, h0, s0, w_dt, conv_w, conv_b, wa, wx, ba, bx, lam, dt_bias, a_log, d_skip,
                 norm_w):
    nb = proj.shape[0]
    hpg = SSD_HEADS // SSD_GROUPS
    a_ins = (proj, x, conv_buf[:, 0], conv_buf[:, 1], conv_buf[:, 2], h0,
             w_dt, conv_w, conv_b.reshape(1, D_CONV), wa.astype(BF16), wx.astype(BF16),
             ba.reshape(1, D_RG), bx.reshape(1, D_RG), lam.reshape(1, D_RG), dt_bias.reshape(1, SSD_HEADS))
    yrg, hnew, xbc, dt = pl.pallas_call(
        _mixer_step_a_kernel,
        grid=(1,),
        in_specs=[_whole(a.shape) for a in a_ins],
        out_specs=[_whole((nb, D_RG)), _whole((nb, D_RG)), _whole((nb, D_XBC)), _whole((nb, SSD_HEADS))],
        out_shape=[jax.ShapeDtypeStruct((nb, D_RG), F32), jax.ShapeDtypeStruct((nb, D_RG), F32),
                   jax.ShapeDtypeStruct((nb, D_XBC), F32), jax.ShapeDtypeStruct((nb, SSD_HEADS), F32)],
        compiler_params=_params("arbitrary"),
        name="mixer_step_a",
    )(*a_ins)
    xs = xbc[:, :D_SSD]
    snew, y = pl.pallas_call(
        _ssd_step_kernel,
        grid=(nb, SSD_GROUPS),
        in_specs=[pl.BlockSpec((None, hpg, SSD_HEAD_DIM, SSD_STATE), lambda bi, gi: (bi, gi, 0, 0)),
                  pl.BlockSpec((None, hpg, SSD_HEAD_DIM, 1), lambda bi, gi: (bi, gi, 0, 0)),
                  pl.BlockSpec((None, hpg, 1, 1), lambda bi, gi: (bi, gi, 0, 0)),
                  pl.BlockSpec((hpg, 1, 1), lambda bi, gi: (gi, 0, 0)),
                  pl.BlockSpec((None, None, 1, SSD_STATE), lambda bi, gi: (bi, gi, 0, 0)),
                  pl.BlockSpec((None, None, 1, SSD_STATE), lambda bi, gi: (bi, gi, 0, 0))],
        out_specs=[pl.BlockSpec((None, hpg, SSD_HEAD_DIM, SSD_STATE), lambda bi, gi: (bi, gi, 0, 0)),
                   pl.BlockSpec((None, hpg, SSD_HEAD_DIM, 1), lambda bi, gi: (bi, gi, 0, 0))],
        out_shape=[jax.ShapeDtypeStruct((nb, SSD_HEADS, SSD_HEAD_DIM, SSD_STATE), F32),
                   jax.ShapeDtypeStruct((nb, SSD_HEADS, SSD_HEAD_DIM, 1), F32)],
        compiler_params=_params("parallel", "parallel"),
        name="ssd_step",
    )(s0, xs.reshape(nb, SSD_HEADS, SSD_HEAD_DIM, 1), dt.reshape(nb, SSD_HEADS, 1, 1),
      a_log.reshape(SSD_HEADS, 1, 1),
      xbc[:, D_SSD:D_SSD + SSD_GN].reshape(nb, SSD_GROUPS, 1, SSD_STATE),
      xbc[:, D_SSD + SSD_GN:].reshape(nb, SSD_GROUPS, 1, SSD_STATE))
    b_ins = (yrg, y.reshape(nb, D_SSD), xs, proj[:, D_CONV + D_RG:D_CONV + D_RG + D_SSD],
             jnp.repeat(d_skip, SSD_HEAD_DIM).reshape(1, D_SSD), norm_w.reshape(1, D_SSD))
    ymix = pl.pallas_call(
        _mixer_step_b_kernel,
        grid=(1,),
        in_specs=[_whole(a.shape) for a in b_ins],
        out_specs=_whole((nb, D_MIX)),
        out_shape=jax.ShapeDtypeStruct((nb, D_MIX), BF16),
        compiler_params=_params("arbitrary"),
        name="mixer_step_b",
    )(*b_ins)
    return ymix, hnew, snew


def _moba_topk_kernel(q_ref, ksum_ref, idx_ref):
    nblk = ksum_ref.shape[0]
    qh = q_ref[...] * (1.0 / MOBA_BLOCK)
    lane = lax.broadcasted_iota(jnp.int32, (N_HEADS, LANES), 1)
    gate = jnp.full((N_HEADS, LANES), -jnp.inf, F32)
    for n in range(nblk):
        gate = jnp.where(lane == n, jnp.sum(ksum_ref[n] * qh, axis=-1, keepdims=True), gate)
    out = jnp.zeros((N_HEADS, LANES), jnp.int32)
    for kk in range(MOBA_TOPK):
        best = jnp.max(gate, axis=-1, keepdims=True)
        pick = jnp.min(jnp.where(gate == best, lane, LANES), axis=-1, keepdims=True)
        out = jnp.where(lane == kk, pick, out)
        gate = jnp.where(lane == pick, -jnp.inf, gate)
    idx_ref[...] = out


def moba_topk(q, ksum):
    nb, nblk = ksum.shape[:2]
    assert MOBA_TOPK <= nblk <= LANES
    return pl.pallas_call(
        _moba_topk_kernel,
        grid=(nb,),
        in_specs=[pl.BlockSpec((None, N_HEADS, HEAD_DIM), lambda bi: (bi, 0, 0)),
                  pl.BlockSpec((None, nblk, N_HEADS, HEAD_DIM), lambda bi: (bi, 0, 0, 0))],
        out_specs=pl.BlockSpec((None, N_HEADS, LANES), lambda bi: (bi, 0, 0)),
        out_shape=jax.ShapeDtypeStruct((nb, N_HEADS, LANES), jnp.int32),
        compiler_params=_params("parallel"),
        name="moba_topk",
    )(q, ksum)


def _moba_decode_kernel(pt_ref, top_ref, slopes_ref, q_ref, kn_ref, vn_ref, ck_hbm, cv_hbm, o_ref,
                        kbuf, vbuf, sem, *, past_len):
    bi = pl.program_id(0)
    scale = HEAD_DIM ** -0.5
    slots = MOBA_TOPK * PAGES_PER_BLOCK

    def head_copies(hd):
        copies = []
        for kk in range(MOBA_TOPK):
            blk_id = top_ref[bi, hd, kk]
            for half in range(PAGES_PER_BLOCK):
                page = pt_ref[bi, PAGES_PER_BLOCK * blk_id + half]
                slot = kk * PAGES_PER_BLOCK + half
                copies.append(pltpu.make_async_copy(ck_hbm.at[0, page, :, hd, :], kbuf.at[hd, slot],
                                                    sem.at[0, hd, slot]))
                copies.append(pltpu.make_async_copy(cv_hbm.at[0, page, :, hd, :], vbuf.at[hd, slot],
                                                    sem.at[1, hd, slot]))
        return copies

    all_copies = [head_copies(hd) for hd in range(N_HEADS)]
    for copies in all_copies:
        for cp in copies:
            cp.start()

    key_off = lax.broadcasted_iota(jnp.int32, (1, MOBA_BLOCK), 1)
    for hd in range(N_HEADS):
        for cp in all_copies[hd]:
            cp.wait()
        lo = hd * HEAD_DIM
        slope = slopes_ref[hd]
        q = q_ref[:, lo:lo + HEAD_DIM]
        q8 = jnp.broadcast_to(q, (SUBLANES, HEAD_DIM)).astype(BF16)
        kh = kbuf[hd].reshape(slots * PAGE_SIZE, HEAD_DIM).astype(BF16)
        vh = vbuf[hd].reshape(slots * PAGE_SIZE, HEAD_DIM).astype(BF16)
        dist = jnp.concatenate(
            [(past_len - top_ref[bi, hd, kk] * MOBA_BLOCK - key_off).astype(F32) for kk in range(MOBA_TOPK)],
            axis=1)
        s = lax.dot_general(q8, kh, NT_DIMS, preferred_element_type=F32)[0:1, :] * scale - slope * dist
        s_own = jnp.sum(q * kn_ref[:, lo:lo + HEAD_DIM], axis=-1, keepdims=True) * scale
        m = jnp.maximum(jnp.max(s, axis=-1, keepdims=True), s_own)
        p = jnp.exp(s - m)
        p_own = jnp.exp(s_own - m)
        denom = jnp.sum(p, axis=-1, keepdims=True) + p_own
        p8 = jnp.broadcast_to(p, (SUBLANES, slots * PAGE_SIZE)).astype(BF16)
        acc = jnp.dot(p8, vh, preferred_element_type=F32)[0:1, :] + p_own * vn_ref[:, lo:lo + HEAD_DIM]
        o_ref[:, lo:lo + HEAD_DIM] = acc / denom


def moba_decode(q, k_new, v_new, cache_k, cache_v, page_table, top_idx):
    nb, n_pages = page_table.shape
    slots = MOBA_TOPK * PAGES_PER_BLOCK
    row_spec = pl.BlockSpec((None, 1, D_MODEL), lambda bi, pt, top: (bi, 0, 0))
    return pl.pallas_call(
        functools.partial(_moba_decode_kernel, past_len=n_pages * PAGE_SIZE),
        grid_spec=pltpu.PrefetchScalarGridSpec(
            num_scalar_prefetch=2,
            grid=(nb,),
            in_specs=[pl.BlockSpec(memory_space=pltpu.SMEM), row_spec, row_spec, row_spec,
                      pl.BlockSpec(memory_space=pl.ANY), pl.BlockSpec(memory_space=pl.ANY)],
            out_specs=row_spec,
            scratch_shapes=[pltpu.VMEM((N_HEADS, slots, PAGE_SIZE, HEAD_DIM), F32),
                            pltpu.VMEM((N_HEADS, slots, PAGE_SIZE, HEAD_DIM), F32),
                            pltpu.SemaphoreType.DMA((2, N_HEADS, slots))]),
        out_shape=jax.ShapeDtypeStruct((nb, 1, D_MODEL), F32),
        compiler_params=_params("arbitrary"),
        name="moba_decode",
    )(page_table, top_idx, alibi_slopes(), q, k_new, v_new, cache_k, cache_v)


def _trunk_prompt(x, w, nb, cache_k, page_table):
    tm = 512
    n_seq = page_table.shape[0]
    half = n_seq // DEPTH
    conv_w, conv_b, *mixer_w = w["mixer"]
    proj_tiles = dict(tm=1024, tn=1024)
    cw, cb = conv_w, conv_b.reshape(1, D_CONV)
    u, tail_u = in_proj(x, w["w_in"], 0, D_RG, "conv", nb, cw, cb, **proj_tiles)
    xbc, tail_xbc = in_proj(x, w["w_in"], D_RG, D_XBC, "conv_silu", nb, cw, cb, **proj_tiles)
    gate = in_proj(x, w["w_in"], D_CONV, D_RG, "gelu", nb, **proj_tiles)
    z = in_proj(x, w["w_in"], D_CONV + D_RG, D_SSD, "silu", nb, **proj_tiles)
    conv_tail = jnp.concatenate([tail_u, tail_xbc], axis=2)[:, SUBLANES - (CONV_W - 1):, :]
    ymix, h_last, s_last = mixer_prompt(u, xbc, gate, z, x, nb, w["w_dt"], *mixer_w)
    x = matmul_ln(ymix, w["w_out"], x, w["ln1_g"][0], w["ln1_b"][0], tm=tm, tk=2048)
    x, ksum0 = mlp_ln_ksum(x, w["w_up"], w["w_down"], 0, w["ln2_g"][0], w["ln2_b"][0],
                           cache_k, page_table, 0, half, tm=tm, tf=1024)
    q, k, v = qkv_proj(x, w["w_qkv"], tm=tm)
    o = moba_prompt(q, k, v, nb)
    x = matmul_ln(o, w["w_o"], x, w["ln1_g"][1], w["ln1_b"][1], tm=tm, tk=2048)
    x, ksum1 = mlp_ln_ksum(x, w["w_up"], w["w_down"], 1, w["ln2_g"][1], w["ln2_b"][1],
                           cache_k, page_table, half, n_seq - half, tm=tm, tf=1024)
    return x, conv_tail, h_last, s_last, k, v, jnp.concatenate([ksum0, ksum1], axis=0)


def _trunk_sample(x, conv_buf, h0, s0, cache_k, cache_v, page_table, ksum, w):
    nb = x.shape[0]
    proj = matmul(x, w["w_in"], tm=nb, tn=1024, n_out=D_IN - SSD_HEADS)
    new_buf = jnp.concatenate([conv_buf[:, 1:], proj[:, None, :D_CONV]], axis=1)
    ymix, h_new, s_new = mixer_sample(proj, x, conv_buf, h0, s0, w["w_dt"], *w["mixer"])
    x = matmul_ln(ymix, w["w_out"], x, w["ln1_g"][0], w["ln1_b"][0], tm=nb, tk=2048)
    x = mlp_ln(x, w["w_up"], w["w_down"], 0, w["ln2_g"][0], w["ln2_b"][0], tm=nb, tf=1024)
    q, k, v = qkv_proj(x, w["w_qkv"], tm=nb)
    top_idx = moba_topk(q.reshape(nb, N_HEADS, HEAD_DIM), ksum)[:, :, :MOBA_TOPK]
    o = moba_decode(q.reshape(nb, 1, D_MODEL), k.reshape(nb, 1, D_MODEL), v.reshape(nb, 1, D_MODEL),
                    cache_k, cache_v, page_table, top_idx).reshape(nb, D_MODEL)
    x = matmul_ln(o, w["w_o"], x, w["ln1_g"][1], w["ln1_b"][1], tm=nb, tk=2048)
    x = mlp_ln(x, w["w_up"], w["w_down"], 1, w["ln2_g"][1], w["ln2_b"][1], tm=nb, tf=1024)
    return x, new_buf, h_new, s_new, k, v


def kernel(x_prompt, x_sample, state_conv, state_rglru, state_ssd, cache_k, cache_v, page_table, w_in, conv_w, conv_b, rg_wa, rg_ba, rg_wx, rg_bx, rg_lambda, ssd_dt_bias, ssd_a_log, ssd_d, ssd_norm_w, w_out, w_qkv, w_o, ln1_g, ln1_b, ln2_g, ln2_b, w_up, w_down):
    bp, lp, _ = x_prompt.shape
    bs = x_sample.shape[0]
    w = {
        "w_in": w_in[0].astype(BF16),
        "w_dt": jnp.pad(w_in[0][:, D_IN - SSD_HEADS:], ((0, 0), (0, LANES - SSD_HEADS))).astype(BF16),
        "w_out": w_out[0].astype(BF16), "w_qkv": w_qkv[0].astype(BF16), "w_o": w_o[0].astype(BF16),
        "w_up": w_up.astype(BF16), "w_down": w_down.astype(BF16),
        "ln1_g": ln1_g, "ln1_b": ln1_b, "ln2_g": ln2_g, "ln2_b": ln2_b,
        "mixer": (conv_w[0], conv_b[0], rg_wa[0], rg_wx[0], rg_ba[0], rg_bx[0], rg_lambda[0],
                  ssd_dt_bias[0], ssd_a_log[0], ssd_d[0], ssd_norm_w[0]),
    }
    yp, conv_p, h_p, s_p, k_p, v_p, ksum = _trunk_prompt(x_prompt.reshape(bp * lp, D_MODEL), w, bp,
                                                         cache_k, page_table)
    ys, conv_s, h_s, s_s, k_s, v_s = _trunk_sample(
        x_sample.reshape(bs, D_MODEL), state_conv[0], state_rglru[0], state_ssd[0],
        cache_k, cache_v, page_table, ksum, w)
    return (yp.reshape(bp, lp, D_MODEL), ys.reshape(bs, 1, D_MODEL),
            conv_p[None], h_p[None], s_p.reshape(1, bp, SSD_HEADS, SSD_HEAD_DIM, SSD_STATE),
            k_p.reshape(1, bp, lp, N_HEADS, HEAD_DIM), v_p.reshape(1, bp, lp, N_HEADS, HEAD_DIM),
            conv_s[None], h_s[None], s_s[None],
            k_s.reshape(1, bs, 1, N_HEADS, HEAD_DIM), v_s.reshape(1, bs, 1, N_HEADS, HEAD_DIM))
```

```python
import functools
import math

import jax
import jax.numpy as jnp
from jax import lax
from jax.experimental import pallas as pl
from jax.experimental.pallas import tpu as pltpu

F32 = jnp.float32
BF16 = jnp.bfloat16

D_MODEL = 2048
DEPTH = 2
PAGE_SIZE = 128
CONV_W = 4
D_RG = D_MODEL
RG_HEADS = 8
RG_BW = D_RG // RG_HEADS
RG_C = 8.0
D_SSD = D_MODEL
SSD_HEAD_DIM = 64
SSD_HEADS = D_SSD // SSD_HEAD_DIM
SSD_GROUPS = 4
SSD_STATE = 128
SSD_CHUNK = 128
SSD_GN = SSD_GROUPS * SSD_STATE
D_XBC = D_SSD + 2 * SSD_GN
D_CONV = D_RG + D_XBC
D_IN = D_CONV + D_RG + D_SSD + SSD_HEADS
D_MIX = D_RG + D_SSD
N_HEADS = 16
HEAD_DIM = D_MODEL // N_HEADS
MOBA_BLOCK = 256
MOBA_TOPK = 3
PAGES_PER_BLOCK = MOBA_BLOCK // PAGE_SIZE
KSUM_BLOCKS_PER_STEP = 2
D_FF = 4 * D_MODEL
ALPHA = (2 * DEPTH) ** 0.25
LN_EPS = 1e-5
RMS_EPS = 1e-6

SUBLANES = 8
BF16_SUBLANES = 16
LANES = 128
LOG2_E = 1.4426950408889634
VMEM_LIMIT_BYTES = 56 * 1024 * 1024

NT_DIMS = (((1,), (1,)), ((), ()))
TN_DIMS = (((0,), (0,)), ((), ()))


def _params(*semantics):
    return pltpu.CompilerParams(dimension_semantics=semantics, vmem_limit_bytes=VMEM_LIMIT_BYTES)


def _bdot(a, b):
    return jnp.dot(a.astype(BF16), b.astype(BF16), preferred_element_type=F32)


def _layer_norm_rows(y, g, b):
    mu = jnp.mean(y, axis=-1, keepdims=True)
    yc = y - mu
    var = jnp.mean(yc * yc, axis=-1, keepdims=True)
    return yc * lax.rsqrt(var + LN_EPS) * g + b


def _softplus(x):
    u = jnp.exp(-jnp.abs(x))
    w = 1.0 + u
    log1p_u = jnp.where(w == 1.0, u, jnp.log(w) * (u / (w - 1.0)))
    return jnp.maximum(x, 0.0) + log1p_u


def _sqrt_one_minus_sq(log_a, a):
    return jnp.sqrt(-jnp.tanh(log_a) * (a * a + 1.0))


def _silu(x):
    return x * jax.nn.sigmoid(x)


def _matmul_kernel(a_ref, w_ref, o_ref, ab_ref):
    @pl.when(pl.program_id(1) == 0)
    def _():
        ab_ref[...] = a_ref[...].astype(BF16)

    o_ref[...] = jnp.dot(ab_ref[...], w_ref[...], preferred_element_type=F32)


IN_PROJ_COL_CHUNK = 256
IN_PROJ_ROW_CHUNK = 64


def _in_proj_kernel(*refs, kind, blocks_per_seq):
    has_conv = kind in ("conv", "conv_silu")
    if has_conv:
        x_ref, w_ref, cw_ref, cb_ref, o_ref, tail_ref, xb_ref, ybuf, carry = refs
    else:
        x_ref, w_ref, o_ref, xb_ref = refs
    i = pl.program_id(0)
    j = pl.program_id(1)
    tm, tn = o_ref.shape
    front = SUBLANES

    @pl.when(j == 0)
    def _():
        xb_ref[...] = x_ref[...].astype(BF16)

    if has_conv:
        @pl.when(i % blocks_per_seq == 0)
        def _():
            ybuf[0:front, :] = jnp.zeros((front, tn), F32)

        @pl.when(i % blocks_per_seq != 0)
        def _():
            ybuf[0:front, :] = carry[j]

    for c0 in range(0, tn, IN_PROJ_COL_CHUNK):
        cols = slice(c0, c0 + IN_PROJ_COL_CHUNK)
        y = jnp.dot(xb_ref[...], w_ref[:, cols], preferred_element_type=F32)
        if has_conv:
            ybuf[front:front + tm, cols] = y
        for r0 in range(0, tm, IN_PROJ_ROW_CHUNK):
            rows = slice(r0, r0 + IN_PROJ_ROW_CHUNK)
            if has_conv:
                ext = ybuf[r0:r0 + front + IN_PROJ_ROW_CHUNK, cols]
                acc = cb_ref[:, cols] + cw_ref[CONV_W - 1:CONV_W, cols] * ext[front:, :]
                for k in range(CONV_W - 1):
                    shifted = pltpu.roll(ext, CONV_W - 1 - k, axis=0)[front:, :]
                    acc = acc + cw_ref[k:k + 1, cols] * shifted
                o_ref[rows, cols] = _silu(acc) if kind == "conv_silu" else acc
            elif kind == "gelu":
                o_ref[rows, cols] = jax.nn.gelu(y[rows, :])
            else:
                o_ref[rows, cols] = _silu(y[rows, :])

    if has_conv:
        last = ybuf[tm:tm + front, :]
        carry[j] = last
        tail_ref[...] = last


def in_proj(x, w, col0, n_out, kind, nb, conv_w=None, conv_b=None, *, tm, tn):
    m, k = x.shape
    off = col0 // tn
    grid = (m // tm, n_out // tn)
    blocks_per_seq = (m // nb) // tm
    has_conv = kind in ("conv", "conv_silu")
    in_specs = [pl.BlockSpec((tm, k), lambda i, j: (i, 0)),
                pl.BlockSpec((k, tn), lambda i, j: (0, j + off))]
    out_specs = [pl.BlockSpec((tm, tn), lambda i, j: (i, j))]
    out_shape = [jax.ShapeDtypeStruct((m, n_out), F32)]
    scratch = [pltpu.VMEM((tm, k), BF16)]
    operands = [x, w]
    if has_conv:
        in_specs += [pl.BlockSpec((CONV_W, tn), lambda i, j: (0, j + off)),
                     pl.BlockSpec((1, tn), lambda i, j: (0, j + off))]
        out_specs += [pl.BlockSpec((None, SUBLANES, tn), lambda i, j: (i, 0, j))]
        out_shape += [jax.ShapeDtypeStruct((grid[0], SUBLANES, n_out), F32)]
        scratch += [pltpu.VMEM((tm + SUBLANES, tn), F32), pltpu.VMEM((grid[1], SUBLANES, tn), F32)]
        operands += [conv_w, conv_b]
    outs = pl.pallas_call(
        functools.partial(_in_proj_kernel, kind=kind, blocks_per_seq=blocks_per_seq),
        grid=grid,
        in_specs=in_specs,
        out_specs=out_specs,
        out_shape=out_shape,
        scratch_shapes=scratch,
        compiler_params=_params("arbitrary", "arbitrary"),
        name="in_proj_" + kind,
    )(*operands)
    if not has_conv:
        return outs[0]
    return outs[0], outs[1][blocks_per_seq - 1::blocks_per_seq]


def matmul(a, w, *, tm, tn, n_out=None):
    m, k = a.shape
    n = w.shape[1] if n_out is None else n_out
    return pl.pallas_call(
        _matmul_kernel,
        grid=(pl.cdiv(m, tm), pl.cdiv(n, tn)),
        in_specs=[pl.BlockSpec((tm, k), lambda i, j: (i, 0)),
                  pl.BlockSpec((k, tn), lambda i, j: (0, j))],
        out_specs=pl.BlockSpec((tm, tn), lambda i, j: (i, j)),
        out_shape=jax.ShapeDtypeStruct((m, n), F32),
        scratch_shapes=[pltpu.VMEM((tm, k), BF16)],
        compiler_params=_params("parallel", "arbitrary"),
        name="matmul",
    )(a, w)


def _resident_w_kernel(a_ref, w_ref, o_ref):
    o_ref[...] = _bdot(a_ref[...], w_ref[...])


def qkv_proj(a, w, *, tm):
    m, k = a.shape
    n = w.shape[1] // 3

    def part(idx):
        return pl.pallas_call(
            _resident_w_kernel,
            grid=(pl.cdiv(m, tm),),
            in_specs=[pl.BlockSpec((tm, k), lambda i: (i, 0)),
                      pl.BlockSpec((k, n), lambda i: (0, idx), pipeline_mode=pl.Buffered(1))],
            out_specs=pl.BlockSpec((tm, n), lambda i: (i, 0)),
            out_shape=jax.ShapeDtypeStruct((m, n), F32),
            compiler_params=_params("parallel"),
            name="qkv_proj",
        )(a, w)

    return part(0), part(1), part(2)


def _matmul_ln_kernel(a_ref, w_ref, res_ref, g_ref, b_ref, o_ref, acc_ref):
    kk = pl.program_id(1)

    @pl.when(kk == 0)
    def _():
        acc_ref[...] = jnp.zeros_like(acc_ref)

    acc_ref[...] += _bdot(a_ref[...], w_ref[...])

    @pl.when(kk == pl.num_programs(1) - 1)
    def _():
        y = ALPHA * res_ref[...] + acc_ref[...]
        o_ref[...] = _layer_norm_rows(y, g_ref[...], b_ref[...])


def _matmul_ln_full_k_kernel(a_ref, w_ref, res_ref, g_ref, b_ref, o_ref):
    y = ALPHA * res_ref[...] + _bdot(a_ref[...], w_ref[...])
    o_ref[...] = _layer_norm_rows(y, g_ref[...], b_ref[...])


def matmul_ln(a, w, res, g, b, *, tm, tk):
    m, k = a.shape
    n = w.shape[1]
    if tk == k:
        return pl.pallas_call(
            _matmul_ln_full_k_kernel,
            grid=(pl.cdiv(m, tm),),
            in_specs=[pl.BlockSpec((tm, k), lambda i: (i, 0)),
                      pl.BlockSpec((k, n), lambda i: (0, 0)),
                      pl.BlockSpec((tm, n), lambda i: (i, 0)),
                      pl.BlockSpec((1, n), lambda i: (0, 0)),
                      pl.BlockSpec((1, n), lambda i: (0, 0))],
            out_specs=pl.BlockSpec((tm, n), lambda i: (i, 0)),
            out_shape=jax.ShapeDtypeStruct((m, n), F32),
            compiler_params=_params("parallel"),
            name="matmul_ln_full_k",
        )(a, w, res, g.reshape(1, n), b.reshape(1, n))
    return pl.pallas_call(
        _matmul_ln_kernel,
        grid=(pl.cdiv(m, tm), k // tk),
        in_specs=[pl.BlockSpec((tm, tk), lambda i, kk: (i, kk)),
                  pl.BlockSpec((tk, n), lambda i, kk: (kk, 0)),
                  pl.BlockSpec((tm, n), lambda i, kk: (i, 0)),
                  pl.BlockSpec((1, n), lambda i, kk: (0, 0)),
                  pl.BlockSpec((1, n), lambda i, kk: (0, 0))],
        out_specs=pl.BlockSpec((tm, n), lambda i, kk: (i, 0)),
        out_shape=jax.ShapeDtypeStruct((m, n), F32),
        scratch_shapes=[pltpu.VMEM((tm, n), F32)],
        compiler_params=_params("parallel", "arbitrary"),
        name="matmul_ln",
    )(a, w, res, g.reshape(1, n), b.reshape(1, n))


def _mlp_ln_body(x_ref, wu_ref, wd_ref, g_ref, b_ref, o_ref, xb_ref, shadow_work=None):
    f = pl.program_id(1)

    @pl.when(f == 0)
    def _():
        xb_ref[...] = x_ref[...].astype(BF16)
        o_ref[...] = jnp.zeros_like(o_ref)

    if shadow_work is not None:
        shadow_work()
    hdn = jnp.maximum(jnp.dot(xb_ref[...], wu_ref[...], preferred_element_type=F32), 0.0)
    o_ref[...] += _bdot(hdn * hdn, wd_ref[...])

    @pl.when(f == pl.num_programs(1) - 1)
    def _():
        y = ALPHA * x_ref[...] + o_ref[...]
        o_ref[...] = _layer_norm_rows(y, g_ref[...], b_ref[...])


def _mlp_ln_kernel(x_ref, wu_ref, wd_ref, g_ref, b_ref, o_ref, xb_ref):
    _mlp_ln_body(x_ref, wu_ref, wd_ref, g_ref, b_ref, o_ref, xb_ref)


def _mlp_ln_ksum_kernel(pt_ref, x_ref, wu_ref, wd_ref, g_ref, b_ref, ck_hbm, o_ref, ksum_ref,
                        xb_ref, kbuf, sem, *, seq0, nblk):
    step = pl.program_id(0) * pl.num_programs(1) + pl.program_id(1)
    n_steps = pl.num_programs(0) * pl.num_programs(1)
    pages_per_step = KSUM_BLOCKS_PER_STEP * PAGES_PER_BLOCK
    slot = step % 2

    def page_copies(at_step, at_slot):
        copies = []
        for pg in range(pages_per_step):
            flat = at_step * KSUM_BLOCKS_PER_STEP + pg // PAGES_PER_BLOCK
            page = pt_ref[seq0 + flat // nblk, (flat % nblk) * PAGES_PER_BLOCK + pg % PAGES_PER_BLOCK]
            copies.append(pltpu.make_async_copy(ck_hbm.at[0, page], kbuf.at[at_slot, pg], sem.at[at_slot, pg]))
        return copies

    @pl.when(step == 0)
    def _():
        for cp in page_copies(0, 0):
            cp.start()

    @pl.when(step + 1 < n_steps)
    def _():
        for cp in page_copies(step + 1, 1 - slot):
            cp.start()

    def sum_pages():
        for cp in page_copies(step, slot):
            cp.wait()
        for j in range(KSUM_BLOCKS_PER_STEP):
            acc = jnp.sum(kbuf[slot, j * PAGES_PER_BLOCK], axis=0)
            for p in range(1, PAGES_PER_BLOCK):
                acc = acc + jnp.sum(kbuf[slot, j * PAGES_PER_BLOCK + p], axis=0)
            ksum_ref[j] = acc

    _mlp_ln_body(x_ref, wu_ref, wd_ref, g_ref, b_ref, o_ref, xb_ref, sum_pages)


def mlp_ln(x, w_up, w_down, layer, g, b, *, tm, tf):
    m, d = x.shape
    ff = w_up.shape[2]
    return pl.pallas_call(
        _mlp_ln_kernel,
        grid=(pl.cdiv(m, tm), ff // tf),
        in_specs=[pl.BlockSpec((tm, d), lambda i, f: (i, 0)),
                  pl.BlockSpec((None, d, tf), lambda i, f: (layer, 0, f)),
                  pl.BlockSpec((None, tf, d), lambda i, f: (layer, f, 0)),
                  pl.BlockSpec((1, d), lambda i, f: (0, 0)),
                  pl.BlockSpec((1, d), lambda i, f: (0, 0))],
        out_specs=pl.BlockSpec((tm, d), lambda i, f: (i, 0)),
        out_shape=jax.ShapeDtypeStruct((m, d), F32),
        scratch_shapes=[pltpu.VMEM((tm, d), BF16)],
        compiler_params=_params("parallel", "arbitrary"),
        name="mlp_ln",
    )(x, w_up, w_down, g.reshape(1, d), b.reshape(1, d))


def mlp_ln_ksum(x, w_up, w_down, layer, g, b, cache_k, page_table, seq0, n_seq, *, tm, tf):
    m, d = x.shape
    ff = w_up.shape[2]
    grid = (m // tm, ff // tf)
    nblk = page_table.shape[1] // PAGES_PER_BLOCK
    assert n_seq * nblk == grid[0] * grid[1] * KSUM_BLOCKS_PER_STEP and nblk % KSUM_BLOCKS_PER_STEP == 0
    steps_per_seq = nblk // KSUM_BLOCKS_PER_STEP

    def ksum_index(i, f, pt):
        step = i * grid[1] + f
        return (step // steps_per_seq, step % steps_per_seq, 0, 0)

    return pl.pallas_call(
        functools.partial(_mlp_ln_ksum_kernel, seq0=seq0, nblk=nblk),
        grid_spec=pltpu.PrefetchScalarGridSpec(
            num_scalar_prefetch=1,
            grid=grid,
            in_specs=[pl.BlockSpec((tm, d), lambda i, f, pt: (i, 0)),
                      pl.BlockSpec((None, d, tf), lambda i, f, pt: (layer, 0, f)),
                      pl.BlockSpec((None, tf, d), lambda i, f, pt: (layer, f, 0)),
                      pl.BlockSpec((1, d), lambda i, f, pt: (0, 0)),
                      pl.BlockSpec((1, d), lambda i, f, pt: (0, 0)),
                      pl.BlockSpec(memory_space=pl.ANY)],
            out_specs=[pl.BlockSpec((tm, d), lambda i, f, pt: (i, 0)),
                       pl.BlockSpec((None, KSUM_BLOCKS_PER_STEP, N_HEADS, HEAD_DIM), ksum_index)],
            scratch_shapes=[pltpu.VMEM((tm, d), BF16),
                            pltpu.VMEM((2, KSUM_BLOCKS_PER_STEP * PAGES_PER_BLOCK, PAGE_SIZE, N_HEADS, HEAD_DIM), F32),
                            pltpu.SemaphoreType.DMA((2, KSUM_BLOCKS_PER_STEP * PAGES_PER_BLOCK))]),
        out_shape=[jax.ShapeDtypeStruct((m, d), F32),
                   jax.ShapeDtypeStruct((n_seq, nblk, N_HEADS, HEAD_DIM), F32)],
        compiler_params=_params("arbitrary", "arbitrary"),
        name="mlp_ln_ksum",
    )(page_table, x, w_up, w_down, g.reshape(1, d), b.reshape(1, d), cache_k)


def _mixer_prompt_kernel(u_ref, xbc_ref, gate_ref, z_ref, x_ref,
                         wdt_ref, wa_ref, wx_ref, ba_ref, bx_ref, lam_ref,
                         dtb_ref, alog_ref, dskip_ref, normw_ref,
                         ymix_ref, hlast_ref, state_ref,
                         a_scr, h_scr, y_scr, decay_rate_scr):
    t = SSD_CHUNK
    c = pl.program_id(1)

    @pl.when(c == 0)
    def _():
        hlast_ref[...] = jnp.zeros_like(hlast_ref)
        state_ref[...] = jnp.zeros_like(state_ref)
        decay_rate_scr[...] = -RG_C * _softplus(-lam_ref[...])

    row = lax.broadcasted_iota(jnp.int32, (t, RG_BW), 0)
    row_in_tile = row % SUBLANES
    is_start = jnp.logical_and(c == 0, row == 0)
    for hd in range(RG_HEADS):
        lo = hd * RG_BW
        u = u_ref[:, lo:lo + RG_BW]
        ub = u.astype(BF16)
        r = jax.nn.sigmoid(jnp.dot(ub, wa_ref[hd], preferred_element_type=F32) + ba_ref[:, lo:lo + RG_BW])
        i = jax.nn.sigmoid(jnp.dot(ub, wx_ref[hd], preferred_element_type=F32) + bx_ref[:, lo:lo + RG_BW])
        log_a = decay_rate_scr[:, lo:lo + RG_BW] * r
        a_raw = jnp.exp(log_a)
        a = jnp.where(is_start, 0.0, a_raw)
        mult = jnp.where(is_start, 1.0, _sqrt_one_minus_sq(log_a, a_raw))
        bt = mult * i * u
        for s in (1, 2, 4):
            keep = row_in_tile >= s
            a_sh = pltpu.roll(a, s, axis=0)
            b_sh = pltpu.roll(bt, s, axis=0)
            bt = jnp.where(keep, a * b_sh + bt, bt)
            a = jnp.where(keep, a * a_sh, a)
        a_scr[:, lo:lo + RG_BW] = a
        h_scr[:, lo:lo + RG_BW] = bt

    def carry_tile(gi, carry):
        r0 = pl.multiple_of(gi * SUBLANES, SUBLANES)
        h = a_scr[pl.ds(r0, SUBLANES), :] * carry + h_scr[pl.ds(r0, SUBLANES), :]
        h_scr[pl.ds(r0, SUBLANES), :] = h
        return h[SUBLANES - 1:SUBLANES, :]

    hlast_ref[...] = lax.fori_loop(0, t // SUBLANES, carry_tile, hlast_ref[...])
    ymix_ref[:, 0:D_RG] = (h_scr[...] * gate_ref[...]).astype(BF16)

    lane = lax.broadcasted_iota(jnp.int32, (t, LANES), 1)
    rows = lax.broadcasted_iota(jnp.int32, (t, LANES), 0)
    dt_raw = jnp.dot(x_ref[...].astype(BF16), wdt_ref[...], preferred_element_type=F32)
    dt = jnp.where(lane < SSD_HEADS, _softplus(dt_raw + dtb_ref[...]), 0.0)
    cs = dt * (-jnp.exp(alog_ref[...]))
    s = 1
    while s < t:
        cs = cs + jnp.where(rows >= s, pltpu.roll(cs, s, axis=0), 0.0)
        s *= 2
    cs_t = cs.T
    cs_last = cs[t - 1:t, :]
    decay_to_end = jnp.exp(cs_last - cs)
    exp_cs = jnp.exp(cs)
    chunk_decay = jnp.exp(cs_last)

    ii = lax.broadcasted_iota(jnp.int32, (t, t), 0)
    jj = lax.broadcasted_iota(jnp.int32, (t, t), 1)
    causal = jj <= ii
    first_head = lane < SSD_HEAD_DIM
    first_head_rows = rows < SSD_HEAD_DIM

    for g in range(SSD_GROUPS):
        b_lo = D_SSD + g * SSD_STATE
        bmat = xbc_ref[:, b_lo:b_lo + SSD_STATE].astype(BF16)
        cmat = xbc_ref[:, b_lo + SSD_GN:b_lo + SSD_GN + SSD_STATE].astype(BF16)
        cb = lax.dot_general(cmat, bmat, NT_DIMS, preferred_element_type=F32)
        for pair in range(SSD_HEADS // SSD_GROUPS // 2):
            h0 = g * (SSD_HEADS // SSD_GROUPS) + 2 * pair
            lo = h0 * SSD_HEAD_DIM

            def per_head(x2d):
                return jnp.where(first_head, x2d[:, h0:h0 + 1], x2d[:, h0 + 1:h0 + 2])

            xs = xbc_ref[:, lo:lo + LANES]
            xdt = xs * per_head(dt)
            xdt_b = xdt.astype(BF16)
            y_heads = []
            for hh in (h0, h0 + 1):
                seg = cs[:, hh:hh + 1] - cs_t[hh:hh + 1, :]
                lmat = jnp.exp(jnp.where(causal, seg, -jnp.inf))
                y_heads.append(jnp.dot((cb * lmat).astype(BF16), xdt_b, preferred_element_type=F32))
            y = jnp.where(first_head, y_heads[0], y_heads[1])
            s_prev = state_ref[lo:lo + LANES, :]
            y = y + lax.dot_general(cmat, s_prev.astype(BF16), NT_DIMS,
                                    preferred_element_type=F32) * per_head(exp_cs)
            y_scr[:, lo:lo + LANES] = y + dskip_ref[:, lo:lo + LANES] * xs
            decay_rows = jnp.where(first_head_rows, chunk_decay[:, h0:h0 + 1], chunk_decay[:, h0 + 1:h0 + 2])
            state_ref[lo:lo + LANES, :] = s_prev * decay_rows + lax.dot_general(
                (xdt * per_head(decay_to_end)).astype(BF16), bmat, TN_DIMS, preferred_element_type=F32)

    gw = D_SSD // SSD_GROUPS
    for g in range(SSD_GROUPS):
        lo = g * gw
        yz = y_scr[:, lo:lo + gw] * z_ref[:, lo:lo + gw]
        ms = jnp.mean(yz * yz, axis=-1, keepdims=True)
        ymix_ref[:, D_RG + lo:D_RG + lo + gw] = (yz * lax.rsqrt(ms + RMS_EPS)
                                                 * normw_ref[:, lo:lo + gw]).astype(BF16)


def _pad_lanes(v):
    return jnp.zeros((1, LANES), F32).at[0, :v.shape[0]].set(v)


def mixer_prompt(u, xbc, gate, z, x, nb, w_dt, wa, wx, ba, bx, lam, dt_bias, a_log, d_skip, norm_w):
    m = u.shape[0]
    l = m // nb
    t = SSD_CHUNK
    nc = l // t
    row_blk = lambda bi, ci: bi * nc + ci
    rows = lambda width: pl.BlockSpec((t, width), lambda bi, ci: (row_blk(bi, ci), 0))
    full = lambda shape: pl.BlockSpec(shape, lambda bi, ci: (0,) * len(shape))
    in_specs = [rows(D_RG), rows(D_XBC), rows(D_RG), rows(D_SSD), rows(D_MODEL)]
    in_specs += [full((D_MODEL, LANES)),
                 full((RG_HEADS, RG_BW, RG_BW)), full((RG_HEADS, RG_BW, RG_BW)),
                 full((1, D_RG)), full((1, D_RG)), full((1, D_RG)),
                 full((1, LANES)), full((1, LANES)), full((1, D_SSD)), full((1, D_SSD))]
    ymix, hlast, state = pl.pallas_call(
        _mixer_prompt_kernel,
        grid=(nb, nc),
        in_specs=in_specs,
        out_specs=[pl.BlockSpec((t, D_MIX), lambda bi, ci: (row_blk(bi, ci), 0)),
                   pl.BlockSpec((None, 1, D_RG), lambda bi, ci: (bi, 0, 0)),
                   pl.BlockSpec((None, D_SSD, SSD_STATE), lambda bi, ci: (bi, 0, 0))],
        out_shape=[jax.ShapeDtypeStruct((m, D_MIX), BF16),
                   jax.ShapeDtypeStruct((nb, 1, D_RG), F32),
                   jax.ShapeDtypeStruct((nb, D_SSD, SSD_STATE), F32)],
        scratch_shapes=[pltpu.VMEM((t, D_RG), F32), pltpu.VMEM((t, D_RG), F32),
                        pltpu.VMEM((t, D_SSD), F32), pltpu.VMEM((1, D_RG), F32)],
        compiler_params=_params("parallel", "arbitrary"),
        name="mixer_prompt",
    )(u, xbc, gate, z, x, w_dt, wa.astype(BF16), wx.astype(BF16),
      ba.reshape(1, D_RG), bx.reshape(1, D_RG), lam.reshape(1, D_RG),
      _pad_lanes(dt_bias), _pad_lanes(a_log),
      jnp.repeat(d_skip, SSD_HEAD_DIM).reshape(1, D_SSD), norm_w.reshape(1, D_SSD))
    return ymix, hlast.reshape(nb, D_RG), state


def _moba_prompt_kernel(slopes_ref, q_ref, k_ref, v_ref, o_ref, kb_ref, vt_ref, pen_ref):
    blk = MOBA_BLOCK
    seq = k_ref.shape[0]
    nblk = seq // blk
    slope = slopes_ref[pl.program_id(1)] * LOG2_E
    scale = HEAD_DIM ** -0.5 * LOG2_E

    kmeans = []
    for n in range(nblk):
        rows = slice(n * blk, (n + 1) * blk)
        kn = k_ref[rows, :]
        kmeans.append(jnp.mean(kn, axis=0, keepdims=True))
        kb_ref[rows, :] = kn.astype(BF16)
        vt_ref[0:HEAD_DIM, rows] = v_ref[rows, :].T.astype(BF16)
    vt_ref[HEAD_DIM:, :] = jnp.ones((vt_ref.shape[0] - HEAD_DIM, seq), BF16)
    kmean = jnp.concatenate(kmeans, axis=0)
    pen_ref[...] = slope * (lax.broadcasted_iota(jnp.int32, (seq, blk), 1)
                            - lax.broadcasted_iota(jnp.int32, (seq, blk), 0)).astype(F32)
    blk_id = lax.broadcasted_iota(jnp.int32, (nblk, blk), 0)
    causal = (lax.broadcasted_iota(jnp.int32, (blk, blk), 0) <= lax.broadcasted_iota(jnp.int32, (blk, blk), 1))

    for qi in range(nblk):
        q = q_ref[qi * blk:(qi + 1) * blk, :]
        qb = q.astype(BF16)
        base = -slope * float(qi * blk)
        if qi > MOBA_TOPK:
            gate = lax.dot_general(kmean, q, NT_DIMS, precision=lax.Precision.HIGHEST,
                                   preferred_element_type=F32)
            gate = jnp.where(blk_id < qi, gate, -jnp.inf)
            ahead = jnp.zeros((nblk, blk), jnp.int32)
            for n in range(qi):
                gate_n = gate[n:n + 1, :]
                before = jnp.logical_or(gate_n > gate, jnp.logical_and(gate_n == gate, n < blk_id))
                ahead = ahead + before.astype(jnp.int32)
            row_terms = jnp.where(ahead < MOBA_TOPK, base, -jnp.inf)
        else:
            row_terms = jnp.full((nblk, blk), base, F32)

        s_blocks = []
        for n in range(qi + 1):
            rows = slice(n * blk, (n + 1) * blk)
            s = lax.dot_general(kb_ref[rows, :], qb, NT_DIMS, preferred_element_type=F32)
            s = s * scale - pen_ref[rows, :]
            if n < qi:
                s = s + row_terms[n:n + 1, :]
            else:
                s = jnp.where(causal, s + base, -jnp.inf)
            s_blocks.append(s)
        m = jnp.max(s_blocks[0], axis=0, keepdims=True)
        for s in s_blocks[1:]:
            m = jnp.maximum(m, jnp.max(s, axis=0, keepdims=True))
        acc = jnp.zeros((vt_ref.shape[0], blk), F32)
        for n, s in enumerate(s_blocks):
            p = jnp.exp2(s - m)
            acc = acc + jnp.dot(vt_ref[:, n * blk:(n + 1) * blk], p.astype(BF16), preferred_element_type=F32)
        denom = acc[HEAD_DIM:HEAD_DIM + 1, :]
        o_ref[qi * blk:(qi + 1) * blk, :] = (acc[0:HEAD_DIM, :] / denom).T.astype(o_ref.dtype)


def alibi_slopes():
    return jnp.exp2(-8.0 * jnp.arange(1, N_HEADS + 1, dtype=F32) / N_HEADS)


def moba_prompt(q, k, v, nb):
    m = q.shape[0]
    l = m // nb
    head_spec = pl.BlockSpec((l, HEAD_DIM), lambda bi, hi: (bi, hi))
    return pl.pallas_call(
        _moba_prompt_kernel,
        grid=(nb, N_HEADS),
        in_specs=[pl.BlockSpec(memory_space=pltpu.SMEM), head_spec, head_spec, head_spec],
        out_specs=head_spec,
        out_shape=jax.ShapeDtypeStruct((m, D_MODEL), BF16),
        scratch_shapes=[pltpu.VMEM((l, HEAD_DIM), BF16), pltpu.VMEM((HEAD_DIM + BF16_SUBLANES, l), BF16),
                        pltpu.VMEM((l, MOBA_BLOCK), F32)],
        compiler_params=_params("parallel", "parallel"),
        name="moba_prompt",
    )(alibi_slopes(), q, k, v)


def _mixer_step_a_kernel(proj_ref, x_ref, b0_ref, b1_ref, b2_ref, h0_ref,
                         wdt_ref, convw_ref, convb_ref, wa_ref, wx_ref, ba_ref, bx_ref, lam_ref, dtb_ref,
                         yrg_ref, hnew_ref, xbc_ref, dt_ref):
    conv = (convb_ref[...] + convw_ref[0:1, :] * b0_ref[...] + convw_ref[1:2, :] * b1_ref[...]
            + convw_ref[2:3, :] * b2_ref[...] + convw_ref[3:4, :] * proj_ref[:, 0:D_CONV])
    xbc_ref[...] = _silu(conv[:, D_RG:])
    dt_raw = jnp.dot(x_ref[...].astype(BF16), wdt_ref[...], preferred_element_type=F32)
    dt_ref[...] = _softplus(dt_raw[:, 0:SSD_HEADS] + dtb_ref[...])
    for hd in range(RG_HEADS):
        lo = hd * RG_BW
        u = conv[:, lo:lo + RG_BW]
        ub = u.astype(BF16)
        r = jax.nn.sigmoid(jnp.dot(ub, wa_ref[hd], preferred_element_type=F32) + ba_ref[:, lo:lo + RG_BW])
        i = jax.nn.sigmoid(jnp.dot(ub, wx_ref[hd], preferred_element_type=F32) + bx_ref[:, lo:lo + RG_BW])
        log_a = -RG_C * r * _softplus(-lam_ref[:, lo:lo + RG_BW])
        a = jnp.exp(log_a)
        h = a * h0_ref[:, lo:lo + RG_BW] + _sqrt_one_minus_sq(log_a, a) * i * u
        hnew_ref[:, lo:lo + RG_BW] = h
        gate = proj_ref[:, D_CONV + lo:D_CONV + lo + RG_BW]
        yrg_ref[:, lo:lo + RG_BW] = h * jax.nn.gelu(gate)


def _ssd_step_kernel(s0_ref, xs_ref, dt_ref, alog_ref, b_ref, c_ref, snew_ref, y_ref):
    dt = dt_ref[...]
    decay = jnp.exp(dt * (-jnp.exp(alog_ref[...])))
    s_new = s0_ref[...] * decay + (xs_ref[...] * dt) * b_ref[...]
    snew_ref[...] = s_new
    y_ref[...] = jnp.sum(s_new * c_ref[...], axis=-1, keepdims=True)


def _mixer_step_b_kernel(yrg_ref, y_ref, xs_ref, z_ref, dskip_ref, normw_ref, o_ref):
    o_ref[:, 0:D_RG] = yrg_ref[...].astype(BF16)
    yz = (y_ref[...] + dskip_ref[...] * xs_ref[...]) * _silu(z_ref[...])
    gw = D_SSD // SSD_GROUPS
    for g in range(SSD_GROUPS):
        blk = yz[:, g * gw:(g + 1) * gw]
        ms = jnp.mean(blk * blk, axis=-1, keepdims=True)
        o_ref[:, D_RG + g * gw:D_RG + (g + 1) * gw] = (
            blk * lax.rsqrt(ms + RMS_EPS) * normw_ref[:, g * gw:(g + 1) * gw]).astype(BF16)


def _whole(shape):
    return pl.BlockSpec(shape, lambda *_: (0,) * len(shape))


def mixer_sample(proj, x, conv_buf, h0, s0, w_dt, conv_w, conv_b, wa, wx, ba, bx, lam, dt_bias, a_log, d_skip,
                 norm_w):
    nb = proj.shape[0]
    hpg = SSD_HEADS // SSD_GROUPS
    a_ins = (proj, x, conv_buf[:, 0], conv_buf[:, 1], conv_buf[:, 2], h0,
             w_dt, conv_w, conv_b.reshape(1, D_CONV), wa.astype(BF16), wx.astype(BF16),
             ba.reshape(1, D_RG), bx.reshape(1, D_RG), lam.reshape(1, D_RG), dt_bias.reshape(1, SSD_HEADS))
    yrg, hnew, xbc, dt = pl.pallas_call(
        _mixer_step_a_kernel,
        grid=(1,),
        in_specs=[_whole(a.shape) for a in a_ins],
        out_specs=[_whole((nb, D_RG)), _whole((nb, D_RG)), _whole((nb, D_XBC)), _whole((nb, SSD_HEADS))],
        out_shape=[jax.ShapeDtypeStruct((nb, D_RG), F32), jax.ShapeDtypeStruct((nb, D_RG), F32),
                   jax.ShapeDtypeStruct((nb, D_XBC), F32), jax.ShapeDtypeStruct((nb, SSD_HEADS), F32)],
        compiler_params=_params("arbitrary"),
        name="mixer_step_a",
    )(*a_ins)
    xs = xbc[:, :D_SSD]
    snew, y = pl.pallas_call(
        _ssd_step_kernel,
        grid=(nb, SSD_GROUPS),
        in_specs=[pl.BlockSpec((None, hpg, SSD_HEAD_DIM, SSD_STATE), lambda bi, gi: (bi, gi, 0, 0)),
                  pl.BlockSpec((None, hpg, SSD_HEAD_DIM, 1), lambda bi, gi: (bi, gi, 0, 0)),
                  pl.BlockSpec((None, hpg, 1, 1), lambda bi, gi: (bi, gi, 0, 0)),
                  pl.BlockSpec((hpg, 1, 1), lambda bi, gi: (gi, 0, 0)),
                  pl.BlockSpec((None, None, 1, SSD_STATE), lambda bi, gi: (bi, gi, 0, 0)),
                  pl.BlockSpec((None, None, 1, SSD_STATE), lambda bi, gi: (bi, gi, 0, 0))],
        out_specs=[pl.BlockSpec((None, hpg, SSD_HEAD_DIM, SSD_STATE), lambda bi, gi: (bi, gi, 0, 0)),
                   pl.BlockSpec((None, hpg, SSD_HEAD_DIM, 1), lambda bi, gi: (bi, gi, 0, 0))],
        out_shape=[jax.ShapeDtypeStruct((nb, SSD_HEADS, SSD_HEAD_DIM, SSD_STATE), F32),
                   jax.ShapeDtypeStruct((nb, SSD_HEADS, SSD_HEAD_DIM, 1), F32)],
        compiler_params=_params("parallel", "parallel"),
        name="ssd_step",
    )(s0, xs.reshape(nb, SSD_HEADS, SSD_HEAD_DIM, 1), dt.reshape(nb, SSD_HEADS, 1, 1),
      a_log.reshape(SSD_HEADS, 1, 1),
      xbc[:, D_SSD:D_SSD + SSD_GN].reshape(nb, SSD_GROUPS, 1, SSD_STATE),
      xbc[:, D_SSD + SSD_GN:].reshape(nb, SSD_GROUPS, 1, SSD_STATE))
    b_ins = (yrg, y.reshape(nb, D_SSD), xs, proj[:, D_CONV + D_RG:D_CONV + D_RG + D_SSD],
             jnp.repeat(d_skip, SSD_HEAD_DIM).reshape(1, D_SSD), norm_w.reshape(1, D_SSD))
    ymix = pl.pallas_call(
        _mixer_step_b_kernel,
        grid=(1,),
        in_specs=[_whole(a.shape) for a in b_ins],
        out_specs=_whole((nb, D_MIX)),
        out_shape=jax.ShapeDtypeStruct((nb, D_MIX), BF16),
        compiler_params=_params("arbitrary"),
        name="mixer_step_b",
    )(*b_ins)
    return ymix, hnew, snew


def _moba_topk_kernel(q_ref, ksum_ref, idx_ref):
    nblk = ksum_ref.shape[0]
    qh = q_ref[...] * (1.0 / MOBA_BLOCK)
    lane = lax.broadcasted_iota(jnp.int32, (N_HEADS, LANES), 1)
    gate = jnp.full((N_HEADS, LANES), -jnp.inf, F32)
    for n in range(nblk):
        gate = jnp.where(lane == n, jnp.sum(ksum_ref[n] * qh, axis=-1, keepdims=True), gate)
    out = jnp.zeros((N_HEADS, LANES), jnp.int32)
    for kk in range(MOBA_TOPK):
        best = jnp.max(gate, axis=-1, keepdims=True)
        pick = jnp.min(jnp.where(gate == best, lane, LANES), axis=-1, keepdims=True)
        out = jnp.where(lane == kk, pick, out)
        gate = jnp.where(lane == pick, -jnp.inf, gate)
    idx_ref[...] = out


def moba_topk(q, ksum):
    nb, nblk = ksum.shape[:2]
    assert MOBA_TOPK <= nblk <= LANES
    return pl.pallas_call(
        _moba_topk_kernel,
        grid=(nb,),
        in_specs=[pl.BlockSpec((None, N_HEADS, HEAD_DIM), lambda bi: (bi, 0, 0)),
                  pl.BlockSpec((None, nblk, N_HEADS, HEAD_DIM), lambda bi: (bi, 0, 0, 0))],
        out_specs=pl.BlockSpec((None, N_HEADS, LANES), lambda bi: (bi, 0, 0)),
        out_shape=jax.ShapeDtypeStruct((nb, N_HEADS, LANES), jnp.int32),
        compiler_params=_params("parallel"),
        name="moba_topk",
    )(q, ksum)


def _moba_decode_kernel(pt_ref, top_ref, slopes_ref, q_ref, kn_ref, vn_ref, ck_hbm, cv_hbm, o_ref,
                        kbuf, vbuf, sem, *, past_len):
    bi = pl.program_id(0)
    scale = HEAD_DIM ** -0.5
    slots = MOBA_TOPK * PAGES_PER_BLOCK

    def head_copies(hd):
        copies = []
        for kk in range(MOBA_TOPK):
            blk_id = top_ref[bi, hd, kk]
            for half in range(PAGES_PER_BLOCK):
                page = pt_ref[bi, PAGES_PER_BLOCK * blk_id + half]
                slot = kk * PAGES_PER_BLOCK + half
                copies.append(pltpu.make_async_copy(ck_hbm.at[0, page, :, hd, :], kbuf.at[hd, slot],
                                                    sem.at[0, hd, slot]))
                copies.append(pltpu.make_async_copy(cv_hbm.at[0, page, :, hd, :], vbuf.at[hd, slot],
                                                    sem.at[1, hd, slot]))
        return copies

    all_copies = [head_copies(hd) for hd in range(N_HEADS)]
    for copies in all_copies:
        for cp in copies:
            cp.start()

    key_off = lax.broadcasted_iota(jnp.int32, (1, MOBA_BLOCK), 1)
    for hd in range(N_HEADS):
        for cp in all_copies[hd]:
            cp.wait()
        lo = hd * HEAD_DIM
        slope = slopes_ref[hd]
        q = q_ref[:, lo:lo + HEAD_DIM]
        q8 = jnp.broadcast_to(q, (SUBLANES, HEAD_DIM)).astype(BF16)
        kh = kbuf[hd].reshape(slots * PAGE_SIZE, HEAD_DIM).astype(BF16)
        vh = vbuf[hd].reshape(slots * PAGE_SIZE, HEAD_DIM).astype(BF16)
        dist = jnp.concatenate(
            [(past_len - top_ref[bi, hd, kk] * MOBA_BLOCK - key_off).astype(F32) for kk in range(MOBA_TOPK)],
            axis=1)
        s = lax.dot_general(q8, kh, NT_DIMS, preferred_element_type=F32)[0:1, :] * scale - slope * dist
        s_own = jnp.sum(q * kn_ref[:, lo:lo + HEAD_DIM], axis=-1, keepdims=True) * scale
        m = jnp.maximum(jnp.max(s, axis=-1, keepdims=True), s_own)
        p = jnp.exp(s - m)
        p_own = jnp.exp(s_own - m)
        denom = jnp.sum(p, axis=-1, keepdims=True) + p_own
        p8 = jnp.broadcast_to(p, (SUBLANES, slots * PAGE_SIZE)).astype(BF16)
        acc = jnp.dot(p8, vh, preferred_element_type=F32)[0:1, :] + p_own * vn_ref[:, lo:lo + HEAD_DIM]
        o_ref[:, lo:lo + HEAD_DIM] = acc / denom


def moba_decode(q, k_new, v_new, cache_k, cache_v, page_table, top_idx):
    nb, n_pages = page_table.shape
    slots = MOBA_TOPK * PAGES_PER_BLOCK
    row_spec = pl.BlockSpec((None, 1, D_MODEL), lambda bi, pt, top: (bi, 0, 0))
    return pl.pallas_call(
        functools.partial(_moba_decode_kernel, past_len=n_pages * PAGE_SIZE),
        grid_spec=pltpu.PrefetchScalarGridSpec(
            num_scalar_prefetch=2,
            grid=(nb,),
            in_specs=[pl.BlockSpec(memory_space=pltpu.SMEM), row_spec, row_spec, row_spec,
                      pl.BlockSpec(memory_space=pl.ANY), pl.BlockSpec(memory_space=pl.ANY)],
            out_specs=row_spec,
            scratch_shapes=[pltpu.VMEM((N_HEADS, slots, PAGE_SIZE, HEAD_DIM), F32),
                            pltpu.VMEM((N_HEADS, slots, PAGE_SIZE, HEAD_DIM), F32),
                            pltpu.SemaphoreType.DMA((2, N_HEADS, slots))]),
        out_shape=jax.ShapeDtypeStruct((nb, 1, D_MODEL), F32),
        compiler_params=_params("arbitrary"),
        name="moba_decode",
    )(page_table, top_idx, alibi_slopes(), q, k_new, v_new, cache_k, cache_v)


def _trunk_prompt(x, w, nb, cache_k, page_table):
    tm = 512
    n_seq = page_table.shape[0]
    half = n_seq // DEPTH
    conv_w, conv_b, *mixer_w = w["mixer"]
    proj_tiles = dict(tm=1024, tn=1024)
    cw, cb = conv_w, conv_b.reshape(1, D_CONV)
    u, tail_u = in_proj(x, w["w_in"], 0, D_RG, "conv", nb, cw, cb, **proj_tiles)
    xbc, tail_xbc = in_proj(x, w["w_in"], D_RG, D_XBC, "conv_silu", nb, cw, cb, **proj_tiles)
    gate = in_proj(x, w["w_in"], D_CONV, D_RG, "gelu", nb, **proj_tiles)
    z = in_proj(x, w["w_in"], D_CONV + D_RG, D_SSD, "silu", nb, **proj_tiles)
    conv_tail = jnp.concatenate([tail_u, tail_xbc], axis=2)[:, SUBLANES - (CONV_W - 1):, :]
    ymix, h_last, s_last = mixer_prompt(u, xbc, gate, z, x, nb, w["w_dt"], *mixer_w)
    x = matmul_ln(ymix, w["w_out"], x, w["ln1_g"][0], w["ln1_b"][0], tm=tm, tk=2048)
    x, ksum0 = mlp_ln_ksum(x, w["w_up"], w["w_down"], 0, w["ln2_g"][0], w["ln2_b"][0],
                           cache_k, page_table, 0, half, tm=tm, tf=1024)
    q, k, v = qkv_proj(x, w["w_qkv"], tm=tm)
    o = moba_prompt(q, k, v, nb)
    x = matmul_ln(o, w["w_o"], x, w["ln1_g"][1], w["ln1_b"][1], tm=tm, tk=2048)
    x, ksum1 = mlp_ln_ksum(x, w["w_up"], w["w_down"], 1, w["ln2_g"][1], w["ln2_b"][1],
                           cache_k, page_table, half, n_seq - half, tm=tm, tf=1024)
    return x, conv_tail, h_last, s_last, k, v, jnp.concatenate([ksum0, ksum1], axis=0)


def _trunk_sample(x, conv_buf, h0, s0, cache_k, cache_v, page_table, ksum, w):
    nb = x.shape[0]
    proj = matmul(x, w["w_in"], tm=nb, tn=1024, n_out=D_IN - SSD_HEADS)
    new_buf = jnp.concatenate([conv_buf[:, 1:], proj[:, None, :D_CONV]], axis=1)
    ymix, h_new, s_new = mixer_sample(proj, x, conv_buf, h0, s0, w["w_dt"], *w["mixer"])
    x = matmul_ln(ymix, w["w_out"], x, w["ln1_g"][0], w["ln1_b"][0], tm=nb, tk=2048)
    x = mlp_ln(x, w["w_up"], w["w_down"], 0, w["ln2_g"][0], w["ln2_b"][0], tm=nb, tf=1024)
    q, k, v = qkv_proj(x, w["w_qkv"], tm=nb)
    top_idx = moba_topk(q.reshape(nb, N_HEADS, HEAD_DIM), ksum)[:, :, :MOBA_TOPK]
    o = moba_decode(q.reshape(nb, 1, D_MODEL), k.reshape(nb, 1, D_MODEL), v.reshape(nb, 1, D_MODEL),
                    cache_k, cache_v, page_table, top_idx).reshape(nb, D_MODEL)
    x = matmul_ln(o, w["w_o"], x, w["ln1_g"][1], w["ln1_b"][1], tm=nb, tk=2048)
    x = mlp_ln(x, w["w_up"], w["w_down"], 1, w["ln2_g"][1], w["ln2_b"][1], tm=nb, tf=1024)
    return x, new_buf, h_new, s_new, k, v


def kernel(x_prompt, x_sample, state_conv, state_rglru, state_ssd, cache_k, cache_v, page_table, w_in, conv_w, conv_b, rg_wa, rg_ba, rg_wx, rg_bx, rg_lambda, ssd_dt_bias, ssd_a_log, ssd_d, ssd_norm_w, w_out, w_qkv, w_o, ln1_g, ln1_b, ln2_g, ln2_b, w_up, w_down):
    bp, lp, _ = x_prompt.shape
    bs = x_sample.shape[0]
    w = {
        "w_in": w_in[0].astype(BF16),
        "w_dt": jnp.pad(w_in[0][:, D_IN - SSD_HEADS:], ((0, 0), (0, LANES - SSD_HEADS))).astype(BF16),
        "w_out": w_out[0].astype(BF16), "w_qkv": w_qkv[0].astype(BF16), "w_o": w_o[0].astype(BF16),
        "w_up": w_up.astype(BF16), "w_down": w_down.astype(BF16),
        "ln1_g": ln1_g, "ln1_b": ln1_b, "ln2_g": ln2_g, "ln2_b": ln2_b,
        "mixer": (conv_w[0], conv_b[0], rg_wa[0], rg_wx[0], rg_ba[0], rg_bx[0], rg_lambda[0],
                  ssd_dt_bias[0], ssd_a_log[0], ssd_d[0], ssd_norm_w[0]),
    }
    yp, conv_p, h_p, s_p, k_p, v_p, ksum = _trunk_prompt(x_prompt.reshape(bp * lp, D_MODEL), w, bp,
                                                         cache_k, page_table)
    ys, conv_s, h_s, s_s, k_s, v_s = _trunk_sample(
        x_sample.reshape(bs, D_MODEL), state_conv[0], state_rglru[0], state_ssd[0],
        cache_k, cache_v, page_table, ksum, w)
    return (yp.reshape(bp, lp, D_MODEL), ys.reshape(bs, 1, D_MODEL),
            conv_p[None], h_p[None], s_p.reshape(1, bp, SSD_HEADS, SSD_HEAD_DIM, SSD_STATE),
            k_p.reshape(1, bp, lp, N_HEADS, HEAD_DIM), v_p.reshape(1, bp, lp, N_HEADS, HEAD_DIM),
            conv_s[None], h_s[None], s_s[None],
            k_s.reshape(1, bs, 1, N_HEADS, HEAD_DIM), v_s.reshape(1, bs, 1, N_HEADS, HEAD_DIM))
```

```python
import functools
import math

import jax
import jax.numpy as jnp
from jax import lax
from jax.experimental import pallas as pl
from jax.experimental.pallas import tpu as pltpu

F32 = jnp.float32
BF16 = jnp.bfloat16

D_MODEL = 2048
DEPTH = 2
PAGE_SIZE = 128
CONV_W = 4
D_RG = D_MODEL
RG_HEADS = 8
RG_BW = D_RG // RG_HEADS
RG_C = 8.0
D_SSD = D_MODEL
SSD_HEAD_DIM = 64
SSD_HEADS = D_SSD // SSD_HEAD_DIM
SSD_GROUPS = 4
SSD_STATE = 128
SSD_CHUNK = 128
SSD_GN = SSD_GROUPS * SSD_STATE
D_XBC = D_SSD + 2 * SSD_GN
D_CONV = D_RG + D_XBC
D_IN = D_CONV + D_RG + D_SSD + SSD_HEADS
D_MIX = D_RG + D_SSD
N_HEADS = 16
HEAD_DIM = D_MODEL // N_HEADS
MOBA_BLOCK = 256
MOBA_TOPK = 3
PAGES_PER_BLOCK = MOBA_BLOCK // PAGE_SIZE
KSUM_BLOCKS_PER_STEP = 2
D_FF = 4 * D_MODEL
ALPHA = (2 * DEPTH) ** 0.25
LN_EPS = 1e-5
RMS_EPS = 1e-6

SUBLANES = 8
BF16_SUBLANES = 16
LANES = 128
LOG2_E = 1.4426950408889634
VMEM_LIMIT_BYTES = 56 * 1024 * 1024

NT_DIMS = (((1,), (1,)), ((), ()))
TN_DIMS = (((0,), (0,)), ((), ()))


def _params(*semantics):
    return pltpu.CompilerParams(dimension_semantics=semantics, vmem_limit_bytes=VMEM_LIMIT_BYTES)


def _bdot(a, b):
    return jnp.dot(a.astype(BF16), b.astype(BF16), preferred_element_type=F32)


def _layer_norm_rows(y, g, b):
    mu = jnp.mean(y, axis=-1, keepdims=True)
    yc = y - mu
    var = jnp.mean(yc * yc, axis=-1, keepdims=True)
    return yc * lax.rsqrt(var + LN_EPS) * g + b


def _softplus(x):
    u = jnp.exp(-jnp.abs(x))
    w = 1.0 + u
    log1p_u = jnp.where(w == 1.0, u, jnp.log(w) * (u / (w - 1.0)))
    return jnp.maximum(x, 0.0) + log1p_u


def _sqrt_one_minus_sq(log_a, a):
    return jnp.sqrt(-jnp.tanh(log_a) * (a * a + 1.0))


def _silu(x):
    return x * jax.nn.sigmoid(x)


def _matmul_kernel(a_ref, w_ref, o_ref, ab_ref):
    @pl.when(pl.program_id(1) == 0)
    def _():
        ab_ref[...] = a_ref[...].astype(BF16)

    o_ref[...] = jnp.dot(ab_ref[...], w_ref[...].astype(BF16), preferred_element_type=F32)


IN_PROJ_COL_CHUNK = 256
IN_PROJ_ROW_CHUNK = 64


def _in_proj_kernel(*refs, kind, blocks_per_seq):
    has_conv = kind in ("conv", "conv_silu")
    if has_conv:
        x_ref, w_ref, cw_ref, cb_ref, o_ref, tail_ref, xb_ref, ybuf, carry = refs
    else:
        x_ref, w_ref, o_ref, xb_ref = refs
    i = pl.program_id(0)
    j = pl.program_id(1)
    tm, tn = o_ref.shape
    front = SUBLANES

    @pl.when(j == 0)
    def _():
        xb_ref[...] = x_ref[...].astype(BF16)

    if has_conv:
        @pl.when(i % blocks_per_seq == 0)
        def _():
            ybuf[0:front, :] = jnp.zeros((front, tn), F32)

        @pl.when(i % blocks_per_seq != 0)
        def _():
            ybuf[0:front, :] = carry[j]

    for c0 in range(0, tn, IN_PROJ_COL_CHUNK):
        cols = slice(c0, c0 + IN_PROJ_COL_CHUNK)
        y = jnp.dot(xb_ref[...], w_ref[:, cols].astype(BF16), preferred_element_type=F32)
        if has_conv:
            ybuf[front:front + tm, cols] = y
        for r0 in range(0, tm, IN_PROJ_ROW_CHUNK):
            rows = slice(r0, r0 + IN_PROJ_ROW_CHUNK)
            if has_conv:
                ext = ybuf[r0:r0 + front + IN_PROJ_ROW_CHUNK, cols]
                acc = cb_ref[:, cols] + cw_ref[CONV_W - 1:CONV_W, cols] * ext[front:, :]
                for k in range(CONV_W - 1):
                    shifted = pltpu.roll(ext, CONV_W - 1 - k, axis=0)[front:, :]
                    acc = acc + cw_ref[k:k + 1, cols] * shifted
                o_ref[rows, cols] = _silu(acc) if kind == "conv_silu" else acc
            elif kind == "gelu":
                o_ref[rows, cols] = jax.nn.gelu(y[rows, :])
            else:
                o_ref[rows, cols] = _silu(y[rows, :])

    if has_conv:
        last = ybuf[tm:tm + front, :]
        carry[j] = last
        tail_ref[...] = last


def in_proj(x, w, col0, n_out, kind, nb, conv_w=None, conv_b=None, *, tm, tn):
    m, k = x.shape
    off = col0 // tn
    grid = (m // tm, n_out // tn)
    blocks_per_seq = (m // nb) // tm
    has_conv = kind in ("conv", "conv_silu")
    in_specs = [pl.BlockSpec((tm, k), lambda i, j: (i, 0)),
                pl.BlockSpec((k, tn), lambda i, j: (0, j + off))]
    out_specs = [pl.BlockSpec((tm, tn), lambda i, j: (i, j))]
    out_shape = [jax.ShapeDtypeStruct((m, n_out), F32)]
    scratch = [pltpu.VMEM((tm, k), BF16)]
    operands = [x, w]
    if has_conv:
        in_specs += [pl.BlockSpec((CONV_W, tn), lambda i, j: (0, j + off)),
                     pl.BlockSpec((1, tn), lambda i, j: (0, j + off))]
        out_specs += [pl.BlockSpec((None, SUBLANES, tn), lambda i, j: (i, 0, j))]
        out_shape += [jax.ShapeDtypeStruct((grid[0], SUBLANES, n_out), F32)]
        scratch += [pltpu.VMEM((tm + SUBLANES, tn), F32), pltpu.VMEM((grid[1], SUBLANES, tn), F32)]
        operands += [conv_w, conv_b]
    outs = pl.pallas_call(
        functools.partial(_in_proj_kernel, kind=kind, blocks_per_seq=blocks_per_seq),
        grid=grid,
        in_specs=in_specs,
        out_specs=out_specs,
        out_shape=out_shape,
        scratch_shapes=scratch,
        compiler_params=_params("arbitrary", "arbitrary"),
        name="in_proj_" + kind,
    )(*operands)
    if not has_conv:
        return outs[0]
    return outs[0], outs[1][blocks_per_seq - 1::blocks_per_seq]


def matmul(a, w, *, tm, tn, n_out=None):
    m, k = a.shape
    n = w.shape[1] if n_out is None else n_out
    return pl.pallas_call(
        _matmul_kernel,
        grid=(pl.cdiv(m, tm), pl.cdiv(n, tn)),
        in_specs=[pl.BlockSpec((tm, k), lambda i, j: (i, 0)),
                  pl.BlockSpec((k, tn), lambda i, j: (0, j))],
        out_specs=pl.BlockSpec((tm, tn), lambda i, j: (i, j)),
        out_shape=jax.ShapeDtypeStruct((m, n), F32),
        scratch_shapes=[pltpu.VMEM((tm, k), BF16)],
        compiler_params=_params("parallel", "arbitrary"),
        name="matmul",
    )(a, w)


def _resident_w_kernel(a_ref, w_ref, o_ref):
    o_ref[...] = _bdot(a_ref[...], w_ref[...])


def qkv_proj(a, w, *, tm):
    m, k = a.shape
    n = w.shape[1] // 3

    def part(idx):
        return pl.pallas_call(
            _resident_w_kernel,
            grid=(pl.cdiv(m, tm),),
            in_specs=[pl.BlockSpec((tm, k), lambda i: (i, 0)),
                      pl.BlockSpec((k, n), lambda i: (0, idx), pipeline_mode=pl.Buffered(1))],
            out_specs=pl.BlockSpec((tm, n), lambda i: (i, 0)),
            out_shape=jax.ShapeDtypeStruct((m, n), F32),
            compiler_params=_params("parallel"),
            name="qkv_proj",
        )(a, w)

    return part(0), part(1), part(2)


LN_ROW_CHUNK = 256


def _matmul_ln_kernel(a_ref, w_ref, res_ref, g_ref, b_ref, o_ref):
    tm = o_ref.shape[0]
    chunk = min(tm, LN_ROW_CHUNK)
    for r0 in range(0, tm, chunk):
        rows = slice(r0, r0 + chunk)
        y = ALPHA * res_ref[rows, :] + _bdot(a_ref[rows, :], w_ref[...])
        o_ref[rows, :] = _layer_norm_rows(y, g_ref[...], b_ref[...])


def matmul_ln(a, w, res, g, b, *, tm):
    m, k = a.shape
    n = w.shape[1]
    return pl.pallas_call(
        _matmul_ln_kernel,
        grid=(pl.cdiv(m, tm),),
        in_specs=[pl.BlockSpec((tm, k), lambda i: (i, 0)),
                  pl.BlockSpec((k, n), lambda i: (0, 0), pipeline_mode=pl.Buffered(1)),
                  pl.BlockSpec((tm, n), lambda i: (i, 0)),
                  pl.BlockSpec((1, n), lambda i: (0, 0)),
                  pl.BlockSpec((1, n), lambda i: (0, 0))],
        out_specs=pl.BlockSpec((tm, n), lambda i: (i, 0)),
        out_shape=jax.ShapeDtypeStruct((m, n), F32),
        compiler_params=_params("parallel"),
        name="matmul_ln",
    )(a, w, res, g.reshape(1, n), b.reshape(1, n))


def _mlp_ln_body(x_ref, wu_ref, wd_ref, g_ref, b_ref, o_ref, xb_ref, shadow_work=None):
    f = pl.program_id(1)

    @pl.when(f == 0)
    def _():
        xb_ref[...] = x_ref[...].astype(BF16)
        o_ref[...] = jnp.zeros_like(o_ref)

    if shadow_work is not None:
        shadow_work()
    hdn = jnp.maximum(jnp.dot(xb_ref[...], wu_ref[...], preferred_element_type=F32), 0.0)
    o_ref[...] += _bdot(hdn * hdn, wd_ref[...])

    @pl.when(f == pl.num_programs(1) - 1)
    def _():
        y = ALPHA * x_ref[...] + o_ref[...]
        o_ref[...] = _layer_norm_rows(y, g_ref[...], b_ref[...])


def _mlp_ln_kernel(x_ref, wu_ref, wd_ref, g_ref, b_ref, o_ref, xb_ref):
    _mlp_ln_body(x_ref, wu_ref, wd_ref, g_ref, b_ref, o_ref, xb_ref)


def _mlp_ln_ksum_kernel(pt_ref, x_ref, wu_ref, wd_ref, g_ref, b_ref, ck_hbm, o_ref, ksum_ref,
                        xb_ref, kbuf, sem, *, seq0, nblk):
    step = pl.program_id(0) * pl.num_programs(1) + pl.program_id(1)
    n_steps = pl.num_programs(0) * pl.num_programs(1)
    pages_per_step = KSUM_BLOCKS_PER_STEP * PAGES_PER_BLOCK
    slot = step % 2

    def page_copies(at_step, at_slot):
        copies = []
        for pg in range(pages_per_step):
            flat = at_step * KSUM_BLOCKS_PER_STEP + pg // PAGES_PER_BLOCK
            page = pt_ref[seq0 + flat // nblk, (flat % nblk) * PAGES_PER_BLOCK + pg % PAGES_PER_BLOCK]
            copies.append(pltpu.make_async_copy(ck_hbm.at[0, page], kbuf.at[at_slot, pg], sem.at[at_slot, pg]))
        return copies

    @pl.when(step == 0)
    def _():
        for cp in page_copies(0, 0):
            cp.start()

    @pl.when(step + 1 < n_steps)
    def _():
        for cp in page_copies(step + 1, 1 - slot):
            cp.start()

    def sum_pages():
        for cp in page_copies(step, slot):
            cp.wait()
        for j in range(KSUM_BLOCKS_PER_STEP):
            acc = jnp.sum(kbuf[slot, j * PAGES_PER_BLOCK], axis=0)
            for p in range(1, PAGES_PER_BLOCK):
                acc = acc + jnp.sum(kbuf[slot, j * PAGES_PER_BLOCK + p], axis=0)
            ksum_ref[j] = acc

    _mlp_ln_body(x_ref, wu_ref, wd_ref, g_ref, b_ref, o_ref, xb_ref, sum_pages)


def mlp_ln(x, w_up, w_down, layer, g, b, *, tm, tf):
    m, d = x.shape
    ff = w_up.shape[2]
    return pl.pallas_call(
        _mlp_ln_kernel,
        grid=(pl.cdiv(m, tm), ff // tf),
        in_specs=[pl.BlockSpec((tm, d), lambda i, f: (i, 0)),
                  pl.BlockSpec((None, d, tf), lambda i, f: (layer, 0, f)),
                  pl.BlockSpec((None, tf, d), lambda i, f: (layer, f, 0)),
                  pl.BlockSpec((1, d), lambda i, f: (0, 0)),
                  pl.BlockSpec((1, d), lambda i, f: (0, 0))],
        out_specs=pl.BlockSpec((tm, d), lambda i, f: (i, 0)),
        out_shape=jax.ShapeDtypeStruct((m, d), F32),
        scratch_shapes=[pltpu.VMEM((tm, d), BF16)],
        compiler_params=_params("parallel", "arbitrary"),
        name="mlp_ln",
    )(x, w_up, w_down, g.reshape(1, d), b.reshape(1, d))


def mlp_ln_ksum(x, w_up, w_down, layer, g, b, cache_k, page_table, seq0, n_seq, *, tm, tf):
    m, d = x.shape
    ff = w_up.shape[2]
    grid = (m // tm, ff // tf)
    nblk = page_table.shape[1] // PAGES_PER_BLOCK
    assert n_seq * nblk == grid[0] * grid[1] * KSUM_BLOCKS_PER_STEP and nblk % KSUM_BLOCKS_PER_STEP == 0
    steps_per_seq = nblk // KSUM_BLOCKS_PER_STEP

    def ksum_index(i, f, pt):
        step = i * grid[1] + f
        return (step // steps_per_seq, step % steps_per_seq, 0, 0)

    return pl.pallas_call(
        functools.partial(_mlp_ln_ksum_kernel, seq0=seq0, nblk=nblk),
        grid_spec=pltpu.PrefetchScalarGridSpec(
            num_scalar_prefetch=1,
            grid=grid,
            in_specs=[pl.BlockSpec((tm, d), lambda i, f, pt: (i, 0)),
                      pl.BlockSpec((None, d, tf), lambda i, f, pt: (layer, 0, f)),
                      pl.BlockSpec((None, tf, d), lambda i, f, pt: (layer, f, 0)),
                      pl.BlockSpec((1, d), lambda i, f, pt: (0, 0)),
                      pl.BlockSpec((1, d), lambda i, f, pt: (0, 0)),
                      pl.BlockSpec(memory_space=pl.ANY)],
            out_specs=[pl.BlockSpec((tm, d), lambda i, f, pt: (i, 0)),
                       pl.BlockSpec((None, KSUM_BLOCKS_PER_STEP, N_HEADS, HEAD_DIM), ksum_index)],
            scratch_shapes=[pltpu.VMEM((tm, d), BF16),
                            pltpu.VMEM((2, KSUM_BLOCKS_PER_STEP * PAGES_PER_BLOCK, PAGE_SIZE, N_HEADS, HEAD_DIM), F32),
                            pltpu.SemaphoreType.DMA((2, KSUM_BLOCKS_PER_STEP * PAGES_PER_BLOCK))]),
        out_shape=[jax.ShapeDtypeStruct((m, d), F32),
                   jax.ShapeDtypeStruct((n_seq, nblk, N_HEADS, HEAD_DIM), F32)],
        compiler_params=_params("arbitrary", "arbitrary"),
        name="mlp_ln_ksum",
    )(page_table, x, w_up, w_down, g.reshape(1, d), b.reshape(1, d), cache_k)


def _mixer_prompt_kernel(u_ref, xbc_ref, gate_ref, z_ref, x_ref,
                         wdt_ref, wa_ref, wx_ref, ba_ref, bx_ref, lam_ref,
                         dtb_ref, alog_ref, dskip_ref, normw_ref,
                         ymix_ref, hlast_ref, state_ref,
                         a_scr, h_scr, y_scr, decay_rate_scr):
    t = SSD_CHUNK
    c = pl.program_id(1)

    @pl.when(c == 0)
    def _():
        hlast_ref[...] = jnp.zeros_like(hlast_ref)
        state_ref[...] = jnp.zeros_like(state_ref)
        decay_rate_scr[...] = -RG_C * _softplus(-lam_ref[...])

    row = lax.broadcasted_iota(jnp.int32, (t, RG_BW), 0)
    row_in_tile = row % SUBLANES
    is_start = jnp.logical_and(c == 0, row == 0)
    for hd in range(RG_HEADS):
        lo = hd * RG_BW
        u = u_ref[:, lo:lo + RG_BW]
        ub = u.astype(BF16)
        r = jax.nn.sigmoid(jnp.dot(ub, wa_ref[hd], preferred_element_type=F32) + ba_ref[:, lo:lo + RG_BW])
        i = jax.nn.sigmoid(jnp.dot(ub, wx_ref[hd], preferred_element_type=F32) + bx_ref[:, lo:lo + RG_BW])
        log_a = decay_rate_scr[:, lo:lo + RG_BW] * r
        a_raw = jnp.exp(log_a)
        a = jnp.where(is_start, 0.0, a_raw)
        mult = jnp.where(is_start, 1.0, _sqrt_one_minus_sq(log_a, a_raw))
        bt = mult * i * u
        for s in (1, 2, 4):
            keep = row_in_tile >= s
            a_sh = pltpu.roll(a, s, axis=0)
            b_sh = pltpu.roll(bt, s, axis=0)
            bt = jnp.where(keep, a * b_sh + bt, bt)
            a = jnp.where(keep, a * a_sh, a)
        a_scr[:, lo:lo + RG_BW] = a
        h_scr[:, lo:lo + RG_BW] = bt

    def carry_tile(gi, carry):
        r0 = pl.multiple_of(gi * SUBLANES, SUBLANES)
        h = a_scr[pl.ds(r0, SUBLANES), :] * carry + h_scr[pl.ds(r0, SUBLANES), :]
        h_scr[pl.ds(r0, SUBLANES), :] = h
        return h[SUBLANES - 1:SUBLANES, :]

    hlast_ref[...] = lax.fori_loop(0, t // SUBLANES, carry_tile, hlast_ref[...])
    ymix_ref[:, 0:D_RG] = (h_scr[...] * gate_ref[...]).astype(BF16)

    lane = lax.broadcasted_iota(jnp.int32, (t, LANES), 1)
    rows = lax.broadcasted_iota(jnp.int32, (t, LANES), 0)
    dt_raw = jnp.dot(x_ref[...].astype(BF16), wdt_ref[...], preferred_element_type=F32)
    dt = jnp.where(lane < SSD_HEADS, _softplus(dt_raw + dtb_ref[...]), 0.0)
    cs = dt * (-jnp.exp(alog_ref[...]))
    s = 1
    while s < t:
        cs = cs + jnp.where(rows >= s, pltpu.roll(cs, s, axis=0), 0.0)
        s *= 2
    cs_t = cs.T
    cs_last = cs[t - 1:t, :]
    decay_to_end = jnp.exp(cs_last - cs)
    exp_cs = jnp.exp(cs)
    chunk_decay = jnp.exp(cs_last)

    ii = lax.broadcasted_iota(jnp.int32, (t, t), 0)
    jj = lax.broadcasted_iota(jnp.int32, (t, t), 1)
    causal = jj <= ii
    first_head = lane < SSD_HEAD_DIM
    first_head_rows = rows < SSD_HEAD_DIM

    for g in range(SSD_GROUPS):
        b_lo = D_SSD + g * SSD_STATE
        bmat = xbc_ref[:, b_lo:b_lo + SSD_STATE].astype(BF16)
        cmat = xbc_ref[:, b_lo + SSD_GN:b_lo + SSD_GN + SSD_STATE].astype(BF16)
        cb = lax.dot_general(cmat, bmat, NT_DIMS, preferred_element_type=F32)
        for pair in range(SSD_HEADS // SSD_GROUPS // 2):
            h0 = g * (SSD_HEADS // SSD_GROUPS) + 2 * pair
            lo = h0 * SSD_HEAD_DIM

            def per_head(x2d):
                return jnp.where(first_head, x2d[:, h0:h0 + 1], x2d[:, h0 + 1:h0 + 2])

            xs = xbc_ref[:, lo:lo + LANES]
            xdt = xs * per_head(dt)
            xdt_b = xdt.astype(BF16)
            y_heads = []
            for hh in (h0, h0 + 1):
                seg = cs[:, hh:hh + 1] - cs_t[hh:hh + 1, :]
                lmat = jnp.exp(jnp.where(causal, seg, -jnp.inf))
                y_heads.append(jnp.dot((cb * lmat).astype(BF16), xdt_b, preferred_element_type=F32))
            y = jnp.where(first_head, y_heads[0], y_heads[1])
            s_prev = state_ref[lo:lo + LANES, :]
            y = y + lax.dot_general(cmat, s_prev.astype(BF16), NT_DIMS,
                                    preferred_element_type=F32) * per_head(exp_cs)
            y_scr[:, lo:lo + LANES] = y + dskip_ref[:, lo:lo + LANES] * xs
            decay_rows = jnp.where(first_head_rows, chunk_decay[:, h0:h0 + 1], chunk_decay[:, h0 + 1:h0 + 2])
            state_ref[lo:lo + LANES, :] = s_prev * decay_rows + lax.dot_general(
                (xdt * per_head(decay_to_end)).astype(BF16), bmat, TN_DIMS, preferred_element_type=F32)

    gw = D_SSD // SSD_GROUPS
    for g in range(SSD_GROUPS):
        lo = g * gw
        yz = y_scr[:, lo:lo + gw] * z_ref[:, lo:lo + gw]
        ms = jnp.mean(yz * yz, axis=-1, keepdims=True)
        ymix_ref[:, D_RG + lo:D_RG + lo + gw] = (yz * lax.rsqrt(ms + RMS_EPS)
                                                 * normw_ref[:, lo:lo + gw]).astype(BF16)


def _pad_lanes(v):
    return jnp.zeros((1, LANES), F32).at[0, :v.shape[0]].set(v)


def mixer_prompt(u, xbc, gate, z, x, nb, w_dt, wa, wx, ba, bx, lam, dt_bias, a_log, d_skip, norm_w):
    m = u.shape[0]
    l = m // nb
    t = SSD_CHUNK
    nc = l // t
    row_blk = lambda bi, ci: bi * nc + ci
    rows = lambda width: pl.BlockSpec((t, width), lambda bi, ci: (row_blk(bi, ci), 0))
    full = lambda shape: pl.BlockSpec(shape, lambda bi, ci: (0,) * len(shape))
    in_specs = [rows(D_RG), rows(D_XBC), rows(D_RG), rows(D_SSD), rows(D_MODEL)]
    in_specs += [full((D_MODEL, LANES)),
                 full((RG_HEADS, RG_BW, RG_BW)), full((RG_HEADS, RG_BW, RG_BW)),
                 full((1, D_RG)), full((1, D_RG)), full((1, D_RG)),
                 full((1, LANES)), full((1, LANES)), full((1, D_SSD)), full((1, D_SSD))]
    ymix, hlast, state = pl.pallas_call(
        _mixer_prompt_kernel,
        grid=(nb, nc),
        in_specs=in_specs,
        out_specs=[pl.BlockSpec((t, D_MIX), lambda bi, ci: (row_blk(bi, ci), 0)),
                   pl.BlockSpec((None, 1, D_RG), lambda bi, ci: (bi, 0, 0)),
                   pl.BlockSpec((None, D_SSD, SSD_STATE), lambda bi, ci: (bi, 0, 0))],
        out_shape=[jax.ShapeDtypeStruct((m, D_MIX), BF16),
                   jax.ShapeDtypeStruct((nb, 1, D_RG), F32),
                   jax.ShapeDtypeStruct((nb, D_SSD, SSD_STATE), F32)],
        scratch_shapes=[pltpu.VMEM((t, D_RG), F32), pltpu.VMEM((t, D_RG), F32),
                        pltpu.VMEM((t, D_SSD), F32), pltpu.VMEM((1, D_RG), F32)],
        compiler_params=_params("parallel", "arbitrary"),
        name="mixer_prompt",
    )(u, xbc, gate, z, x, w_dt, wa.astype(BF16), wx.astype(BF16),
      ba.reshape(1, D_RG), bx.reshape(1, D_RG), lam.reshape(1, D_RG),
      _pad_lanes(dt_bias), _pad_lanes(a_log),
      jnp.repeat(d_skip, SSD_HEAD_DIM).reshape(1, D_SSD), norm_w.reshape(1, D_SSD))
    return ymix, hlast.reshape(nb, D_RG), state


def _moba_prompt_kernel(slopes_ref, q_ref, k_ref, v_ref, o_ref, kb_ref, vt_ref, pen_ref):
    blk = MOBA_BLOCK
    seq = k_ref.shape[0]
    nblk = seq // blk
    slope = slopes_ref[pl.program_id(1)] * LOG2_E
    scale = HEAD_DIM ** -0.5 * LOG2_E

    kmeans = []
    for n in range(nblk):
        rows = slice(n * blk, (n + 1) * blk)
        kn = k_ref[rows, :]
        kmeans.append(jnp.mean(kn, axis=0, keepdims=True))
        kb_ref[rows, :] = kn.astype(BF16)
        vt_ref[0:HEAD_DIM, rows] = v_ref[rows, :].T.astype(BF16)
    vt_ref[HEAD_DIM:, :] = jnp.ones((vt_ref.shape[0] - HEAD_DIM, seq), BF16)
    kmean = jnp.concatenate(kmeans, axis=0)
    pen_ref[...] = slope * (lax.broadcasted_iota(jnp.int32, (seq, blk), 1)
                            - lax.broadcasted_iota(jnp.int32, (seq, blk), 0)).astype(F32)
    blk_id = lax.broadcasted_iota(jnp.int32, (nblk, blk), 0)
    causal = (lax.broadcasted_iota(jnp.int32, (blk, blk), 0) <= lax.broadcasted_iota(jnp.int32, (blk, blk), 1))

    for qi in range(nblk):
        q = q_ref[qi * blk:(qi + 1) * blk, :]
        qb = q.astype(BF16)
        base = -slope * float(qi * blk)
        if qi > MOBA_TOPK:
            gate = lax.dot_general(kmean, q, NT_DIMS, precision=lax.Precision.HIGHEST,
                                   preferred_element_type=F32)
            gate = jnp.where(blk_id < qi, gate, -jnp.inf)
            ahead = jnp.zeros((nblk, blk), jnp.int32)
            for n in range(qi):
                gate_n = gate[n:n + 1, :]
                before = jnp.logical_or(gate_n > gate, jnp.logical_and(gate_n == gate, n < blk_id))
                ahead = ahead + before.astype(jnp.int32)
            row_terms = jnp.where(ahead < MOBA_TOPK, base, -jnp.inf)
        else:
            row_terms = jnp.full((nblk, blk), base, F32)

        s_blocks = []
        for n in range(qi + 1):
            rows = slice(n * blk, (n + 1) * blk)
            s = lax.dot_general(kb_ref[rows, :], qb, NT_DIMS, preferred_element_type=F32)
            s = s * scale - pen_ref[rows, :]
            if n < qi:
                s = s + row_terms[n:n + 1, :]
            else:
                s = jnp.where(causal, s + base, -jnp.inf)
            s_blocks.append(s)
        m = jnp.max(s_blocks[0], axis=0, keepdims=True)
        for s in s_blocks[1:]:
            m = jnp.maximum(m, jnp.max(s, axis=0, keepdims=True))
        acc = jnp.zeros((vt_ref.shape[0], blk), F32)
        for n, s in enumerate(s_blocks):
            p = jnp.exp2(s - m)
            acc = acc + jnp.dot(vt_ref[:, n * blk:(n + 1) * blk], p.astype(BF16), preferred_element_type=F32)
        denom = acc[HEAD_DIM:HEAD_DIM + 1, :]
        o_ref[qi * blk:(qi + 1) * blk, :] = (acc[0:HEAD_DIM, :] / denom).T.astype(o_ref.dtype)


def alibi_slopes():
    return jnp.exp2(-8.0 * jnp.arange(1, N_HEADS + 1, dtype=F32) / N_HEADS)


def moba_prompt(q, k, v, nb):
    m = q.shape[0]
    l = m // nb
    head_spec = pl.BlockSpec((l, HEAD_DIM), lambda bi, hi: (bi, hi))
    return pl.pallas_call(
        _moba_prompt_kernel,
        grid=(nb, N_HEADS),
        in_specs=[pl.BlockSpec(memory_space=pltpu.SMEM), head_spec, head_spec, head_spec],
        out_specs=head_spec,
        out_shape=jax.ShapeDtypeStruct((m, D_MODEL), BF16),
        scratch_shapes=[pltpu.VMEM((l, HEAD_DIM), BF16), pltpu.VMEM((HEAD_DIM + BF16_SUBLANES, l), BF16),
                        pltpu.VMEM((l, MOBA_BLOCK), F32)],
        compiler_params=_params("parallel", "parallel"),
        name="moba_prompt",
    )(alibi_slopes(), q, k, v)


def _mixer_step_a_kernel(proj_ref, x_ref, b0_ref, b1_ref, b2_ref, h0_ref,
                         wdt_ref, convw_ref, convb_ref, wa_ref, wx_ref, ba_ref, bx_ref, lam_ref, dtb_ref,
                         yrg_ref, hnew_ref, xbc_ref, dt_ref):
    conv = (convb_ref[...] + convw_ref[0:1, :] * b0_ref[...] + convw_ref[1:2, :] * b1_ref[...]
            + convw_ref[2:3, :] * b2_ref[...] + convw_ref[3:4, :] * proj_ref[:, 0:D_CONV])
    xbc_ref[...] = _silu(conv[:, D_RG:])
    dt_raw = jnp.dot(x_ref[...].astype(BF16), wdt_ref[...], preferred_element_type=F32)
    dt_ref[...] = _softplus(dt_raw[:, 0:SSD_HEADS] + dtb_ref[...])
    for hd in range(RG_HEADS):
        lo = hd * RG_BW
        u = conv[:, lo:lo + RG_BW]
        ub = u.astype(BF16)
        r = jax.nn.sigmoid(jnp.dot(ub, wa_ref[hd], preferred_element_type=F32) + ba_ref[:, lo:lo + RG_BW])
        i = jax.nn.sigmoid(jnp.dot(ub, wx_ref[hd], preferred_element_type=F32) + bx_ref[:, lo:lo + RG_BW])
        log_a = -RG_C * r * _softplus(-lam_ref[:, lo:lo + RG_BW])
        a = jnp.exp(log_a)
        h = a * h0_ref[:, lo:lo + RG_BW] + _sqrt_one_minus_sq(log_a, a) * i * u
        hnew_ref[:, lo:lo + RG_BW] = h
        gate = proj_ref[:, D_CONV + lo:D_CONV + lo + RG_BW]
        yrg_ref[:, lo:lo + RG_BW] = h * jax.nn.gelu(gate)


def _ssd_step_kernel(s0_ref, xs_ref, dt_ref, alog_ref, b_ref, c_ref, snew_ref, y_ref):
    dt = dt_ref[...]
    decay = jnp.exp(dt * (-jnp.exp(alog_ref[...])))
    s_new = s0_ref[...] * decay + (xs_ref[...] * dt) * b_ref[...]
    snew_ref[...] = s_new
    y_ref[...] = jnp.sum(s_new * c_ref[...], axis=-1, keepdims=True)


def _mixer_step_b_kernel(yrg_ref, y_ref, xs_ref, z_ref, dskip_ref, normw_ref, o_ref):
    o_ref[:, 0:D_RG] = yrg_ref[...].astype(BF16)
    yz = (y_ref[...] + dskip_ref[...] * xs_ref[...]) * _silu(z_ref[...])
    gw = D_SSD // SSD_GROUPS
    for g in range(SSD_GROUPS):
        blk = yz[:, g * gw:(g + 1) * gw]
        ms = jnp.mean(blk * blk, axis=-1, keepdims=True)
        o_ref[:, D_RG + g * gw:D_RG + (g + 1) * gw] = (
            blk * lax.rsqrt(ms + RMS_EPS) * normw_ref[:, g * gw:(g + 1) * gw]).astype(BF16)


def _whole(shape):
    return pl.BlockSpec(shape, lambda *_: (0,) * len(shape))


def mixer_sample(proj, x, conv_buf, h0, s0, w_dt, conv_w, conv_b, wa, wx, ba, bx, lam, dt_bias, a_log, d_skip,
                 norm_w):
    nb = proj.shape[0]
    hpg = SSD_HEADS // SSD_GROUPS
    a_ins = (proj, x, conv_buf[:, 0], conv_buf[:, 1], conv_buf[:, 2], h0,
             w_dt, conv_w, conv_b.reshape(1, D_CONV), wa.astype(BF16), wx.astype(BF16),
             ba.reshape(1, D_RG), bx.reshape(1, D_RG), lam.reshape(1, D_RG), dt_bias.reshape(1, SSD_HEADS))
    yrg, hnew, xbc, dt = pl.pallas_call(
        _mixer_step_a_kernel,
        grid=(1,),
        in_specs=[_whole(a.shape) for a in a_ins],
        out_specs=[_whole((nb, D_RG)), _whole((nb, D_RG)), _whole((nb, D_XBC)), _whole((nb, SSD_HEADS))],
        out_shape=[jax.ShapeDtypeStruct((nb, D_RG), F32), jax.ShapeDtypeStruct((nb, D_RG), F32),
                   jax.ShapeDtypeStruct((nb, D_XBC), F32), jax.ShapeDtypeStruct((nb, SSD_HEADS), F32)],
        compiler_params=_params("arbitrary"),
        name="mixer_step_a",
    )(*a_ins)
    xs = xbc[:, :D_SSD]
    snew, y = pl.pallas_call(
        _ssd_step_kernel,
        grid=(nb, SSD_GROUPS),
        in_specs=[pl.BlockSpec((None, hpg, SSD_HEAD_DIM, SSD_STATE), lambda bi, gi: (bi, gi, 0, 0)),
                  pl.BlockSpec((None, hpg, SSD_HEAD_DIM, 1), lambda bi, gi: (bi, gi, 0, 0)),
                  pl.BlockSpec((None, hpg, 1, 1), lambda bi, gi: (bi, gi, 0, 0)),
                  pl.BlockSpec((hpg, 1, 1), lambda bi, gi: (gi, 0, 0)),
                  pl.BlockSpec((None, None, 1, SSD_STATE), lambda bi, gi: (bi, gi, 0, 0)),
                  pl.BlockSpec((None, None, 1, SSD_STATE), lambda bi, gi: (bi, gi, 0, 0))],
        out_specs=[pl.BlockSpec((None, hpg, SSD_HEAD_DIM, SSD_STATE), lambda bi, gi: (bi, gi, 0, 0)),
                   pl.BlockSpec((None, hpg, SSD_HEAD_DIM, 1), lambda bi, gi: (bi, gi, 0, 0))],
        out_shape=[jax.ShapeDtypeStruct((nb, SSD_HEADS, SSD_HEAD_DIM, SSD_STATE), F32),
                   jax.ShapeDtypeStruct((nb, SSD_HEADS, SSD_HEAD_DIM, 1), F32)],
        compiler_params=_params("parallel", "parallel"),
        name="ssd_step",
    )(s0, xs.reshape(nb, SSD_HEADS, SSD_HEAD_DIM, 1), dt.reshape(nb, SSD_HEADS, 1, 1),
      a_log.reshape(SSD_HEADS, 1, 1),
      xbc[:, D_SSD:D_SSD + SSD_GN].reshape(nb, SSD_GROUPS, 1, SSD_STATE),
      xbc[:, D_SSD + SSD_GN:].reshape(nb, SSD_GROUPS, 1, SSD_STATE))
    b_ins = (yrg, y.reshape(nb, D_SSD), xs, proj[:, D_CONV + D_RG:D_CONV + D_RG + D_SSD],
             jnp.repeat(d_skip, SSD_HEAD_DIM).reshape(1, D_SSD), norm_w.reshape(1, D_SSD))
    ymix = pl.pallas_call(
        _mixer_step_b_kernel,
        grid=(1,),
        in_specs=[_whole(a.shape) for a in b_ins],
        out_specs=_whole((nb, D_MIX)),
        out_shape=jax.ShapeDtypeStruct((nb, D_MIX), BF16),
        compiler_params=_params("arbitrary"),
        name="mixer_step_b",
    )(*b_ins)
    return ymix, hnew, snew


def _moba_topk_kernel(q_ref, ksum_ref, idx_ref):
    nblk = ksum_ref.shape[0]
    qh = q_ref[...] * (1.0 / MOBA_BLOCK)
    lane = lax.broadcasted_iota(jnp.int32, (N_HEADS, LANES), 1)
    gate = jnp.full((N_HEADS, LANES), -jnp.inf, F32)
    for n in range(nblk):
        gate = jnp.where(lane == n, jnp.sum(ksum_ref[n] * qh, axis=-1, keepdims=True), gate)
    out = jnp.zeros((N_HEADS, LANES), jnp.int32)
    for kk in range(MOBA_TOPK):
        best = jnp.max(gate, axis=-1, keepdims=True)
        pick = jnp.min(jnp.where(gate == best, lane, LANES), axis=-1, keepdims=True)
        out = jnp.where(lane == kk, pick, out)
        gate = jnp.where(lane == pick, -jnp.inf, gate)
    idx_ref[...] = out


def moba_topk(q, ksum):
    nb, nblk = ksum.shape[:2]
    assert MOBA_TOPK <= nblk <= LANES
    return pl.pallas_call(
        _moba_topk_kernel,
        grid=(nb,),
        in_specs=[pl.BlockSpec((None, N_HEADS, HEAD_DIM), lambda bi: (bi, 0, 0)),
                  pl.BlockSpec((None, nblk, N_HEADS, HEAD_DIM), lambda bi: (bi, 0, 0, 0))],
        out_specs=pl.BlockSpec((None, N_HEADS, LANES), lambda bi: (bi, 0, 0)),
        out_shape=jax.ShapeDtypeStruct((nb, N_HEADS, LANES), jnp.int32),
        compiler_params=_params("parallel"),
        name="moba_topk",
    )(q, ksum)


def _moba_decode_kernel(pt_ref, top_ref, slopes_ref, q_ref, kn_ref, vn_ref, ck_hbm, cv_hbm, o_ref,
                        kbuf, vbuf, sem, *, past_len):
    bi = pl.program_id(0)
    scale = HEAD_DIM ** -0.5
    slots = MOBA_TOPK * PAGES_PER_BLOCK

    def head_copies(hd):
        copies = []
        for kk in range(MOBA_TOPK):
            blk_id = top_ref[bi, hd, kk]
            for half in range(PAGES_PER_BLOCK):
                page = pt_ref[bi, PAGES_PER_BLOCK * blk_id + half]
                slot = kk * PAGES_PER_BLOCK + half
                copies.append(pltpu.make_async_copy(ck_hbm.at[0, page, :, hd, :], kbuf.at[hd, slot],
                                                    sem.at[0, hd, slot]))
                copies.append(pltpu.make_async_copy(cv_hbm.at[0, page, :, hd, :], vbuf.at[hd, slot],
                                                    sem.at[1, hd, slot]))
        return copies

    all_copies = [head_copies(hd) for hd in range(N_HEADS)]
    for copies in all_copies:
        for cp in copies:
            cp.start()

    key_off = lax.broadcasted_iota(jnp.int32, (1, MOBA_BLOCK), 1)
    for hd in range(N_HEADS):
        for cp in all_copies[hd]:
            cp.wait()
        lo = hd * HEAD_DIM
        slope = slopes_ref[hd]
        q = q_ref[:, lo:lo + HEAD_DIM]
        q8 = jnp.broadcast_to(q, (SUBLANES, HEAD_DIM)).astype(BF16)
        kh = kbuf[hd].reshape(slots * PAGE_SIZE, HEAD_DIM).astype(BF16)
        vh = vbuf[hd].reshape(slots * PAGE_SIZE, HEAD_DIM).astype(BF16)
        dist = jnp.concatenate(
            [(past_len - top_ref[bi, hd, kk] * MOBA_BLOCK - key_off).astype(F32) for kk in range(MOBA_TOPK)],
            axis=1)
        s = lax.dot_general(q8, kh, NT_DIMS, preferred_element_type=F32)[0:1, :] * scale - slope * dist
        s_own = jnp.sum(q * kn_ref[:, lo:lo + HEAD_DIM], axis=-1, keepdims=True) * scale
        m = jnp.maximum(jnp.max(s, axis=-1, keepdims=True), s_own)
        p = jnp.exp(s - m)
        p_own = jnp.exp(s_own - m)
        denom = jnp.sum(p, axis=-1, keepdims=True) + p_own
        p8 = jnp.broadcast_to(p, (SUBLANES, slots * PAGE_SIZE)).astype(BF16)
        acc = jnp.dot(p8, vh, preferred_element_type=F32)[0:1, :] + p_own * vn_ref[:, lo:lo + HEAD_DIM]
        o_ref[:, lo:lo + HEAD_DIM] = acc / denom


def moba_decode(q, k_new, v_new, cache_k, cache_v, page_table, top_idx):
    nb, n_pages = page_table.shape
    slots = MOBA_TOPK * PAGES_PER_BLOCK
    row_spec = pl.BlockSpec((None, 1, D_MODEL), lambda bi, pt, top: (bi, 0, 0))
    return pl.pallas_call(
        functools.partial(_moba_decode_kernel, past_len=n_pages * PAGE_SIZE),
        grid_spec=pltpu.PrefetchScalarGridSpec(
            num_scalar_prefetch=2,
            grid=(nb,),
            in_specs=[pl.BlockSpec(memory_space=pltpu.SMEM), row_spec, row_spec, row_spec,
                      pl.BlockSpec(memory_space=pl.ANY), pl.BlockSpec(memory_space=pl.ANY)],
            out_specs=row_spec,
            scratch_shapes=[pltpu.VMEM((N_HEADS, slots, PAGE_SIZE, HEAD_DIM), F32),
                            pltpu.VMEM((N_HEADS, slots, PAGE_SIZE, HEAD_DIM), F32),
                            pltpu.SemaphoreType.DMA((2, N_HEADS, slots))]),
        out_shape=jax.ShapeDtypeStruct((nb, 1, D_MODEL), F32),
        compiler_params=_params("arbitrary"),
        name="moba_decode",
    )(page_table, top_idx, alibi_slopes(), q, k_new, v_new, cache_k, cache_v)


def _trunk_prompt(x, w, nb, cache_k, page_table):
    tm = 512
    n_seq = page_table.shape[0]
    half = n_seq // DEPTH
    conv_w, conv_b, *mixer_w = w["mixer"]
    proj_tiles = dict(tm=1024, tn=1024)
    cw, cb = conv_w, conv_b.reshape(1, D_CONV)
    u, tail_u = in_proj(x, w["w_in"], 0, D_RG, "conv", nb, cw, cb, **proj_tiles)
    xbc, tail_xbc = in_proj(x, w["w_in"], D_RG, D_XBC, "conv_silu", nb, cw, cb, **proj_tiles)
    gate = in_proj(x, w["w_in"], D_CONV, D_RG, "gelu", nb, **proj_tiles)
    z = in_proj(x, w["w_in"], D_CONV + D_RG, D_SSD, "silu", nb, **proj_tiles)
    conv_tail = jnp.concatenate([tail_u, tail_xbc], axis=2)[:, SUBLANES - (CONV_W - 1):, :]
    ymix, h_last, s_last = mixer_prompt(u, xbc, gate, z, x, nb, w["w_dt"], *mixer_w)
    x = matmul_ln(ymix, w["w_out"], x, w["ln1_g"][0], w["ln1_b"][0], tm=tm)
    x, ksum0 = mlp_ln_ksum(x, w["w_up"], w["w_down"], 0, w["ln2_g"][0], w["ln2_b"][0],
                           cache_k, page_table, 0, half, tm=tm, tf=1024)
    q, k, v = qkv_proj(x, w["w_qkv"], tm=tm)
    o = moba_prompt(q, k, v, nb)
    x = matmul_ln(o, w["w_o"], x, w["ln1_g"][1], w["ln1_b"][1], tm=tm)
    x, ksum1 = mlp_ln_ksum(x, w["w_up"], w["w_down"], 1, w["ln2_g"][1], w["ln2_b"][1],
                           cache_k, page_table, half, n_seq - half, tm=tm, tf=1024)
    return x, conv_tail, h_last, s_last, k, v, jnp.concatenate([ksum0, ksum1], axis=0)


def _trunk_sample(x, conv_buf, h0, s0, cache_k, cache_v, page_table, ksum, w):
    nb = x.shape[0]
    proj = matmul(x, w["w_in"], tm=nb, tn=1024, n_out=D_IN - SSD_HEADS)
    new_buf = jnp.concatenate([conv_buf[:, 1:], proj[:, None, :D_CONV]], axis=1)
    ymix, h_new, s_new = mixer_sample(proj, x, conv_buf, h0, s0, w["w_dt"], *w["mixer"])
    x = matmul_ln(ymix, w["w_out"], x, w["ln1_g"][0], w["ln1_b"][0], tm=nb)
    x = mlp_ln(x, w["w_up"], w["w_down"], 0, w["ln2_g"][0], w["ln2_b"][0], tm=nb, tf=1024)
    q, k, v = qkv_proj(x, w["w_qkv"], tm=nb)
    top_idx = moba_topk(q.reshape(nb, N_HEADS, HEAD_DIM), ksum)[:, :, :MOBA_TOPK]
    o = moba_decode(q.reshape(nb, 1, D_MODEL), k.reshape(nb, 1, D_MODEL), v.reshape(nb, 1, D_MODEL),
                    cache_k, cache_v, page_table, top_idx).reshape(nb, D_MODEL)
    x = matmul_ln(o, w["w_o"], x, w["ln1_g"][1], w["ln1_b"][1], tm=nb)
    x = mlp_ln(x, w["w_up"], w["w_down"], 1, w["ln2_g"][1], w["ln2_b"][1], tm=nb, tf=1024)
    return x, new_buf, h_new, s_new, k, v


def kernel(x_prompt, x_sample, state_conv, state_rglru, state_ssd, cache_k, cache_v, page_table, w_in, conv_w, conv_b, rg_wa, rg_ba, rg_wx, rg_bx, rg_lambda, ssd_dt_bias, ssd_a_log, ssd_d, ssd_norm_w, w_out, w_qkv, w_o, ln1_g, ln1_b, ln2_g, ln2_b, w_up, w_down):
    bp, lp, _ = x_prompt.shape
    bs = x_sample.shape[0]
    w = {
        "w_in": w_in[0],
        "w_dt": jnp.pad(w_in[0][:, D_IN - SSD_HEADS:], ((0, 0), (0, LANES - SSD_HEADS))).astype(BF16),
        "w_out": w_out[0].astype(BF16), "w_qkv": w_qkv[0].astype(BF16), "w_o": w_o[0].astype(BF16),
        "w_up": w_up.astype(BF16), "w_down": w_down.astype(BF16),
        "ln1_g": ln1_g, "ln1_b": ln1_b, "ln2_g": ln2_g, "ln2_b": ln2_b,
        "mixer": (conv_w[0], conv_b[0], rg_wa[0], rg_wx[0], rg_ba[0], rg_bx[0], rg_lambda[0],
                  ssd_dt_bias[0], ssd_a_log[0], ssd_d[0], ssd_norm_w[0]),
    }
    yp, conv_p, h_p, s_p, k_p, v_p, ksum = _trunk_prompt(x_prompt.reshape(bp * lp, D_MODEL), w, bp,
                                                         cache_k, page_table)
    ys, conv_s, h_s, s_s, k_s, v_s = _trunk_sample(
        x_sample.reshape(bs, D_MODEL), state_conv[0], state_rglru[0], state_ssd[0],
        cache_k, cache_v, page_table, ksum, w)
    return (yp.reshape(bp, lp, D_MODEL), ys.reshape(bs, 1, D_MODEL),
            conv_p[None], h_p[None], s_p.reshape(1, bp, SSD_HEADS, SSD_HEAD_DIM, SSD_STATE),
            k_p.reshape(1, bp, lp, N_HEADS, HEAD_DIM), v_p.reshape(1, bp, lp, N_HEADS, HEAD_DIM),
            conv_s[None], h_s[None], s_s[None],
            k_s.reshape(1, bs, 1, N_HEADS, HEAD_DIM), v_s.reshape(1, bs, 1, N_HEADS, HEAD_DIM))
```

```python
import functools
import math

import jax
import jax.numpy as jnp
from jax import lax
from jax.experimental import pallas as pl
from jax.experimental.pallas import tpu as pltpu

F32 = jnp.float32
BF16 = jnp.bfloat16

D_MODEL = 2048
DEPTH = 2
PAGE_SIZE = 128
CONV_W = 4
D_RG = D_MODEL
RG_HEADS = 8
RG_BW = D_RG // RG_HEADS
RG_C = 8.0
D_SSD = D_MODEL
SSD_HEAD_DIM = 64
SSD_HEADS = D_SSD // SSD_HEAD_DIM
SSD_GROUPS = 4
SSD_STATE = 128
SSD_CHUNK = 128
SSD_GN = SSD_GROUPS * SSD_STATE
D_XBC = D_SSD + 2 * SSD_GN
D_CONV = D_RG + D_XBC
D_IN = D_CONV + D_RG + D_SSD + SSD_HEADS
D_MIX = D_RG + D_SSD
N_HEADS = 16
HEAD_DIM = D_MODEL // N_HEADS
MOBA_BLOCK = 256
MOBA_TOPK = 3
PAGES_PER_BLOCK = MOBA_BLOCK // PAGE_SIZE
KSUM_BLOCKS_PER_STEP = 2
D_FF = 4 * D_MODEL
ALPHA = (2 * DEPTH) ** 0.25
LN_EPS = 1e-5
RMS_EPS = 1e-6

SUBLANES = 8
BF16_SUBLANES = 16
LANES = 128
LOG2_E = 1.4426950408889634
VMEM_LIMIT_BYTES = 56 * 1024 * 1024

ROW_TILE = 512
IN_PROJ_ROW_TILE = 1024
COL_TILE = 1024
HIDDEN_TILE = 1024

NT_DIMS = (((1,), (1,)), ((), ()))
TN_DIMS = (((0,), (0,)), ((), ()))


def _params(*semantics):
    return pltpu.CompilerParams(dimension_semantics=semantics, vmem_limit_bytes=VMEM_LIMIT_BYTES)


def _bdot(a, b):
    return jnp.dot(a.astype(BF16), b.astype(BF16), preferred_element_type=F32)


def _layer_norm_rows(y, g, b):
    mu = jnp.mean(y, axis=-1, keepdims=True)
    yc = y - mu
    var = jnp.mean(yc * yc, axis=-1, keepdims=True)
    return yc * lax.rsqrt(var + LN_EPS) * g + b


def _softplus(x):
    u = jnp.exp(-jnp.abs(x))
    w = 1.0 + u
    log1p_u = jnp.where(w == 1.0, u, jnp.log(w) * (u / (w - 1.0)))
    return jnp.maximum(x, 0.0) + log1p_u


def _sqrt_one_minus_sq(log_a, a):
    return jnp.sqrt(-jnp.tanh(log_a) * (a * a + 1.0))


def _silu(x):
    return x * jax.nn.sigmoid(x)


def _matmul_kernel(a_ref, w_ref, o_ref, ab_ref):
    @pl.when(pl.program_id(1) == 0)
    def _():
        ab_ref[...] = a_ref[...].astype(BF16)

    o_ref[...] = jnp.dot(ab_ref[...], w_ref[...], preferred_element_type=F32)


IN_PROJ_COL_CHUNK = 256
IN_PROJ_ROW_CHUNK = 64


def _in_proj_kernel(*refs, kind, blocks_per_seq):
    has_conv = kind in ("conv", "conv_silu")
    if has_conv:
        x_ref, w_ref, cw_ref, cb_ref, o_ref, tail_ref, xb_ref, ybuf, carry = refs
    else:
        x_ref, w_ref, o_ref, xb_ref = refs
    i = pl.program_id(0)
    j = pl.program_id(1)
    tm, tn = o_ref.shape
    front = SUBLANES

    @pl.when(j == 0)
    def _():
        xb_ref[...] = x_ref[...].astype(BF16)

    if has_conv:
        @pl.when(i % blocks_per_seq == 0)
        def _():
            ybuf[0:front, :] = jnp.zeros((front, tn), F32)

        @pl.when(i % blocks_per_seq != 0)
        def _():
            ybuf[0:front, :] = carry[j]

    for c0 in range(0, tn, IN_PROJ_COL_CHUNK):
        cols = slice(c0, c0 + IN_PROJ_COL_CHUNK)
        y = jnp.dot(xb_ref[...], w_ref[:, cols], preferred_element_type=F32)
        if has_conv:
            ybuf[front:front + tm, cols] = y
        for r0 in range(0, tm, IN_PROJ_ROW_CHUNK):
            rows = slice(r0, r0 + IN_PROJ_ROW_CHUNK)
            if has_conv:
                ext = ybuf[r0:r0 + front + IN_PROJ_ROW_CHUNK, cols]
                acc = cb_ref[:, cols] + cw_ref[CONV_W - 1:CONV_W, cols] * ext[front:, :]
                for k in range(CONV_W - 1):
                    shifted = pltpu.roll(ext, CONV_W - 1 - k, axis=0)[front:, :]
                    acc = acc + cw_ref[k:k + 1, cols] * shifted
                o_ref[rows, cols] = _silu(acc) if kind == "conv_silu" else acc
            elif kind == "gelu":
                o_ref[rows, cols] = jax.nn.gelu(y[rows, :])
            else:
                o_ref[rows, cols] = _silu(y[rows, :])

    if has_conv:
        last = ybuf[tm:tm + front, :]
        carry[j] = last
        tail_ref[...] = last


def in_proj(x, w, col0, n_out, kind, nb, conv_w=None, conv_b=None, *, tm, tn):
    m, k = x.shape
    off = col0 // tn
    grid = (m // tm, n_out // tn)
    blocks_per_seq = (m // nb) // tm
    has_conv = kind in ("conv", "conv_silu")
    in_specs = [pl.BlockSpec((tm, k), lambda i, j: (i, 0)),
                pl.BlockSpec((k, tn), lambda i, j: (0, j + off))]
    out_specs = [pl.BlockSpec((tm, tn), lambda i, j: (i, j))]
    out_shape = [jax.ShapeDtypeStruct((m, n_out), F32)]
    scratch = [pltpu.VMEM((tm, k), BF16)]
    operands = [x, w]
    if has_conv:
        in_specs += [pl.BlockSpec((CONV_W, tn), lambda i, j: (0, j + off)),
                     pl.BlockSpec((1, tn), lambda i, j: (0, j + off))]
        out_specs += [pl.BlockSpec((None, SUBLANES, tn), lambda i, j: (i, 0, j))]
        out_shape += [jax.ShapeDtypeStruct((grid[0], SUBLANES, n_out), F32)]
        scratch += [pltpu.VMEM((tm + SUBLANES, tn), F32), pltpu.VMEM((grid[1], SUBLANES, tn), F32)]
        operands += [conv_w, conv_b]
    outs = pl.pallas_call(
        functools.partial(_in_proj_kernel, kind=kind, blocks_per_seq=blocks_per_seq),
        grid=grid,
        in_specs=in_specs,
        out_specs=out_specs,
        out_shape=out_shape,
        scratch_shapes=scratch,
        compiler_params=_params("arbitrary", "arbitrary"),
        name="in_proj_" + kind,
    )(*operands)
    if not has_conv:
        return outs[0]
    return outs[0], outs[1][blocks_per_seq - 1::blocks_per_seq]


def matmul(a, w, *, tm, tn):
    m, k = a.shape
    n = w.shape[1]
    return pl.pallas_call(
        _matmul_kernel,
        grid=(pl.cdiv(m, tm), pl.cdiv(n, tn)),
        in_specs=[pl.BlockSpec((tm, k), lambda i, j: (i, 0)),
                  pl.BlockSpec((k, tn), lambda i, j: (0, j))],
        out_specs=pl.BlockSpec((tm, tn), lambda i, j: (i, j)),
        out_shape=jax.ShapeDtypeStruct((m, n), F32),
        scratch_shapes=[pltpu.VMEM((tm, k), BF16)],
        compiler_params=_params("parallel", "arbitrary"),
        name="matmul",
    )(a, w)


def _resident_w_kernel(a_ref, w_ref, o_ref):
    o_ref[...] = _bdot(a_ref[...], w_ref[...])


def qkv_proj(a, w, *, tm):
    m, k = a.shape
    n = w.shape[1] // 3

    def part(idx):
        return pl.pallas_call(
            _resident_w_kernel,
            grid=(pl.cdiv(m, tm),),
            in_specs=[pl.BlockSpec((tm, k), lambda i: (i, 0)),
                      pl.BlockSpec((k, n), lambda i: (0, idx), pipeline_mode=pl.Buffered(1))],
            out_specs=pl.BlockSpec((tm, n), lambda i: (i, 0)),
            out_shape=jax.ShapeDtypeStruct((m, n), F32),
            compiler_params=_params("parallel"),
            name="qkv_proj",
        )(a, w)

    return part(0), part(1), part(2)


LN_ROW_CHUNK = 256


def _matmul_ln_kernel(a_ref, w_ref, res_ref, g_ref, b_ref, o_ref):
    tm = o_ref.shape[0]
    chunk = min(tm, LN_ROW_CHUNK)
    for r0 in range(0, tm, chunk):
        rows = slice(r0, r0 + chunk)
        y = ALPHA * res_ref[rows, :] + _bdot(a_ref[rows, :], w_ref[...])
        o_ref[rows, :] = _layer_norm_rows(y, g_ref[...], b_ref[...])


def matmul_ln(a, w, res, g, b, *, tm):
    m, k = a.shape
    n = w.shape[1]
    return pl.pallas_call(
        _matmul_ln_kernel,
        grid=(pl.cdiv(m, tm),),
        in_specs=[pl.BlockSpec((tm, k), lambda i: (i, 0)),
                  pl.BlockSpec((k, n), lambda i: (0, 0), pipeline_mode=pl.Buffered(1)),
                  pl.BlockSpec((tm, n), lambda i: (i, 0)),
                  pl.BlockSpec((1, n), lambda i: (0, 0)),
                  pl.BlockSpec((1, n), lambda i: (0, 0))],
        out_specs=pl.BlockSpec((tm, n), lambda i: (i, 0)),
        out_shape=jax.ShapeDtypeStruct((m, n), F32),
        compiler_params=_params("parallel"),
        name="matmul_ln",
    )(a, w, res, g.reshape(1, n), b.reshape(1, n))


def _mlp_ln_body(x_ref, wu_ref, wd_ref, g_ref, b_ref, o_ref, xb_ref, shadow_work=None):
    f = pl.program_id(1)

    @pl.when(f == 0)
    def _():
        xb_ref[...] = x_ref[...].astype(BF16)
        o_ref[...] = jnp.zeros_like(o_ref)

    if shadow_work is not None:
        shadow_work()
    hdn = jnp.maximum(jnp.dot(xb_ref[...], wu_ref[...], preferred_element_type=F32), 0.0)
    o_ref[...] += _bdot(hdn * hdn, wd_ref[...])

    @pl.when(f == pl.num_programs(1) - 1)
    def _():
        y = ALPHA * x_ref[...] + o_ref[...]
        o_ref[...] = _layer_norm_rows(y, g_ref[...], b_ref[...])


def _mlp_ln_kernel(x_ref, wu_ref, wd_ref, g_ref, b_ref, o_ref, xb_ref):
    _mlp_ln_body(x_ref, wu_ref, wd_ref, g_ref, b_ref, o_ref, xb_ref)


def _mlp_ln_ksum_kernel(pt_ref, x_ref, wu_ref, wd_ref, g_ref, b_ref, ck_hbm, o_ref, ksum_ref,
                        xb_ref, kbuf, sem, *, seq0, nblk):
    step = pl.program_id(0) * pl.num_programs(1) + pl.program_id(1)
    n_steps = pl.num_programs(0) * pl.num_programs(1)
    pages_per_step = KSUM_BLOCKS_PER_STEP * PAGES_PER_BLOCK
    slot = step % 2

    def page_copies(at_step, at_slot):
        copies = []
        for pg in range(pages_per_step):
            flat = at_step * KSUM_BLOCKS_PER_STEP + pg // PAGES_PER_BLOCK
            page = pt_ref[seq0 + flat // nblk, (flat % nblk) * PAGES_PER_BLOCK + pg % PAGES_PER_BLOCK]
            copies.append(pltpu.make_async_copy(ck_hbm.at[0, page], kbuf.at[at_slot, pg], sem.at[at_slot, pg]))
        return copies

    @pl.when(step == 0)
    def _():
        for cp in page_copies(0, 0):
            cp.start()

    @pl.when(step + 1 < n_steps)
    def _():
        for cp in page_copies(step + 1, 1 - slot):
            cp.start()

    def sum_pages():
        for cp in page_copies(step, slot):
            cp.wait()
        for j in range(KSUM_BLOCKS_PER_STEP):
            acc = jnp.sum(kbuf[slot, j * PAGES_PER_BLOCK], axis=0)
            for p in range(1, PAGES_PER_BLOCK):
                acc = acc + jnp.sum(kbuf[slot, j * PAGES_PER_BLOCK + p], axis=0)
            ksum_ref[j] = acc

    _mlp_ln_body(x_ref, wu_ref, wd_ref, g_ref, b_ref, o_ref, xb_ref, sum_pages)


def mlp_ln(x, w_up, w_down, layer, g, b, *, tm, tf):
    m, d = x.shape
    ff = w_up.shape[2]
    return pl.pallas_call(
        _mlp_ln_kernel,
        grid=(pl.cdiv(m, tm), ff // tf),
        in_specs=[pl.BlockSpec((tm, d), lambda i, f: (i, 0)),
                  pl.BlockSpec((None, d, tf), lambda i, f: (layer, 0, f)),
                  pl.BlockSpec((None, tf, d), lambda i, f: (layer, f, 0)),
                  pl.BlockSpec((1, d), lambda i, f: (0, 0)),
                  pl.BlockSpec((1, d), lambda i, f: (0, 0))],
        out_specs=pl.BlockSpec((tm, d), lambda i, f: (i, 0)),
        out_shape=jax.ShapeDtypeStruct((m, d), F32),
        scratch_shapes=[pltpu.VMEM((tm, d), BF16)],
        compiler_params=_params("parallel", "arbitrary"),
        name="mlp_ln",
    )(x, w_up, w_down, g.reshape(1, d), b.reshape(1, d))


def mlp_ln_ksum(x, w_up, w_down, layer, g, b, cache_k, page_table, seq0, n_seq, *, tm, tf):
    m, d = x.shape
    ff = w_up.shape[2]
    grid = (m // tm, ff // tf)
    nblk = page_table.shape[1] // PAGES_PER_BLOCK
    assert n_seq * nblk == grid[0] * grid[1] * KSUM_BLOCKS_PER_STEP and nblk % KSUM_BLOCKS_PER_STEP == 0
    steps_per_seq = nblk // KSUM_BLOCKS_PER_STEP

    def ksum_index(i, f, pt):
        step = i * grid[1] + f
        return (step // steps_per_seq, step % steps_per_seq, 0, 0)

    return pl.pallas_call(
        functools.partial(_mlp_ln_ksum_kernel, seq0=seq0, nblk=nblk),
        grid_spec=pltpu.PrefetchScalarGridSpec(
            num_scalar_prefetch=1,
            grid=grid,
            in_specs=[pl.BlockSpec((tm, d), lambda i, f, pt: (i, 0)),
                      pl.BlockSpec((None, d, tf), lambda i, f, pt: (layer, 0, f)),
                      pl.BlockSpec((None, tf, d), lambda i, f, pt: (layer, f, 0)),
                      pl.BlockSpec((1, d), lambda i, f, pt: (0, 0)),
                      pl.BlockSpec((1, d), lambda i, f, pt: (0, 0)),
                      pl.BlockSpec(memory_space=pl.ANY)],
            out_specs=[pl.BlockSpec((tm, d), lambda i, f, pt: (i, 0)),
                       pl.BlockSpec((None, KSUM_BLOCKS_PER_STEP, N_HEADS, HEAD_DIM), ksum_index)],
            scratch_shapes=[pltpu.VMEM((tm, d), BF16),
                            pltpu.VMEM((2, KSUM_BLOCKS_PER_STEP * PAGES_PER_BLOCK, PAGE_SIZE, N_HEADS, HEAD_DIM), F32),
                            pltpu.SemaphoreType.DMA((2, KSUM_BLOCKS_PER_STEP * PAGES_PER_BLOCK))]),
        out_shape=[jax.ShapeDtypeStruct((m, d), F32),
                   jax.ShapeDtypeStruct((n_seq, nblk, N_HEADS, HEAD_DIM), F32)],
        compiler_params=_params("arbitrary", "arbitrary"),
        name="mlp_ln_ksum",
    )(page_table, x, w_up, w_down, g.reshape(1, d), b.reshape(1, d), cache_k)


def _mixer_prompt_kernel(u_ref, xbc_ref, gate_ref, z_ref, x_ref,
                         wdt_ref, wa_ref, wx_ref, ba_ref, bx_ref, lam_ref,
                         dtb_ref, alog_ref, dskip_ref, normw_ref,
                         ymix_ref, hlast_ref, state_ref,
                         a_scr, h_scr, y_scr, decay_rate_scr):
    t = SSD_CHUNK
    c = pl.program_id(1)

    @pl.when(c == 0)
    def _():
        hlast_ref[...] = jnp.zeros_like(hlast_ref)
        state_ref[...] = jnp.zeros_like(state_ref)
        decay_rate_scr[...] = -RG_C * _softplus(-lam_ref[...])

    row = lax.broadcasted_iota(jnp.int32, (t, RG_BW), 0)
    row_in_tile = row % SUBLANES
    is_start = jnp.logical_and(c == 0, row == 0)
    for hd in range(RG_HEADS):
        lo = hd * RG_BW
        u = u_ref[:, lo:lo + RG_BW]
        ub = u.astype(BF16)
        r = jax.nn.sigmoid(jnp.dot(ub, wa_ref[hd], preferred_element_type=F32) + ba_ref[:, lo:lo + RG_BW])
        i = jax.nn.sigmoid(jnp.dot(ub, wx_ref[hd], preferred_element_type=F32) + bx_ref[:, lo:lo + RG_BW])
        log_a = decay_rate_scr[:, lo:lo + RG_BW] * r
        a_raw = jnp.exp(log_a)
        a = jnp.where(is_start, 0.0, a_raw)
        mult = jnp.where(is_start, 1.0, _sqrt_one_minus_sq(log_a, a_raw))
        bt = mult * i * u
        for s in (1, 2, 4):
            keep = row_in_tile >= s
            a_sh = pltpu.roll(a, s, axis=0)
            b_sh = pltpu.roll(bt, s, axis=0)
            bt = jnp.where(keep, a * b_sh + bt, bt)
            a = jnp.where(keep, a * a_sh, a)
        a_scr[:, lo:lo + RG_BW] = a
        h_scr[:, lo:lo + RG_BW] = bt

    def carry_tile(gi, carry):
        r0 = pl.multiple_of(gi * SUBLANES, SUBLANES)
        h = a_scr[pl.ds(r0, SUBLANES), :] * carry + h_scr[pl.ds(r0, SUBLANES), :]
        h_scr[pl.ds(r0, SUBLANES), :] = h
        return h[SUBLANES - 1:SUBLANES, :]

    hlast_ref[...] = lax.fori_loop(0, t // SUBLANES, carry_tile, hlast_ref[...])
    ymix_ref[:, 0:D_RG] = (h_scr[...] * gate_ref[...]).astype(BF16)

    lane = lax.broadcasted_iota(jnp.int32, (t, LANES), 1)
    rows = lax.broadcasted_iota(jnp.int32, (t, LANES), 0)
    dt_raw = jnp.dot(x_ref[...].astype(BF16), wdt_ref[...], preferred_element_type=F32)
    dt = jnp.where(lane < SSD_HEADS, _softplus(dt_raw + dtb_ref[...]), 0.0)
    cs = dt * (-jnp.exp(alog_ref[...]))
    s = 1
    while s < t:
        cs = cs + jnp.where(rows >= s, pltpu.roll(cs, s, axis=0), 0.0)
        s *= 2
    cs_t = cs.T
    cs_last = cs[t - 1:t, :]
    decay_to_end = jnp.exp(cs_last - cs)
    exp_cs = jnp.exp(cs)
    chunk_decay = jnp.exp(cs_last)

    ii = lax.broadcasted_iota(jnp.int32, (t, t), 0)
    jj = lax.broadcasted_iota(jnp.int32, (t, t), 1)
    causal = jj <= ii
    first_head = lane < SSD_HEAD_DIM
    first_head_rows = rows < SSD_HEAD_DIM

    for g in range(SSD_GROUPS):
        b_lo = D_SSD + g * SSD_STATE
        bmat = xbc_ref[:, b_lo:b_lo + SSD_STATE].astype(BF16)
        cmat = xbc_ref[:, b_lo + SSD_GN:b_lo + SSD_GN + SSD_STATE].astype(BF16)
        cb = lax.dot_general(cmat, bmat, NT_DIMS, preferred_element_type=F32)
        for pair in range(SSD_HEADS // SSD_GROUPS // 2):
            h0 = g * (SSD_HEADS // SSD_GROUPS) + 2 * pair
            lo = h0 * SSD_HEAD_DIM

            def per_head(x2d):
                return jnp.where(first_head, x2d[:, h0:h0 + 1], x2d[:, h0 + 1:h0 + 2])

            xs = xbc_ref[:, lo:lo + LANES]
            xdt = xs * per_head(dt)
            xdt_b = xdt.astype(BF16)
            y_heads = []
            for hh in (h0, h0 + 1):
                seg = cs[:, hh:hh + 1] - cs_t[hh:hh + 1, :]
                lmat = jnp.exp(jnp.where(causal, seg, -jnp.inf))
                y_heads.append(jnp.dot((cb * lmat).astype(BF16), xdt_b, preferred_element_type=F32))
            y = jnp.where(first_head, y_heads[0], y_heads[1])
            s_prev = state_ref[lo:lo + LANES, :]
            y = y + lax.dot_general(cmat, s_prev.astype(BF16), NT_DIMS,
                                    preferred_element_type=F32) * per_head(exp_cs)
            y_scr[:, lo:lo + LANES] = y + dskip_ref[:, lo:lo + LANES] * xs
            decay_rows = jnp.where(first_head_rows, chunk_decay[:, h0:h0 + 1], chunk_decay[:, h0 + 1:h0 + 2])
            state_ref[lo:lo + LANES, :] = s_prev * decay_rows + lax.dot_general(
                (xdt * per_head(decay_to_end)).astype(BF16), bmat, TN_DIMS, preferred_element_type=F32)

    gw = D_SSD // SSD_GROUPS
    for g in range(SSD_GROUPS):
        lo = g * gw
        yz = y_scr[:, lo:lo + gw] * z_ref[:, lo:lo + gw]
        ms = jnp.mean(yz * yz, axis=-1, keepdims=True)
        ymix_ref[:, D_RG + lo:D_RG + lo + gw] = (yz * lax.rsqrt(ms + RMS_EPS)
                                                 * normw_ref[:, lo:lo + gw]).astype(BF16)


def _pad_lanes(v):
    return jnp.zeros((1, LANES), F32).at[0, :v.shape[0]].set(v)


def mixer_prompt(u, xbc, gate, z, x, nb, w_dt, wa, wx, ba, bx, lam, dt_bias, a_log, d_skip, norm_w):
    m = u.shape[0]
    l = m // nb
    t = SSD_CHUNK
    nc = l // t
    row_blk = lambda bi, ci: bi * nc + ci
    rows = lambda width: pl.BlockSpec((t, width), lambda bi, ci: (row_blk(bi, ci), 0))
    full = lambda shape: pl.BlockSpec(shape, lambda bi, ci: (0,) * len(shape))
    in_specs = [rows(D_RG), rows(D_XBC), rows(D_RG), rows(D_SSD), rows(D_MODEL)]
    in_specs += [full((D_MODEL, LANES)),
                 full((RG_HEADS, RG_BW, RG_BW)), full((RG_HEADS, RG_BW, RG_BW)),
                 full((1, D_RG)), full((1, D_RG)), full((1, D_RG)),
                 full((1, LANES)), full((1, LANES)), full((1, D_SSD)), full((1, D_SSD))]
    ymix, hlast, state = pl.pallas_call(
        _mixer_prompt_kernel,
        grid=(nb, nc),
        in_specs=in_specs,
        out_specs=[pl.BlockSpec((t, D_MIX), lambda bi, ci: (row_blk(bi, ci), 0)),
                   pl.BlockSpec((None, 1, D_RG), lambda bi, ci: (bi, 0, 0)),
                   pl.BlockSpec((None, D_SSD, SSD_STATE), lambda bi, ci: (bi, 0, 0))],
        out_shape=[jax.ShapeDtypeStruct((m, D_MIX), BF16),
                   jax.ShapeDtypeStruct((nb, 1, D_RG), F32),
                   jax.ShapeDtypeStruct((nb, D_SSD, SSD_STATE), F32)],
        scratch_shapes=[pltpu.VMEM((t, D_RG), F32), pltpu.VMEM((t, D_RG), F32),
                        pltpu.VMEM((t, D_SSD), F32), pltpu.VMEM((1, D_RG), F32)],
        compiler_params=_params("parallel", "arbitrary"),
        name="mixer_prompt",
    )(u, xbc, gate, z, x, w_dt, wa.astype(BF16), wx.astype(BF16),
      ba.reshape(1, D_RG), bx.reshape(1, D_RG), lam.reshape(1, D_RG),
      _pad_lanes(dt_bias), _pad_lanes(a_log),
      jnp.repeat(d_skip, SSD_HEAD_DIM).reshape(1, D_SSD), norm_w.reshape(1, D_SSD))
    return ymix, hlast.reshape(nb, D_RG), state


def _moba_prompt_kernel(slopes_ref, q_ref, k_ref, v_ref, o_ref, kb_ref, vt_ref, pen_ref):
    blk = MOBA_BLOCK
    seq = k_ref.shape[0]
    nblk = seq // blk
    slope = slopes_ref[pl.program_id(1)] * LOG2_E
    scale = HEAD_DIM ** -0.5 * LOG2_E

    kmeans = []
    for n in range(nblk):
        rows = slice(n * blk, (n + 1) * blk)
        kn = k_ref[rows, :]
        kmeans.append(jnp.mean(kn, axis=0, keepdims=True))
        kb_ref[rows, :] = kn.astype(BF16)
        vt_ref[0:HEAD_DIM, rows] = v_ref[rows, :].T.astype(BF16)
    vt_ref[HEAD_DIM:, :] = jnp.ones((vt_ref.shape[0] - HEAD_DIM, seq), BF16)
    kmean = jnp.concatenate(kmeans, axis=0)
    pen_ref[...] = slope * (lax.broadcasted_iota(jnp.int32, (seq, blk), 1)
                            - lax.broadcasted_iota(jnp.int32, (seq, blk), 0)).astype(F32)
    blk_id = lax.broadcasted_iota(jnp.int32, (nblk, blk), 0)
    causal = (lax.broadcasted_iota(jnp.int32, (blk, blk), 0) <= lax.broadcasted_iota(jnp.int32, (blk, blk), 1))

    for qi in range(nblk):
        q = q_ref[qi * blk:(qi + 1) * blk, :]
        qb = q.astype(BF16)
        base = -slope * float(qi * blk)
        if qi > MOBA_TOPK:
            gate = lax.dot_general(kmean, q, NT_DIMS, precision=lax.Precision.HIGHEST,
                                   preferred_element_type=F32)
            gate = jnp.where(blk_id < qi, gate, -jnp.inf)
            ahead = jnp.zeros((nblk, blk), jnp.int32)
            for n in range(qi):
                gate_n = gate[n:n + 1, :]
                before = jnp.logical_or(gate_n > gate, jnp.logical_and(gate_n == gate, n < blk_id))
                ahead = ahead + before.astype(jnp.int32)
            row_terms = jnp.where(ahead < MOBA_TOPK, base, -jnp.inf)
        else:
            row_terms = jnp.full((nblk, blk), base, F32)

        s_blocks = []
        for n in range(qi + 1):
            rows = slice(n * blk, (n + 1) * blk)
            s = lax.dot_general(kb_ref[rows, :], qb, NT_DIMS, preferred_element_type=F32)
            s = s * scale - pen_ref[rows, :]
            if n < qi:
                s = s + row_terms[n:n + 1, :]
            else:
                s = jnp.where(causal, s + base, -jnp.inf)
            s_blocks.append(s)
        m = jnp.max(s_blocks[0], axis=0, keepdims=True)
        for s in s_blocks[1:]:
            m = jnp.maximum(m, jnp.max(s, axis=0, keepdims=True))
        acc = jnp.zeros((vt_ref.shape[0], blk), F32)
        for n, s in enumerate(s_blocks):
            p = jnp.exp2(s - m)
            acc = acc + jnp.dot(vt_ref[:, n * blk:(n + 1) * blk], p.astype(BF16), preferred_element_type=F32)
        denom = acc[HEAD_DIM:HEAD_DIM + 1, :]
        o_ref[qi * blk:(qi + 1) * blk, :] = (acc[0:HEAD_DIM, :] / denom).T.astype(o_ref.dtype)


def alibi_slopes():
    return jnp.exp2(-8.0 * jnp.arange(1, N_HEADS + 1, dtype=F32) / N_HEADS)


def moba_prompt(q, k, v, nb):
    m = q.shape[0]
    l = m // nb
    head_spec = pl.BlockSpec((l, HEAD_DIM), lambda bi, hi: (bi, hi))
    return pl.pallas_call(
        _moba_prompt_kernel,
        grid=(nb, N_HEADS),
        in_specs=[pl.BlockSpec(memory_space=pltpu.SMEM), head_spec, head_spec, head_spec],
        out_specs=head_spec,
        out_shape=jax.ShapeDtypeStruct((m, D_MODEL), BF16),
        scratch_shapes=[pltpu.VMEM((l, HEAD_DIM), BF16), pltpu.VMEM((HEAD_DIM + BF16_SUBLANES, l), BF16),
                        pltpu.VMEM((l, MOBA_BLOCK), F32)],
        compiler_params=_params("parallel", "parallel"),
        name="moba_prompt",
    )(alibi_slopes(), q, k, v)


def _mixer_step_a_kernel(proj_ref, x_ref, b0_ref, b1_ref, b2_ref, h0_ref,
                         wdt_ref, convw_ref, convb_ref, wa_ref, wx_ref, ba_ref, bx_ref, lam_ref, dtb_ref,
                         yrg_ref, hnew_ref, xbc_ref, dt_ref):
    conv = (convb_ref[...] + convw_ref[0:1, :] * b0_ref[...] + convw_ref[1:2, :] * b1_ref[...]
            + convw_ref[2:3, :] * b2_ref[...] + convw_ref[3:4, :] * proj_ref[:, 0:D_CONV])
    xbc_ref[...] = _silu(conv[:, D_RG:])
    dt_raw = jnp.dot(x_ref[...].astype(BF16), wdt_ref[...], preferred_element_type=F32)
    dt_ref[...] = _softplus(dt_raw[:, 0:SSD_HEADS] + dtb_ref[...])
    for hd in range(RG_HEADS):
        lo = hd * RG_BW
        u = conv[:, lo:lo + RG_BW]
        ub = u.astype(BF16)
        r = jax.nn.sigmoid(jnp.dot(ub, wa_ref[hd], preferred_element_type=F32) + ba_ref[:, lo:lo + RG_BW])
        i = jax.nn.sigmoid(jnp.dot(ub, wx_ref[hd], preferred_element_type=F32) + bx_ref[:, lo:lo + RG_BW])
        log_a = -RG_C * r * _softplus(-lam_ref[:, lo:lo + RG_BW])
        a = jnp.exp(log_a)
        h = a * h0_ref[:, lo:lo + RG_BW] + _sqrt_one_minus_sq(log_a, a) * i * u
        hnew_ref[:, lo:lo + RG_BW] = h
        gate = proj_ref[:, D_CONV + lo:D_CONV + lo + RG_BW]
        yrg_ref[:, lo:lo + RG_BW] = h * jax.nn.gelu(gate)


def _ssd_step_kernel(s0_ref, xs_ref, dt_ref, alog_ref, b_ref, c_ref, snew_ref, y_ref):
    dt = dt_ref[...]
    decay = jnp.exp(dt * (-jnp.exp(alog_ref[...])))
    s_new = s0_ref[...] * decay + (xs_ref[...] * dt) * b_ref[...]
    snew_ref[...] = s_new
    y_ref[...] = jnp.sum(s_new * c_ref[...], axis=-1, keepdims=True)


def _mixer_step_b_kernel(yrg_ref, y_ref, xs_ref, z_ref, dskip_ref, normw_ref, o_ref):
    o_ref[:, 0:D_RG] = yrg_ref[...].astype(BF16)
    yz = (y_ref[...] + dskip_ref[...] * xs_ref[...]) * _silu(z_ref[...])
    gw = D_SSD // SSD_GROUPS
    for g in range(SSD_GROUPS):
        blk = yz[:, g * gw:(g + 1) * gw]
        ms = jnp.mean(blk * blk, axis=-1, keepdims=True)
        o_ref[:, D_RG + g * gw:D_RG + (g + 1) * gw] = (
            blk * lax.rsqrt(ms + RMS_EPS) * normw_ref[:, g * gw:(g + 1) * gw]).astype(BF16)


def _whole(shape):
    return pl.BlockSpec(shape, lambda *_: (0,) * len(shape))


def mixer_sample(proj, x, conv_buf, h0, s0, w_dt, conv_w, conv_b, wa, wx, ba, bx, lam, dt_bias, a_log, d_skip,
                 norm_w):
    nb = proj.shape[0]
    hpg = SSD_HEADS // SSD_GROUPS
    a_ins = (proj, x, conv_buf[:, 0], conv_buf[:, 1], conv_buf[:, 2], h0,
             w_dt, conv_w, conv_b.reshape(1, D_CONV), wa.astype(BF16), wx.astype(BF16),
             ba.reshape(1, D_RG), bx.reshape(1, D_RG), lam.reshape(1, D_RG), dt_bias.reshape(1, SSD_HEADS))
    yrg, hnew, xbc, dt = pl.pallas_call(
        _mixer_step_a_kernel,
        grid=(1,),
        in_specs=[_whole(a.shape) for a in a_ins],
        out_specs=[_whole((nb, D_RG)), _whole((nb, D_RG)), _whole((nb, D_XBC)), _whole((nb, SSD_HEADS))],
        out_shape=[jax.ShapeDtypeStruct((nb, D_RG), F32), jax.ShapeDtypeStruct((nb, D_RG), F32),
                   jax.ShapeDtypeStruct((nb, D_XBC), F32), jax.ShapeDtypeStruct((nb, SSD_HEADS), F32)],
        compiler_params=_params("arbitrary"),
        name="mixer_step_a",
    )(*a_ins)
    xs = xbc[:, :D_SSD]
    snew, y = pl.pallas_call(
        _ssd_step_kernel,
        grid=(nb, SSD_GROUPS),
        in_specs=[pl.BlockSpec((None, hpg, SSD_HEAD_DIM, SSD_STATE), lambda bi, gi: (bi, gi, 0, 0)),
                  pl.BlockSpec((None, hpg, SSD_HEAD_DIM, 1), lambda bi, gi: (bi, gi, 0, 0)),
                  pl.BlockSpec((None, hpg, 1, 1), lambda bi, gi: (bi, gi, 0, 0)),
                  pl.BlockSpec((hpg, 1, 1), lambda bi, gi: (gi, 0, 0)),
                  pl.BlockSpec((None, None, 1, SSD_STATE), lambda bi, gi: (bi, gi, 0, 0)),
                  pl.BlockSpec((None, None, 1, SSD_STATE), lambda bi, gi: (bi, gi, 0, 0))],
        out_specs=[pl.BlockSpec((None, hpg, SSD_HEAD_DIM, SSD_STATE), lambda bi, gi: (bi, gi, 0, 0)),
                   pl.BlockSpec((None, hpg, SSD_HEAD_DIM, 1), lambda bi, gi: (bi, gi, 0, 0))],
        out_shape=[jax.ShapeDtypeStruct((nb, SSD_HEADS, SSD_HEAD_DIM, SSD_STATE), F32),
                   jax.ShapeDtypeStruct((nb, SSD_HEADS, SSD_HEAD_DIM, 1), F32)],
        compiler_params=_params("parallel", "parallel"),
        name="ssd_step",
    )(s0, xs.reshape(nb, SSD_HEADS, SSD_HEAD_DIM, 1), dt.reshape(nb, SSD_HEADS, 1, 1),
      a_log.reshape(SSD_HEADS, 1, 1),
      xbc[:, D_SSD:D_SSD + SSD_GN].reshape(nb, SSD_GROUPS, 1, SSD_STATE),
      xbc[:, D_SSD + SSD_GN:].reshape(nb, SSD_GROUPS, 1, SSD_STATE))
    b_ins = (yrg, y.reshape(nb, D_SSD), xs, proj[:, D_CONV + D_RG:D_CONV + D_RG + D_SSD],
             jnp.repeat(d_skip, SSD_HEAD_DIM).reshape(1, D_SSD), norm_w.reshape(1, D_SSD))
    ymix = pl.pallas_call(
        _mixer_step_b_kernel,
        grid=(1,),
        in_specs=[_whole(a.shape) for a in b_ins],
        out_specs=_whole((nb, D_MIX)),
        out_shape=jax.ShapeDtypeStruct((nb, D_MIX), BF16),
        compiler_params=_params("arbitrary"),
        name="mixer_step_b",
    )(*b_ins)
    return ymix, hnew, snew


def _moba_topk_kernel(q_ref, ksum_ref, idx_ref):
    nblk = ksum_ref.shape[0]
    qh = q_ref[...] * (1.0 / MOBA_BLOCK)
    lane = lax.broadcasted_iota(jnp.int32, (N_HEADS, LANES), 1)
    gate = jnp.full((N_HEADS, LANES), -jnp.inf, F32)
    for n in range(nblk):
        gate = jnp.where(lane == n, jnp.sum(ksum_ref[n] * qh, axis=-1, keepdims=True), gate)
    out = jnp.zeros((N_HEADS, LANES), jnp.int32)
    for kk in range(MOBA_TOPK):
        best = jnp.max(gate, axis=-1, keepdims=True)
        pick = jnp.min(jnp.where(gate == best, lane, LANES), axis=-1, keepdims=True)
        out = jnp.where(lane == kk, pick, out)
        gate = jnp.where(lane == pick, -jnp.inf, gate)
    idx_ref[...] = out


def moba_topk(q, ksum):
    nb, nblk = ksum.shape[:2]
    assert MOBA_TOPK <= nblk <= LANES
    return pl.pallas_call(
        _moba_topk_kernel,
        grid=(nb,),
        in_specs=[pl.BlockSpec((None, N_HEADS, HEAD_DIM), lambda bi: (bi, 0, 0)),
                  pl.BlockSpec((None, nblk, N_HEADS, HEAD_DIM), lambda bi: (bi, 0, 0, 0))],
        out_specs=pl.BlockSpec((None, N_HEADS, LANES), lambda bi: (bi, 0, 0)),
        out_shape=jax.ShapeDtypeStruct((nb, N_HEADS, LANES), jnp.int32),
        compiler_params=_params("parallel"),
        name="moba_topk",
    )(q, ksum)


def _moba_decode_kernel(pt_ref, top_ref, slopes_ref, q_ref, kn_ref, vn_ref, ck_hbm, cv_hbm, o_ref,
                        kbuf, vbuf, sem, *, past_len):
    bi = pl.program_id(0)
    n_seq = pl.num_programs(0)
    buf = bi % 2
    scale = HEAD_DIM ** -0.5
    slots = MOBA_TOPK * PAGES_PER_BLOCK

    def head_copies(seq, at_buf, hd):
        copies = []
        for kk in range(MOBA_TOPK):
            blk_id = top_ref[seq, hd, kk]
            for half in range(PAGES_PER_BLOCK):
                page = pt_ref[seq, PAGES_PER_BLOCK * blk_id + half]
                slot = kk * PAGES_PER_BLOCK + half
                copies.append(pltpu.make_async_copy(ck_hbm.at[0, page, :, hd, :], kbuf.at[at_buf, hd, slot],
                                                    sem.at[at_buf, 0, hd, slot]))
                copies.append(pltpu.make_async_copy(cv_hbm.at[0, page, :, hd, :], vbuf.at[at_buf, hd, slot],
                                                    sem.at[at_buf, 1, hd, slot]))
        return copies

    @pl.when(bi == 0)
    def _():
        for hd in range(N_HEADS):
            for cp in head_copies(0, 0, hd):
                cp.start()

    @pl.when(bi + 1 < n_seq)
    def _():
        for hd in range(N_HEADS):
            for cp in head_copies(bi + 1, 1 - buf, hd):
                cp.start()

    key_off = lax.broadcasted_iota(jnp.int32, (1, MOBA_BLOCK), 1)
    for hd in range(N_HEADS):
        for cp in head_copies(bi, buf, hd):
            cp.wait()
        lo = hd * HEAD_DIM
        slope = slopes_ref[hd]
        q = q_ref[:, lo:lo + HEAD_DIM]
        q8 = jnp.broadcast_to(q, (SUBLANES, HEAD_DIM)).astype(BF16)
        kh = kbuf[buf, hd].reshape(slots * PAGE_SIZE, HEAD_DIM).astype(BF16)
        vh = vbuf[buf, hd].reshape(slots * PAGE_SIZE, HEAD_DIM).astype(BF16)
        dist = jnp.concatenate(
            [(past_len - top_ref[bi, hd, kk] * MOBA_BLOCK - key_off).astype(F32) for kk in range(MOBA_TOPK)],
            axis=1)
        s = lax.dot_general(q8, kh, NT_DIMS, preferred_element_type=F32)[0:1, :] * scale - slope * dist
        s_own = jnp.sum(q * kn_ref[:, lo:lo + HEAD_DIM], axis=-1, keepdims=True) * scale
        m = jnp.maximum(jnp.max(s, axis=-1, keepdims=True), s_own)
        p = jnp.exp(s - m)
        p_own = jnp.exp(s_own - m)
        denom = jnp.sum(p, axis=-1, keepdims=True) + p_own
        p8 = jnp.broadcast_to(p, (SUBLANES, slots * PAGE_SIZE)).astype(BF16)
        acc = jnp.dot(p8, vh, preferred_element_type=F32)[0:1, :] + p_own * vn_ref[:, lo:lo + HEAD_DIM]
        o_ref[:, lo:lo + HEAD_DIM] = acc / denom


def moba_decode(q, k_new, v_new, cache_k, cache_v, page_table, top_idx):
    nb, n_pages = page_table.shape
    slots = MOBA_TOPK * PAGES_PER_BLOCK
    row_spec = pl.BlockSpec((None, 1, D_MODEL), lambda bi, pt, top: (bi, 0, 0))
    return pl.pallas_call(
        functools.partial(_moba_decode_kernel, past_len=n_pages * PAGE_SIZE),
        grid_spec=pltpu.PrefetchScalarGridSpec(
            num_scalar_prefetch=2,
            grid=(nb,),
            in_specs=[pl.BlockSpec(memory_space=pltpu.SMEM), row_spec, row_spec, row_spec,
                      pl.BlockSpec(memory_space=pl.ANY), pl.BlockSpec(memory_space=pl.ANY)],
            out_specs=row_spec,
            scratch_shapes=[pltpu.VMEM((2, N_HEADS, slots, PAGE_SIZE, HEAD_DIM), F32),
                            pltpu.VMEM((2, N_HEADS, slots, PAGE_SIZE, HEAD_DIM), F32),
                            pltpu.SemaphoreType.DMA((2, 2, N_HEADS, slots))]),
        out_shape=jax.ShapeDtypeStruct((nb, 1, D_MODEL), F32),
        compiler_params=_params("arbitrary"),
        name="moba_decode",
    )(page_table, top_idx, alibi_slopes(), q, k_new, v_new, cache_k, cache_v)


def _trunk_prompt(x, w, nb, cache_k, page_table):
    tm = ROW_TILE
    n_seq = page_table.shape[0]
    half = n_seq // DEPTH
    conv_w, conv_b, *mixer_w = w["mixer"]
    proj_tiles = dict(tm=IN_PROJ_ROW_TILE, tn=COL_TILE)
    cw, cb = conv_w, conv_b.reshape(1, D_CONV)
    u, tail_u = in_proj(x, w["w_in"], 0, D_RG, "conv", nb, cw, cb, **proj_tiles)
    xbc, tail_xbc = in_proj(x, w["w_in"], D_RG, D_XBC, "conv_silu", nb, cw, cb, **proj_tiles)
    gate = in_proj(x, w["w_in"], D_CONV, D_RG, "gelu", nb, **proj_tiles)
    z = in_proj(x, w["w_in"], D_CONV + D_RG, D_SSD, "silu", nb, **proj_tiles)
    conv_tail = jnp.concatenate([tail_u, tail_xbc], axis=2)[:, SUBLANES - (CONV_W - 1):, :]
    ymix, h_last, s_last = mixer_prompt(u, xbc, gate, z, x, nb, w["w_dt"], *mixer_w)
    x = matmul_ln(ymix, w["w_out"], x, w["ln1_g"][0], w["ln1_b"][0], tm=tm)
    x, ksum0 = mlp_ln_ksum(x, w["w_up"], w["w_down"], 0, w["ln2_g"][0], w["ln2_b"][0],
                           cache_k, page_table, 0, half, tm=tm, tf=HIDDEN_TILE)
    q, k, v = qkv_proj(x, w["w_qkv"], tm=tm)
    o = moba_prompt(q, k, v, nb)
    x = matmul_ln(o, w["w_o"], x, w["ln1_g"][1], w["ln1_b"][1], tm=tm)
    x, ksum1 = mlp_ln_ksum(x, w["w_up"], w["w_down"], 1, w["ln2_g"][1], w["ln2_b"][1],
                           cache_k, page_table, half, n_seq - half, tm=tm, tf=HIDDEN_TILE)
    return x, conv_tail, h_last, s_last, k, v, jnp.concatenate([ksum0, ksum1], axis=0)


def _trunk_sample(x, conv_buf, h0, s0, cache_k, cache_v, page_table, ksum, w):
    nb = x.shape[0]
    proj = matmul(x, w["w_in"], tm=nb, tn=COL_TILE)
    new_buf = jnp.concatenate([conv_buf[:, 1:], proj[:, None, :D_CONV]], axis=1)
    ymix, h_new, s_new = mixer_sample(proj, x, conv_buf, h0, s0, w["w_dt"], *w["mixer"])
    x = matmul_ln(ymix, w["w_out"], x, w["ln1_g"][0], w["ln1_b"][0], tm=nb)
    x = mlp_ln(x, w["w_up"], w["w_down"], 0, w["ln2_g"][0], w["ln2_b"][0], tm=nb, tf=HIDDEN_TILE)
    q, k, v = qkv_proj(x, w["w_qkv"], tm=nb)
    top_idx = moba_topk(q.reshape(nb, N_HEADS, HEAD_DIM), ksum)[:, :, :MOBA_TOPK]
    o = moba_decode(q.reshape(nb, 1, D_MODEL), k.reshape(nb, 1, D_MODEL), v.reshape(nb, 1, D_MODEL),
                    cache_k, cache_v, page_table, top_idx).reshape(nb, D_MODEL)
    x = matmul_ln(o, w["w_o"], x, w["ln1_g"][1], w["ln1_b"][1], tm=nb)
    x = mlp_ln(x, w["w_up"], w["w_down"], 1, w["ln2_g"][1], w["ln2_b"][1], tm=nb, tf=HIDDEN_TILE)
    return x, new_buf, h_new, s_new, k, v


def kernel(x_prompt, x_sample, state_conv, state_rglru, state_ssd, cache_k, cache_v, page_table, w_in, conv_w, conv_b, rg_wa, rg_ba, rg_wx, rg_bx, rg_lambda, ssd_dt_bias, ssd_a_log, ssd_d, ssd_norm_w, w_out, w_qkv, w_o, ln1_g, ln1_b, ln2_g, ln2_b, w_up, w_down):
    bp, lp, _ = x_prompt.shape
    bs = x_sample.shape[0]
    w = {
        "w_in": w_in[0, :, :D_IN - SSD_HEADS].astype(BF16),
        "w_dt": jnp.pad(w_in[0][:, D_IN - SSD_HEADS:], ((0, 0), (0, LANES - SSD_HEADS))).astype(BF16),
        "w_out": w_out[0].astype(BF16), "w_qkv": w_qkv[0].astype(BF16), "w_o": w_o[0].astype(BF16),
        "w_up": w_up.astype(BF16), "w_down": w_down.astype(BF16),
        "ln1_g": ln1_g, "ln1_b": ln1_b, "ln2_g": ln2_g, "ln2_b": ln2_b,
        "mixer": (conv_w[0], conv_b[0], rg_wa[0], rg_wx[0], rg_ba[0], rg_bx[0], rg_lambda[0],
                  ssd_dt_bias[0], ssd_a_log[0], ssd_d[0], ssd_norm_w[0]),
    }
    yp, conv_p, h_p, s_p, k_p, v_p, ksum = _trunk_prompt(x_prompt.reshape(bp * lp, D_MODEL), w, bp,
                                                         cache_k, page_table)
    ys, conv_s, h_s, s_s, k_s, v_s = _trunk_sample(
        x_sample.reshape(bs, D_MODEL), state_conv[0], state_rglru[0], state_ssd[0],
        cache_k, cache_v, page_table, ksum, w)
    return (yp.reshape(bp, lp, D_MODEL), ys.reshape(bs, 1, D_MODEL),
            conv_p[None], h_p[None], s_p.reshape(1, bp, SSD_HEADS, SSD_HEAD_DIM, SSD_STATE),
            k_p.reshape(1, bp, lp, N_HEADS, HEAD_DIM), v_p.reshape(1, bp, lp, N_HEADS, HEAD_DIM),
            conv_s[None], h_s[None], s_s[None],
            k_s.reshape(1, bs, 1, N_HEADS, HEAD_DIM), v_s.reshape(1, bs, 1, N_HEADS, HEAD_DIM))
```

```python
import functools
import math

import jax
import jax.numpy as jnp
from jax import lax
from jax.experimental import pallas as pl
from jax.experimental.pallas import tpu as pltpu

F32 = jnp.float32
BF16 = jnp.bfloat16

D_MODEL = 2048
DEPTH = 2
PAGE_SIZE = 128
CONV_W = 4
D_RG = D_MODEL
RG_HEADS = 8
RG_BW = D_RG // RG_HEADS
RG_C = 8.0
D_SSD = D_MODEL
SSD_HEAD_DIM = 64
SSD_HEADS = D_SSD // SSD_HEAD_DIM
SSD_GROUPS = 4
SSD_STATE = 128
SSD_CHUNK = 128
SSD_GN = SSD_GROUPS * SSD_STATE
D_XBC = D_SSD + 2 * SSD_GN
D_CONV = D_RG + D_XBC
D_IN = D_CONV + D_RG + D_SSD + SSD_HEADS
D_MIX = D_RG + D_SSD
N_HEADS = 16
HEAD_DIM = D_MODEL // N_HEADS
MOBA_BLOCK = 256
MOBA_TOPK = 3
PAGES_PER_BLOCK = MOBA_BLOCK // PAGE_SIZE
KSUM_BLOCKS_PER_STEP = 2
D_FF = 4 * D_MODEL
ALPHA = (2 * DEPTH) ** 0.25
LN_EPS = 1e-5
RMS_EPS = 1e-6

SUBLANES = 8
BF16_SUBLANES = 16
LANES = 128
LOG2_E = 1.4426950408889634
VMEM_LIMIT_BYTES = 56 * 1024 * 1024

ROW_TILE = 512
IN_PROJ_ROW_TILE = 1024
COL_TILE = 1024
HIDDEN_TILE = 1024

NT_DIMS = (((1,), (1,)), ((), ()))
TN_DIMS = (((0,), (0,)), ((), ()))


def _params(*semantics):
    return pltpu.CompilerParams(dimension_semantics=semantics, vmem_limit_bytes=VMEM_LIMIT_BYTES)


def _bdot(a, b):
    return jnp.dot(a.astype(BF16), b.astype(BF16), preferred_element_type=F32)


def _layer_norm_rows(y, g, b):
    mu = jnp.mean(y, axis=-1, keepdims=True)
    yc = y - mu
    var = jnp.mean(yc * yc, axis=-1, keepdims=True)
    return yc * lax.rsqrt(var + LN_EPS) * g + b


def _softplus(x):
    u = jnp.exp(-jnp.abs(x))
    w = 1.0 + u
    log1p_u = jnp.where(w == 1.0, u, jnp.log(w) * (u / (w - 1.0)))
    return jnp.maximum(x, 0.0) + log1p_u


def _sqrt_one_minus_sq(log_a, a):
    return jnp.sqrt(-jnp.tanh(log_a) * (a * a + 1.0))


def _silu(x):
    return x * jax.nn.sigmoid(x)


def _matmul_kernel(a_ref, w_ref, o_ref, ab_ref):
    @pl.when(pl.program_id(1) == 0)
    def _():
        ab_ref[...] = a_ref[...].astype(BF16)

    o_ref[...] = jnp.dot(ab_ref[...], w_ref[...], preferred_element_type=F32)


IN_PROJ_COL_CHUNK = 256
IN_PROJ_ROW_CHUNK = 64


def _in_proj_kernel(*refs, kind, blocks_per_seq):
    has_conv = kind in ("conv", "conv_silu")
    if has_conv:
        x_ref, w_ref, cw_ref, cb_ref, o_ref, tail_ref, xb_ref, ybuf, carry = refs
    else:
        x_ref, w_ref, o_ref, xb_ref = refs
    i = pl.program_id(0)
    j = pl.program_id(1)
    tm, tn = o_ref.shape
    front = SUBLANES

    @pl.when(j == 0)
    def _():
        xb_ref[...] = x_ref[...].astype(BF16)

    if has_conv:
        @pl.when(i % blocks_per_seq == 0)
        def _():
            ybuf[0:front, :] = jnp.zeros((front, tn), F32)

        @pl.when(i % blocks_per_seq != 0)
        def _():
            ybuf[0:front, :] = carry[j]

    for c0 in range(0, tn, IN_PROJ_COL_CHUNK):
        cols = slice(c0, c0 + IN_PROJ_COL_CHUNK)
        y = jnp.dot(xb_ref[...], w_ref[:, cols], preferred_element_type=F32)
        if has_conv:
            ybuf[front:front + tm, cols] = y
        for r0 in range(0, tm, IN_PROJ_ROW_CHUNK):
            rows = slice(r0, r0 + IN_PROJ_ROW_CHUNK)
            if has_conv:
                ext = ybuf[r0:r0 + front + IN_PROJ_ROW_CHUNK, cols]
                acc = cb_ref[:, cols] + cw_ref[CONV_W - 1:CONV_W, cols] * ext[front:, :]
                for k in range(CONV_W - 1):
                    shifted = pltpu.roll(ext, CONV_W - 1 - k, axis=0)[front:, :]
                    acc = acc + cw_ref[k:k + 1, cols] * shifted
                o_ref[rows, cols] = _silu(acc) if kind == "conv_silu" else acc
            elif kind == "gelu":
                o_ref[rows, cols] = jax.nn.gelu(y[rows, :])
            else:
                o_ref[rows, cols] = _silu(y[rows, :])

    if has_conv:
        last = ybuf[tm:tm + front, :]
        carry[j] = last
        tail_ref[...] = last


def in_proj(x, w, col0, n_out, kind, nb, conv_w=None, conv_b=None, *, tm, tn):
    m, k = x.shape
    off = col0 // tn
    grid = (m // tm, n_out // tn)
    blocks_per_seq = (m // nb) // tm
    has_conv = kind in ("conv", "conv_silu")
    in_specs = [pl.BlockSpec((tm, k), lambda i, j: (i, 0)),
                pl.BlockSpec((None, k, tn), lambda i, j: (0, 0, j + off))]
    out_specs = [pl.BlockSpec((tm, tn), lambda i, j: (i, j))]
    out_shape = [jax.ShapeDtypeStruct((m, n_out), F32)]
    scratch = [pltpu.VMEM((tm, k), BF16)]
    operands = [x, w]
    if has_conv:
        in_specs += [pl.BlockSpec((CONV_W, tn), lambda i, j: (0, j + off)),
                     pl.BlockSpec((1, tn), lambda i, j: (0, j + off))]
        out_specs += [pl.BlockSpec((None, SUBLANES, tn), lambda i, j: (i, 0, j))]
        out_shape += [jax.ShapeDtypeStruct((grid[0], SUBLANES, n_out), F32)]
        scratch += [pltpu.VMEM((tm + SUBLANES, tn), F32), pltpu.VMEM((grid[1], SUBLANES, tn), F32)]
        operands += [conv_w, conv_b]
    outs = pl.pallas_call(
        functools.partial(_in_proj_kernel, kind=kind, blocks_per_seq=blocks_per_seq),
        grid=grid,
        in_specs=in_specs,
        out_specs=out_specs,
        out_shape=out_shape,
        scratch_shapes=scratch,
        compiler_params=_params("arbitrary", "arbitrary"),
        name="in_proj_" + kind,
    )(*operands)
    if not has_conv:
        return outs[0]
    return outs[0], outs[1][blocks_per_seq - 1::blocks_per_seq]


def matmul(a, w, n, *, tm, tn):
    m, k = a.shape
    return pl.pallas_call(
        _matmul_kernel,
        grid=(pl.cdiv(m, tm), pl.cdiv(n, tn)),
        in_specs=[pl.BlockSpec((tm, k), lambda i, j: (i, 0)),
                  pl.BlockSpec((None, k, tn), lambda i, j: (0, 0, j))],
        out_specs=pl.BlockSpec((tm, tn), lambda i, j: (i, j)),
        out_shape=jax.ShapeDtypeStruct((m, n), F32),
        scratch_shapes=[pltpu.VMEM((tm, k), BF16)],
        compiler_params=_params("parallel", "arbitrary"),
        name="matmul",
    )(a, w)


def _resident_w_kernel(a_ref, w_ref, o_ref):
    o_ref[...] = _bdot(a_ref[...], w_ref[...])


def qkv_proj(a, w, *, tm):
    m, k = a.shape
    n = w.shape[1] // 3

    def part(idx):
        return pl.pallas_call(
            _resident_w_kernel,
            grid=(pl.cdiv(m, tm),),
            in_specs=[pl.BlockSpec((tm, k), lambda i: (i, 0)),
                      pl.BlockSpec((k, n), lambda i: (0, idx), pipeline_mode=pl.Buffered(1))],
            out_specs=pl.BlockSpec((tm, n), lambda i: (i, 0)),
            out_shape=jax.ShapeDtypeStruct((m, n), F32),
            compiler_params=_params("parallel"),
            name="qkv_proj",
        )(a, w)

    return part(0), part(1), part(2)


LN_ROW_CHUNK = 256


def _matmul_ln_kernel(a_ref, w_ref, res_ref, g_ref, b_ref, o_ref):
    tm = o_ref.shape[0]
    chunk = min(tm, LN_ROW_CHUNK)
    for r0 in range(0, tm, chunk):
        rows = slice(r0, r0 + chunk)
        y = ALPHA * res_ref[rows, :] + _bdot(a_ref[rows, :], w_ref[...])
        o_ref[rows, :] = _layer_norm_rows(y, g_ref[...], b_ref[...])


def matmul_ln(a, w, res, g, b, *, tm):
    m, k = a.shape
    n = w.shape[1]
    return pl.pallas_call(
        _matmul_ln_kernel,
        grid=(pl.cdiv(m, tm),),
        in_specs=[pl.BlockSpec((tm, k), lambda i: (i, 0)),
                  pl.BlockSpec((k, n), lambda i: (0, 0), pipeline_mode=pl.Buffered(1)),
                  pl.BlockSpec((tm, n), lambda i: (i, 0)),
                  pl.BlockSpec((1, n), lambda i: (0, 0)),
                  pl.BlockSpec((1, n), lambda i: (0, 0))],
        out_specs=pl.BlockSpec((tm, n), lambda i: (i, 0)),
        out_shape=jax.ShapeDtypeStruct((m, n), F32),
        compiler_params=_params("parallel"),
        name="matmul_ln",
    )(a, w, res, g.reshape(1, n), b.reshape(1, n))


def _mlp_ln_body(x_ref, wu_ref, wd_ref, g_ref, b_ref, o_ref, xb_ref, shadow_work=None):
    f = pl.program_id(1)

    @pl.when(f == 0)
    def _():
        xb_ref[...] = x_ref[...].astype(BF16)
        o_ref[...] = jnp.zeros_like(o_ref)

    if shadow_work is not None:
        shadow_work()
    hdn = jnp.maximum(jnp.dot(xb_ref[...], wu_ref[...], preferred_element_type=F32), 0.0)
    o_ref[...] += _bdot(hdn * hdn, wd_ref[...])

    @pl.when(f == pl.num_programs(1) - 1)
    def _():
        y = ALPHA * x_ref[...] + o_ref[...]
        o_ref[...] = _layer_norm_rows(y, g_ref[...], b_ref[...])


def _mlp_ln_kernel(x_ref, wu_ref, wd_ref, g_ref, b_ref, o_ref, xb_ref):
    _mlp_ln_body(x_ref, wu_ref, wd_ref, g_ref, b_ref, o_ref, xb_ref)


def _mlp_ln_ksum_kernel(pt_ref, x_ref, wu_ref, wd_ref, g_ref, b_ref, ck_hbm, o_ref, ksum_ref,
                        xb_ref, kbuf, sem, *, seq0, nblk):
    step = pl.program_id(0) * pl.num_programs(1) + pl.program_id(1)
    n_steps = pl.num_programs(0) * pl.num_programs(1)
    pages_per_step = KSUM_BLOCKS_PER_STEP * PAGES_PER_BLOCK
    slot = step % 2

    def page_copies(at_step, at_slot):
        copies = []
        for pg in range(pages_per_step):
            flat = at_step * KSUM_BLOCKS_PER_STEP + pg // PAGES_PER_BLOCK
            page = pt_ref[seq0 + flat // nblk, (flat % nblk) * PAGES_PER_BLOCK + pg % PAGES_PER_BLOCK]
            copies.append(pltpu.make_async_copy(ck_hbm.at[0, page], kbuf.at[at_slot, pg], sem.at[at_slot, pg]))
        return copies

    @pl.when(step == 0)
    def _():
        for cp in page_copies(0, 0):
            cp.start()

    @pl.when(step + 1 < n_steps)
    def _():
        for cp in page_copies(step + 1, 1 - slot):
            cp.start()

    def sum_pages():
        for cp in page_copies(step, slot):
            cp.wait()
        for j in range(KSUM_BLOCKS_PER_STEP):
            acc = jnp.sum(kbuf[slot, j * PAGES_PER_BLOCK], axis=0)
            for p in range(1, PAGES_PER_BLOCK):
                acc = acc + jnp.sum(kbuf[slot, j * PAGES_PER_BLOCK + p], axis=0)
            ksum_ref[j] = acc

    _mlp_ln_body(x_ref, wu_ref, wd_ref, g_ref, b_ref, o_ref, xb_ref, sum_pages)


def mlp_ln(x, w_up, w_down, layer, g, b, *, tm, tf):
    m, d = x.shape
    ff = w_up.shape[2]
    return pl.pallas_call(
        _mlp_ln_kernel,
        grid=(pl.cdiv(m, tm), ff // tf),
        in_specs=[pl.BlockSpec((tm, d), lambda i, f: (i, 0)),
                  pl.BlockSpec((None, d, tf), lambda i, f: (layer, 0, f)),
                  pl.BlockSpec((None, tf, d), lambda i, f: (layer, f, 0)),
                  pl.BlockSpec((1, d), lambda i, f: (0, 0)),
                  pl.BlockSpec((1, d), lambda i, f: (0, 0))],
        out_specs=pl.BlockSpec((tm, d), lambda i, f: (i, 0)),
        out_shape=jax.ShapeDtypeStruct((m, d), F32),
        scratch_shapes=[pltpu.VMEM((tm, d), BF16)],
        compiler_params=_params("parallel", "arbitrary"),
        name="mlp_ln",
    )(x, w_up, w_down, g.reshape(1, d), b.reshape(1, d))


def mlp_ln_ksum(x, w_up, w_down, layer, g, b, cache_k, page_table, seq0, n_seq, *, tm, tf):
    m, d = x.shape
    ff = w_up.shape[2]
    grid = (m // tm, ff // tf)
    nblk = page_table.shape[1] // PAGES_PER_BLOCK
    assert n_seq * nblk == grid[0] * grid[1] * KSUM_BLOCKS_PER_STEP and nblk % KSUM_BLOCKS_PER_STEP == 0
    steps_per_seq = nblk // KSUM_BLOCKS_PER_STEP

    def ksum_index(i, f, pt):
        step = i * grid[1] + f
        return (step // steps_per_seq, step % steps_per_seq, 0, 0)

    return pl.pallas_call(
        functools.partial(_mlp_ln_ksum_kernel, seq0=seq0, nblk=nblk),
        grid_spec=pltpu.PrefetchScalarGridSpec(
            num_scalar_prefetch=1,
            grid=grid,
            in_specs=[pl.BlockSpec((tm, d), lambda i, f, pt: (i, 0)),
                      pl.BlockSpec((None, d, tf), lambda i, f, pt: (layer, 0, f)),
                      pl.BlockSpec((None, tf, d), lambda i, f, pt: (layer, f, 0)),
                      pl.BlockSpec((1, d), lambda i, f, pt: (0, 0)),
                      pl.BlockSpec((1, d), lambda i, f, pt: (0, 0)),
                      pl.BlockSpec(memory_space=pl.ANY)],
            out_specs=[pl.BlockSpec((tm, d), lambda i, f, pt: (i, 0)),
                       pl.BlockSpec((None, KSUM_BLOCKS_PER_STEP, N_HEADS, HEAD_DIM), ksum_index)],
            scratch_shapes=[pltpu.VMEM((tm, d), BF16),
                            pltpu.VMEM((2, KSUM_BLOCKS_PER_STEP * PAGES_PER_BLOCK, PAGE_SIZE, N_HEADS, HEAD_DIM), F32),
                            pltpu.SemaphoreType.DMA((2, KSUM_BLOCKS_PER_STEP * PAGES_PER_BLOCK))]),
        out_shape=[jax.ShapeDtypeStruct((m, d), F32),
                   jax.ShapeDtypeStruct((n_seq, nblk, N_HEADS, HEAD_DIM), F32)],
        compiler_params=_params("arbitrary", "arbitrary"),
        name="mlp_ln_ksum",
    )(page_table, x, w_up, w_down, g.reshape(1, d), b.reshape(1, d), cache_k)


def _mixer_prompt_kernel(u_ref, xbc_ref, gate_ref, z_ref, x_ref,
                         wdt_ref, wa_ref, wx_ref, ba_ref, bx_ref, lam_ref,
                         dtb_ref, alog_ref, dskip_ref, normw_ref,
                         ymix_ref, hlast_ref, state_ref,
                         a_scr, h_scr, y_scr, decay_rate_scr):
    t = SSD_CHUNK
    c = pl.program_id(1)

    @pl.when(c == 0)
    def _():
        hlast_ref[...] = jnp.zeros_like(hlast_ref)
        state_ref[...] = jnp.zeros_like(state_ref)
        decay_rate_scr[...] = -RG_C * _softplus(-lam_ref[...])

    row = lax.broadcasted_iota(jnp.int32, (t, RG_BW), 0)
    row_in_tile = row % SUBLANES
    is_start = jnp.logical_and(c == 0, row == 0)
    for hd in range(RG_HEADS):
        lo = hd * RG_BW
        u = u_ref[:, lo:lo + RG_BW]
        ub = u.astype(BF16)
        r = jax.nn.sigmoid(jnp.dot(ub, wa_ref[hd], preferred_element_type=F32) + ba_ref[:, lo:lo + RG_BW])
        i = jax.nn.sigmoid(jnp.dot(ub, wx_ref[hd], preferred_element_type=F32) + bx_ref[:, lo:lo + RG_BW])
        log_a = decay_rate_scr[:, lo:lo + RG_BW] * r
        a_raw = jnp.exp(log_a)
        a = jnp.where(is_start, 0.0, a_raw)
        mult = jnp.where(is_start, 1.0, _sqrt_one_minus_sq(log_a, a_raw))
        bt = mult * i * u
        for s in (1, 2, 4):
            keep = row_in_tile >= s
            a_sh = pltpu.roll(a, s, axis=0)
            b_sh = pltpu.roll(bt, s, axis=0)
            bt = jnp.where(keep, a * b_sh + bt, bt)
            a = jnp.where(keep, a * a_sh, a)
        a_scr[:, lo:lo + RG_BW] = a
        h_scr[:, lo:lo + RG_BW] = bt

    def carry_tile(gi, carry):
        r0 = pl.multiple_of(gi * SUBLANES, SUBLANES)
        h = a_scr[pl.ds(r0, SUBLANES), :] * carry + h_scr[pl.ds(r0, SUBLANES), :]
        h_scr[pl.ds(r0, SUBLANES), :] = h
        return h[SUBLANES - 1:SUBLANES, :]

    hlast_ref[...] = lax.fori_loop(0, t // SUBLANES, carry_tile, hlast_ref[...])
    ymix_ref[:, 0:D_RG] = (h_scr[...] * gate_ref[...]).astype(BF16)

    lane = lax.broadcasted_iota(jnp.int32, (t, LANES), 1)
    rows = lax.broadcasted_iota(jnp.int32, (t, LANES), 0)
    dt_raw = jnp.dot(x_ref[...].astype(BF16), wdt_ref[...], preferred_element_type=F32)
    dt = jnp.where(lane < SSD_HEADS, _softplus(dt_raw + dtb_ref[...]), 0.0)
    cs = dt * (-jnp.exp(alog_ref[...]))
    s = 1
    while s < t:
        cs = cs + jnp.where(rows >= s, pltpu.roll(cs, s, axis=0), 0.0)
        s *= 2
    cs_t = cs.T
    cs_last = cs[t - 1:t, :]
    decay_to_end = jnp.exp(cs_last - cs)
    exp_cs = jnp.exp(cs)
    chunk_decay = jnp.exp(cs_last)

    ii = lax.broadcasted_iota(jnp.int32, (t, t), 0)
    jj = lax.broadcasted_iota(jnp.int32, (t, t), 1)
    causal = jj <= ii
    first_head = lane < SSD_HEAD_DIM
    first_head_rows = rows < SSD_HEAD_DIM

    for g in range(SSD_GROUPS):
        b_lo = D_SSD + g * SSD_STATE
        bmat = xbc_ref[:, b_lo:b_lo + SSD_STATE].astype(BF16)
        cmat = xbc_ref[:, b_lo + SSD_GN:b_lo + SSD_GN + SSD_STATE].astype(BF16)
        cb = lax.dot_general(cmat, bmat, NT_DIMS, preferred_element_type=F32)
        for pair in range(SSD_HEADS // SSD_GROUPS // 2):
            h0 = g * (SSD_HEADS // SSD_GROUPS) + 2 * pair
            lo = h0 * SSD_HEAD_DIM

            def per_head(x2d):
                return jnp.where(first_head, x2d[:, h0:h0 + 1], x2d[:, h0 + 1:h0 + 2])

            xs = xbc_ref[:, lo:lo + LANES]
            xdt = xs * per_head(dt)
            xdt_b = xdt.astype(BF16)
            y_heads = []
            for hh in (h0, h0 + 1):
                seg = cs[:, hh:hh + 1] - cs_t[hh:hh + 1, :]
                lmat = jnp.exp(jnp.where(causal, seg, -jnp.inf))
                y_heads.append(jnp.dot((cb * lmat).astype(BF16), xdt_b, preferred_element_type=F32))
            y = jnp.where(first_head, y_heads[0], y_heads[1])
            s_prev = state_ref[lo:lo + LANES, :]
            y = y + lax.dot_general(cmat, s_prev.astype(BF16), NT_DIMS,
                                    preferred_element_type=F32) * per_head(exp_cs)
            y_scr[:, lo:lo + LANES] = y + dskip_ref[:, lo:lo + LANES] * xs
            decay_rows = jnp.where(first_head_rows, chunk_decay[:, h0:h0 + 1], chunk_decay[:, h0 + 1:h0 + 2])
            state_ref[lo:lo + LANES, :] = s_prev * decay_rows + lax.dot_general(
                (xdt * per_head(decay_to_end)).astype(BF16), bmat, TN_DIMS, preferred_element_type=F32)

    gw = D_SSD // SSD_GROUPS
    for g in range(SSD_GROUPS):
        lo = g * gw
        yz = y_scr[:, lo:lo + gw] * z_ref[:, lo:lo + gw]
        ms = jnp.mean(yz * yz, axis=-1, keepdims=True)
        ymix_ref[:, D_RG + lo:D_RG + lo + gw] = (yz * lax.rsqrt(ms + RMS_EPS)
                                                 * normw_ref[:, lo:lo + gw]).astype(BF16)


def _pad_lanes(v):
    return jnp.zeros((1, LANES), F32).at[0, :v.shape[0]].set(v)


def mixer_prompt(u, xbc, gate, z, x, nb, w_dt, wa, wx, ba, bx, lam, dt_bias, a_log, d_skip, norm_w):
    m = u.shape[0]
    l = m // nb
    t = SSD_CHUNK
    nc = l // t
    row_blk = lambda bi, ci: bi * nc + ci
    rows = lambda width: pl.BlockSpec((t, width), lambda bi, ci: (row_blk(bi, ci), 0))
    full = lambda shape: pl.BlockSpec(shape, lambda bi, ci: (0,) * len(shape))
    in_specs = [rows(D_RG), rows(D_XBC), rows(D_RG), rows(D_SSD), rows(D_MODEL)]
    in_specs += [full((D_MODEL, LANES)),
                 full((RG_HEADS, RG_BW, RG_BW)), full((RG_HEADS, RG_BW, RG_BW)),
                 full((1, D_RG)), full((1, D_RG)), full((1, D_RG)),
                 full((1, LANES)), full((1, LANES)), full((1, D_SSD)), full((1, D_SSD))]
    ymix, hlast, state = pl.pallas_call(
        _mixer_prompt_kernel,
        grid=(nb, nc),
        in_specs=in_specs,
        out_specs=[pl.BlockSpec((t, D_MIX), lambda bi, ci: (row_blk(bi, ci), 0)),
                   pl.BlockSpec((None, 1, D_RG), lambda bi, ci: (bi, 0, 0)),
                   pl.BlockSpec((None, D_SSD, SSD_STATE), lambda bi, ci: (bi, 0, 0))],
        out_shape=[jax.ShapeDtypeStruct((m, D_MIX), BF16),
                   jax.ShapeDtypeStruct((nb, 1, D_RG), F32),
                   jax.ShapeDtypeStruct((nb, D_SSD, SSD_STATE), F32)],
        scratch_shapes=[pltpu.VMEM((t, D_RG), F32), pltpu.VMEM((t, D_RG), F32),
                        pltpu.VMEM((t, D_SSD), F32), pltpu.VMEM((1, D_RG), F32)],
        compiler_params=_params("parallel", "arbitrary"),
        name="mixer_prompt",
    )(u, xbc, gate, z, x, w_dt, wa.astype(BF16), wx.astype(BF16),
      ba.reshape(1, D_RG), bx.reshape(1, D_RG), lam.reshape(1, D_RG),
      _pad_lanes(dt_bias), _pad_lanes(a_log),
      jnp.repeat(d_skip, SSD_HEAD_DIM).reshape(1, D_SSD), norm_w.reshape(1, D_SSD))
    return ymix, hlast.reshape(nb, D_RG), state


def _moba_prompt_kernel(slopes_ref, q_ref, k_ref, v_ref, o_ref, kb_ref, vt_ref, pen_ref):
    blk = MOBA_BLOCK
    seq = k_ref.shape[0]
    nblk = seq // blk
    slope = slopes_ref[pl.program_id(1)] * LOG2_E
    scale = HEAD_DIM ** -0.5 * LOG2_E

    kmeans = []
    for n in range(nblk):
        rows = slice(n * blk, (n + 1) * blk)
        kn = k_ref[rows, :]
        kmeans.append(jnp.mean(kn, axis=0, keepdims=True))
        kb_ref[rows, :] = kn.astype(BF16)
        vt_ref[0:HEAD_DIM, rows] = v_ref[rows, :].T.astype(BF16)
    vt_ref[HEAD_DIM:, :] = jnp.ones((vt_ref.shape[0] - HEAD_DIM, seq), BF16)
    kmean = jnp.concatenate(kmeans, axis=0)
    pen_ref[...] = slope * (lax.broadcasted_iota(jnp.int32, (seq, blk), 1)
                            - lax.broadcasted_iota(jnp.int32, (seq, blk), 0)).astype(F32)
    blk_id = lax.broadcasted_iota(jnp.int32, (nblk, blk), 0)
    causal = (lax.broadcasted_iota(jnp.int32, (blk, blk), 0) <= lax.broadcasted_iota(jnp.int32, (blk, blk), 1))

    for qi in range(nblk):
        q = q_ref[qi * blk:(qi + 1) * blk, :]
        qb = q.astype(BF16)
        base = -slope * float(qi * blk)
        if qi > MOBA_TOPK:
            gate = lax.dot_general(kmean, q, NT_DIMS, precision=lax.Precision.HIGHEST,
                                   preferred_element_type=F32)
            gate = jnp.where(blk_id < qi, gate, -jnp.inf)
            ahead = jnp.zeros((nblk, blk), jnp.int32)
            for n in range(qi):
                gate_n = gate[n:n + 1, :]
                before = jnp.logical_or(gate_n > gate, jnp.logical_and(gate_n == gate, n < blk_id))
                ahead = ahead + before.astype(jnp.int32)
            row_terms = jnp.where(ahead < MOBA_TOPK, base, -jnp.inf)
        else:
            row_terms = jnp.full((nblk, blk), base, F32)

        s_blocks = []
        for n in range(qi + 1):
            rows = slice(n * blk, (n + 1) * blk)
            s = lax.dot_general(kb_ref[rows, :], qb, NT_DIMS, preferred_element_type=F32)
            s = s * scale - pen_ref[rows, :]
            if n < qi:
                s = s + row_terms[n:n + 1, :]
            else:
                s = jnp.where(causal, s + base, -jnp.inf)
            s_blocks.append(s)
        m = jnp.max(s_blocks[0], axis=0, keepdims=True)
        for s in s_blocks[1:]:
            m = jnp.maximum(m, jnp.max(s, axis=0, keepdims=True))
        acc = jnp.zeros((vt_ref.shape[0], blk), F32)
        for n, s in enumerate(s_blocks):
            p = jnp.exp2(s - m)
            acc = acc + jnp.dot(vt_ref[:, n * blk:(n + 1) * blk], p.astype(BF16), preferred_element_type=F32)
        denom = acc[HEAD_DIM:HEAD_DIM + 1, :]
        o_ref[qi * blk:(qi + 1) * blk, :] = (acc[0:HEAD_DIM, :] / denom).T.astype(o_ref.dtype)


def alibi_slopes():
    return jnp.exp2(-8.0 * jnp.arange(1, N_HEADS + 1, dtype=F32) / N_HEADS)


def moba_prompt(q, k, v, nb):
    m = q.shape[0]
    l = m // nb
    head_spec = pl.BlockSpec((l, HEAD_DIM), lambda bi, hi: (bi, hi))
    return pl.pallas_call(
        _moba_prompt_kernel,
        grid=(nb, N_HEADS),
        in_specs=[pl.BlockSpec(memory_space=pltpu.SMEM), head_spec, head_spec, head_spec],
        out_specs=head_spec,
        out_shape=jax.ShapeDtypeStruct((m, D_MODEL), BF16),
        scratch_shapes=[pltpu.VMEM((l, HEAD_DIM), BF16), pltpu.VMEM((HEAD_DIM + BF16_SUBLANES, l), BF16),
                        pltpu.VMEM((l, MOBA_BLOCK), F32)],
        compiler_params=_params("parallel", "parallel"),
        name="moba_prompt",
    )(alibi_slopes(), q, k, v)


def _mixer_step_a_kernel(proj_ref, x_ref, b0_ref, b1_ref, b2_ref, h0_ref,
                         wdt_ref, convw_ref, convb_ref, wa_ref, wx_ref, ba_ref, bx_ref, lam_ref, dtb_ref,
                         yrg_ref, hnew_ref, xbc_ref, dt_ref):
    conv = (convb_ref[...] + convw_ref[0:1, :] * b0_ref[...] + convw_ref[1:2, :] * b1_ref[...]
            + convw_ref[2:3, :] * b2_ref[...] + convw_ref[3:4, :] * proj_ref[:, 0:D_CONV])
    xbc_ref[...] = _silu(conv[:, D_RG:])
    dt_raw = jnp.dot(x_ref[...].astype(BF16), wdt_ref[...], preferred_element_type=F32)
    dt_ref[...] = _softplus(dt_raw[:, 0:SSD_HEADS] + dtb_ref[...])
    for hd in range(RG_HEADS):
        lo = hd * RG_BW
        u = conv[:, lo:lo + RG_BW]
        ub = u.astype(BF16)
        r = jax.nn.sigmoid(jnp.dot(ub, wa_ref[hd], preferred_element_type=F32) + ba_ref[:, lo:lo + RG_BW])
        i = jax.nn.sigmoid(jnp.dot(ub, wx_ref[hd], preferred_element_type=F32) + bx_ref[:, lo:lo + RG_BW])
        log_a = -RG_C * r * _softplus(-lam_ref[:, lo:lo + RG_BW])
        a = jnp.exp(log_a)
        h = a * h0_ref[:, lo:lo + RG_BW] + _sqrt_one_minus_sq(log_a, a) * i * u
        hnew_ref[:, lo:lo + RG_BW] = h
        gate = proj_ref[:, D_CONV + lo:D_CONV + lo + RG_BW]
        yrg_ref[:, lo:lo + RG_BW] = h * jax.nn.gelu(gate)


def _ssd_step_kernel(s0_ref, xs_ref, dt_ref, alog_ref, b_ref, c_ref, snew_ref, y_ref):
    dt = dt_ref[...]
    decay = jnp.exp(dt * (-jnp.exp(alog_ref[...])))
    s_new = s0_ref[...] * decay + (xs_ref[...] * dt) * b_ref[...]
    snew_ref[...] = s_new
    y_ref[...] = jnp.sum(s_new * c_ref[...], axis=-1, keepdims=True)


def _mixer_step_b_kernel(yrg_ref, y_ref, xs_ref, z_ref, dskip_ref, normw_ref, o_ref):
    o_ref[:, 0:D_RG] = yrg_ref[...].astype(BF16)
    yz = (y_ref[...] + dskip_ref[...] * xs_ref[...]) * _silu(z_ref[...])
    gw = D_SSD // SSD_GROUPS
    for g in range(SSD_GROUPS):
        blk = yz[:, g * gw:(g + 1) * gw]
        ms = jnp.mean(blk * blk, axis=-1, keepdims=True)
        o_ref[:, D_RG + g * gw:D_RG + (g + 1) * gw] = (
            blk * lax.rsqrt(ms + RMS_EPS) * normw_ref[:, g * gw:(g + 1) * gw]).astype(BF16)


def _whole(shape):
    return pl.BlockSpec(shape, lambda *_: (0,) * len(shape))


def mixer_sample(proj, x, conv_buf, h0, s0, w_dt, conv_w, conv_b, wa, wx, ba, bx, lam, dt_bias, a_log, d_skip,
                 norm_w):
    nb = proj.shape[0]
    hpg = SSD_HEADS // SSD_GROUPS
    a_ins = (proj, x, conv_buf[:, 0], conv_buf[:, 1], conv_buf[:, 2], h0,
             w_dt, conv_w, conv_b.reshape(1, D_CONV), wa.astype(BF16), wx.astype(BF16),
             ba.reshape(1, D_RG), bx.reshape(1, D_RG), lam.reshape(1, D_RG), dt_bias.reshape(1, SSD_HEADS))
    yrg, hnew, xbc, dt = pl.pallas_call(
        _mixer_step_a_kernel,
        grid=(1,),
        in_specs=[_whole(a.shape) for a in a_ins],
        out_specs=[_whole((nb, D_RG)), _whole((nb, D_RG)), _whole((nb, D_XBC)), _whole((nb, SSD_HEADS))],
        out_shape=[jax.ShapeDtypeStruct((nb, D_RG), F32), jax.ShapeDtypeStruct((nb, D_RG), F32),
                   jax.ShapeDtypeStruct((nb, D_XBC), F32), jax.ShapeDtypeStruct((nb, SSD_HEADS), F32)],
        compiler_params=_params("arbitrary"),
        name="mixer_step_a",
    )(*a_ins)
    xs = xbc[:, :D_SSD]
    snew, y = pl.pallas_call(
        _ssd_step_kernel,
        grid=(nb, SSD_GROUPS),
        in_specs=[pl.BlockSpec((None, hpg, SSD_HEAD_DIM, SSD_STATE), lambda bi, gi: (bi, gi, 0, 0)),
                  pl.BlockSpec((None, hpg, SSD_HEAD_DIM, 1), lambda bi, gi: (bi, gi, 0, 0)),
                  pl.BlockSpec((None, hpg, 1, 1), lambda bi, gi: (bi, gi, 0, 0)),
                  pl.BlockSpec((hpg, 1, 1), lambda bi, gi: (gi, 0, 0)),
                  pl.BlockSpec((None, None, 1, SSD_STATE), lambda bi, gi: (bi, gi, 0, 0)),
                  pl.BlockSpec((None, None, 1, SSD_STATE), lambda bi, gi: (bi, gi, 0, 0))],
        out_specs=[pl.BlockSpec((None, hpg, SSD_HEAD_DIM, SSD_STATE), lambda bi, gi: (bi, gi, 0, 0)),
                   pl.BlockSpec((None, hpg, SSD_HEAD_DIM, 1), lambda bi, gi: (bi, gi, 0, 0))],
        out_shape=[jax.ShapeDtypeStruct((nb, SSD_HEADS, SSD_HEAD_DIM, SSD_STATE), F32),
                   jax.ShapeDtypeStruct((nb, SSD_HEADS, SSD_HEAD_DIM, 1), F32)],
        compiler_params=_params("parallel", "parallel"),
        name="ssd_step",
    )(s0, xs.reshape(nb, SSD_HEADS, SSD_HEAD_DIM, 1), dt.reshape(nb, SSD_HEADS, 1, 1),
      a_log.reshape(SSD_HEADS, 1, 1),
      xbc[:, D_SSD:D_SSD + SSD_GN].reshape(nb, SSD_GROUPS, 1, SSD_STATE),
      xbc[:, D_SSD + SSD_GN:].reshape(nb, SSD_GROUPS, 1, SSD_STATE))
    b_ins = (yrg, y.reshape(nb, D_SSD), xs, proj[:, D_CONV + D_RG:D_CONV + D_RG + D_SSD],
             jnp.repeat(d_skip, SSD_HEAD_DIM).reshape(1, D_SSD), norm_w.reshape(1, D_SSD))
    ymix = pl.pallas_call(
        _mixer_step_b_kernel,
        grid=(1,),
        in_specs=[_whole(a.shape) for a in b_ins],
        out_specs=_whole((nb, D_MIX)),
        out_shape=jax.ShapeDtypeStruct((nb, D_MIX), BF16),
        compiler_params=_params("arbitrary"),
        name="mixer_step_b",
    )(*b_ins)
    return ymix, hnew, snew


def _moba_topk_kernel(q_ref, ksum_ref, idx_ref):
    nblk = ksum_ref.shape[0]
    qh = q_ref[...] * (1.0 / MOBA_BLOCK)
    lane = lax.broadcasted_iota(jnp.int32, (N_HEADS, LANES), 1)
    gate = jnp.full((N_HEADS, LANES), -jnp.inf, F32)
    for n in range(nblk):
        gate = jnp.where(lane == n, jnp.sum(ksum_ref[n] * qh, axis=-1, keepdims=True), gate)
    out = jnp.zeros((N_HEADS, LANES), jnp.int32)
    for kk in range(MOBA_TOPK):
        best = jnp.max(gate, axis=-1, keepdims=True)
        pick = jnp.min(jnp.where(gate == best, lane, LANES), axis=-1, keepdims=True)
        out = jnp.where(lane == kk, pick, out)
        gate = jnp.where(lane == pick, -jnp.inf, gate)
    idx_ref[...] = out


def moba_topk(q, ksum):
    nb, nblk = ksum.shape[:2]
    assert MOBA_TOPK <= nblk <= LANES
    return pl.pallas_call(
        _moba_topk_kernel,
        grid=(nb,),
        in_specs=[pl.BlockSpec((None, N_HEADS, HEAD_DIM), lambda bi: (bi, 0, 0)),
                  pl.BlockSpec((None, nblk, N_HEADS, HEAD_DIM), lambda bi: (bi, 0, 0, 0))],
        out_specs=pl.BlockSpec((None, N_HEADS, LANES), lambda bi: (bi, 0, 0)),
        out_shape=jax.ShapeDtypeStruct((nb, N_HEADS, LANES), jnp.int32),
        compiler_params=_params("parallel"),
        name="moba_topk",
    )(q, ksum)


def _moba_decode_kernel(pt_ref, top_ref, slopes_ref, q_ref, kn_ref, vn_ref, ck_hbm, cv_hbm, o_ref,
                        kbuf, vbuf, sem, *, past_len):
    bi = pl.program_id(0)
    n_seq = pl.num_programs(0)
    buf = bi % 2
    scale = HEAD_DIM ** -0.5
    slots = MOBA_TOPK * PAGES_PER_BLOCK

    def head_copies(seq, at_buf, hd):
        copies = []
        for kk in range(MOBA_TOPK):
            blk_id = top_ref[seq, hd, kk]
            for half in range(PAGES_PER_BLOCK):
                page = pt_ref[seq, PAGES_PER_BLOCK * blk_id + half]
                slot = kk * PAGES_PER_BLOCK + half
                copies.append(pltpu.make_async_copy(ck_hbm.at[0, page, :, hd, :], kbuf.at[at_buf, hd, slot],
                                                    sem.at[at_buf, 0, hd, slot]))
                copies.append(pltpu.make_async_copy(cv_hbm.at[0, page, :, hd, :], vbuf.at[at_buf, hd, slot],
                                                    sem.at[at_buf, 1, hd, slot]))
        return copies

    @pl.when(bi == 0)
    def _():
        for hd in range(N_HEADS):
            for cp in head_copies(0, 0, hd):
                cp.start()

    @pl.when(bi + 1 < n_seq)
    def _():
        for hd in range(N_HEADS):
            for cp in head_copies(bi + 1, 1 - buf, hd):
                cp.start()

    key_off = lax.broadcasted_iota(jnp.int32, (1, MOBA_BLOCK), 1)
    for hd in range(N_HEADS):
        for cp in head_copies(bi, buf, hd):
            cp.wait()
        lo = hd * HEAD_DIM
        slope = slopes_ref[hd]
        q = q_ref[:, lo:lo + HEAD_DIM]
        q8 = jnp.broadcast_to(q, (SUBLANES, HEAD_DIM)).astype(BF16)
        kh = kbuf[buf, hd].reshape(slots * PAGE_SIZE, HEAD_DIM).astype(BF16)
        vh = vbuf[buf, hd].reshape(slots * PAGE_SIZE, HEAD_DIM).astype(BF16)
        dist = jnp.concatenate(
            [(past_len - top_ref[bi, hd, kk] * MOBA_BLOCK - key_off).astype(F32) for kk in range(MOBA_TOPK)],
            axis=1)
        s = lax.dot_general(q8, kh, NT_DIMS, preferred_element_type=F32)[0:1, :] * scale - slope * dist
        s_own = jnp.sum(q * kn_ref[:, lo:lo + HEAD_DIM], axis=-1, keepdims=True) * scale
        m = jnp.maximum(jnp.max(s, axis=-1, keepdims=True), s_own)
        p = jnp.exp(s - m)
        p_own = jnp.exp(s_own - m)
        denom = jnp.sum(p, axis=-1, keepdims=True) + p_own
        p8 = jnp.broadcast_to(p, (SUBLANES, slots * PAGE_SIZE)).astype(BF16)
        acc = jnp.dot(p8, vh, preferred_element_type=F32)[0:1, :] + p_own * vn_ref[:, lo:lo + HEAD_DIM]
        o_ref[:, lo:lo + HEAD_DIM] = acc / denom


def moba_decode(q, k_new, v_new, cache_k, cache_v, page_table, top_idx):
    nb, n_pages = page_table.shape
    slots = MOBA_TOPK * PAGES_PER_BLOCK
    row_spec = pl.BlockSpec((None, 1, D_MODEL), lambda bi, pt, top: (bi, 0, 0))
    return pl.pallas_call(
        functools.partial(_moba_decode_kernel, past_len=n_pages * PAGE_SIZE),
        grid_spec=pltpu.PrefetchScalarGridSpec(
            num_scalar_prefetch=2,
            grid=(nb,),
            in_specs=[pl.BlockSpec(memory_space=pltpu.SMEM), row_spec, row_spec, row_spec,
                      pl.BlockSpec(memory_space=pl.ANY), pl.BlockSpec(memory_space=pl.ANY)],
            out_specs=row_spec,
            scratch_shapes=[pltpu.VMEM((2, N_HEADS, slots, PAGE_SIZE, HEAD_DIM), F32),
                            pltpu.VMEM((2, N_HEADS, slots, PAGE_SIZE, HEAD_DIM), F32),
                            pltpu.SemaphoreType.DMA((2, 2, N_HEADS, slots))]),
        out_shape=jax.ShapeDtypeStruct((nb, 1, D_MODEL), F32),
        compiler_params=_params("arbitrary"),
        name="moba_decode",
    )(page_table, top_idx, alibi_slopes(), q, k_new, v_new, cache_k, cache_v)


def _trunk_prompt(x, w, nb, cache_k, page_table):
    tm = ROW_TILE
    n_seq = page_table.shape[0]
    half = n_seq // DEPTH
    conv_w, conv_b, *mixer_w = w["mixer"]
    proj_tiles = dict(tm=IN_PROJ_ROW_TILE, tn=COL_TILE)
    cw, cb = conv_w, conv_b.reshape(1, D_CONV)
    u, tail_u = in_proj(x, w["w_in"], 0, D_RG, "conv", nb, cw, cb, **proj_tiles)
    xbc, tail_xbc = in_proj(x, w["w_in"], D_RG, D_XBC, "conv_silu", nb, cw, cb, **proj_tiles)
    gate = in_proj(x, w["w_in"], D_CONV, D_RG, "gelu", nb, **proj_tiles)
    z = in_proj(x, w["w_in"], D_CONV + D_RG, D_SSD, "silu", nb, **proj_tiles)
    conv_tail = jnp.concatenate([tail_u, tail_xbc], axis=2)[:, SUBLANES - (CONV_W - 1):, :]
    ymix, h_last, s_last = mixer_prompt(u, xbc, gate, z, x, nb, w["w_dt"], *mixer_w)
    x = matmul_ln(ymix, w["w_out"], x, w["ln1_g"][0], w["ln1_b"][0], tm=tm)
    x, ksum0 = mlp_ln_ksum(x, w["w_up"], w["w_down"], 0, w["ln2_g"][0], w["ln2_b"][0],
                           cache_k, page_table, 0, half, tm=tm, tf=HIDDEN_TILE)
    q, k, v = qkv_proj(x, w["w_qkv"], tm=tm)
    o = moba_prompt(q, k, v, nb)
    x = matmul_ln(o, w["w_o"], x, w["ln1_g"][1], w["ln1_b"][1], tm=tm)
    x, ksum1 = mlp_ln_ksum(x, w["w_up"], w["w_down"], 1, w["ln2_g"][1], w["ln2_b"][1],
                           cache_k, page_table, half, n_seq - half, tm=tm, tf=HIDDEN_TILE)
    return x, conv_tail, h_last, s_last, k, v, jnp.concatenate([ksum0, ksum1], axis=0)


def _trunk_sample(x, conv_buf, h0, s0, cache_k, cache_v, page_table, ksum, w):
    nb = x.shape[0]
    proj = matmul(x, w["w_in"], D_IN - SSD_HEADS, tm=nb, tn=COL_TILE)
    new_buf = jnp.concatenate([conv_buf[:, 1:], proj[:, None, :D_CONV]], axis=1)
    ymix, h_new, s_new = mixer_sample(proj, x, conv_buf, h0, s0, w["w_dt"], *w["mixer"])
    x = matmul_ln(ymix, w["w_out"], x, w["ln1_g"][0], w["ln1_b"][0], tm=nb)
    x = mlp_ln(x, w["w_up"], w["w_down"], 0, w["ln2_g"][0], w["ln2_b"][0], tm=nb, tf=HIDDEN_TILE)
    q, k, v = qkv_proj(x, w["w_qkv"], tm=nb)
    top_idx = moba_topk(q.reshape(nb, N_HEADS, HEAD_DIM), ksum)[:, :, :MOBA_TOPK]
    o = moba_decode(q.reshape(nb, 1, D_MODEL), k.reshape(nb, 1, D_MODEL), v.reshape(nb, 1, D_MODEL),
                    cache_k, cache_v, page_table, top_idx).reshape(nb, D_MODEL)
    x = matmul_ln(o, w["w_o"], x, w["ln1_g"][1], w["ln1_b"][1], tm=nb)
    x = mlp_ln(x, w["w_up"], w["w_down"], 1, w["ln2_g"][1], w["ln2_b"][1], tm=nb, tf=HIDDEN_TILE)
    return x, new_buf, h_new, s_new, k, v


def kernel(x_prompt, x_sample, state_conv, state_rglru, state_ssd, cache_k, cache_v, page_table, w_in, conv_w, conv_b, rg_wa, rg_ba, rg_wx, rg_bx, rg_lambda, ssd_dt_bias, ssd_a_log, ssd_d, ssd_norm_w, w_out, w_qkv, w_o, ln1_g, ln1_b, ln2_g, ln2_b, w_up, w_down):
    bp, lp, _ = x_prompt.shape
    bs = x_sample.shape[0]
    w = {
        "w_in": w_in.astype(BF16),
        "w_dt": jnp.pad(w_in[0][:, D_IN - SSD_HEADS:], ((0, 0), (0, LANES - SSD_HEADS))).astype(BF16),
        "w_out": w_out[0].astype(BF16), "w_qkv": w_qkv[0].astype(BF16), "w_o": w_o[0].astype(BF16),
        "w_up": w_up.astype(BF16), "w_down": w_down.astype(BF16),
        "ln1_g": ln1_g, "ln1_b": ln1_b, "ln2_g": ln2_g, "ln2_b": ln2_b,
        "mixer": (conv_w[0], conv_b[0], rg_wa[0], rg_wx[0], rg_ba[0], rg_bx[0], rg_lambda[0],
                  ssd_dt_bias[0], ssd_a_log[0], ssd_d[0], ssd_norm_w[0]),
    }
    yp, conv_p, h_p, s_p, k_p, v_p, ksum = _trunk_prompt(x_prompt.reshape(bp * lp, D_MODEL), w, bp,
                                                         cache_k, page_table)
    ys, conv_s, h_s, s_s, k_s, v_s = _trunk_sample(
        x_sample.reshape(bs, D_MODEL), state_conv[0], state_rglru[0], state_ssd[0],
        cache_k, cache_v, page_table, ksum, w)
    return (yp.reshape(bp, lp, D_MODEL), ys.reshape(bs, 1, D_MODEL),
            conv_p[None], h_p[None], s_p.reshape(1, bp, SSD_HEADS, SSD_HEAD_DIM, SSD_STATE),
            k_p.reshape(1, bp, lp, N_HEADS, HEAD_DIM), v_p.reshape(1, bp, lp, N_HEADS, HEAD_DIM),
            conv_s[None], h_s[None], s_s[None],
            k_s.reshape(1, bs, 1, N_HEADS, HEAD_DIM), v_s.reshape(1, bs, 1, N_HEADS, HEAD_DIM))
```

```python
import functools
import math

import jax
import jax.numpy as jnp
from jax import lax
from jax.experimental import pallas as pl
from jax.experimental.pallas import tpu as pltpu

F32 = jnp.float32
BF16 = jnp.bfloat16

D_MODEL = 2048
DEPTH = 2
PAGE_SIZE = 128
CONV_W = 4
D_RG = D_MODEL
RG_HEADS = 8
RG_BW = D_RG // RG_HEADS
RG_C = 8.0
D_SSD = D_MODEL
SSD_HEAD_DIM = 64
SSD_HEADS = D_SSD // SSD_HEAD_DIM
SSD_GROUPS = 4
SSD_STATE = 128
SSD_CHUNK = 128
SSD_GN = SSD_GROUPS * SSD_STATE
D_XBC = D_SSD + 2 * SSD_GN
D_CONV = D_RG + D_XBC
D_IN = D_CONV + D_RG + D_SSD + SSD_HEADS
D_MIX = D_RG + D_SSD
N_HEADS = 16
HEAD_DIM = D_MODEL // N_HEADS
MOBA_BLOCK = 256
MOBA_TOPK = 3
PAGES_PER_BLOCK = MOBA_BLOCK // PAGE_SIZE
KSUM_BLOCKS_PER_STEP = 2
D_FF = 4 * D_MODEL
ALPHA = (2 * DEPTH) ** 0.25
LN_EPS = 1e-5
RMS_EPS = 1e-6

SUBLANES = 8
BF16_SUBLANES = 16
LANES = 128
LOG2_E = 1.4426950408889634
VMEM_LIMIT_BYTES = 56 * 1024 * 1024

ROW_TILE = 512
IN_PROJ_ROW_TILE = 1024
COL_TILE = 1024
HIDDEN_TILE = 1024

NT_DIMS = (((1,), (1,)), ((), ()))
TN_DIMS = (((0,), (0,)), ((), ()))


def _params(*semantics):
    return pltpu.CompilerParams(dimension_semantics=semantics, vmem_limit_bytes=VMEM_LIMIT_BYTES)


def _bdot(a, b):
    return jnp.dot(a.astype(BF16), b.astype(BF16), preferred_element_type=F32)


def _layer_norm_rows(y, g, b):
    mu = jnp.mean(y, axis=-1, keepdims=True)
    yc = y - mu
    var = jnp.mean(yc * yc, axis=-1, keepdims=True)
    return yc * lax.rsqrt(var + LN_EPS) * g + b


def _softplus(x):
    u = jnp.exp(-jnp.abs(x))
    w = 1.0 + u
    log1p_u = jnp.where(w == 1.0, u, jnp.log(w) * (u / (w - 1.0)))
    return jnp.maximum(x, 0.0) + log1p_u


def _sqrt_one_minus_sq(log_a, a):
    return jnp.sqrt(-jnp.tanh(log_a) * (a * a + 1.0))


def _silu(x):
    return x * jax.nn.sigmoid(x)


def _matmul_kernel(a_ref, w_ref, o_ref, ab_ref):
    @pl.when(pl.program_id(1) == 0)
    def _():
        ab_ref[...] = a_ref[...].astype(BF16)

    o_ref[...] = jnp.dot(ab_ref[...], w_ref[...], preferred_element_type=F32)


IN_PROJ_COL_CHUNK = 256
IN_PROJ_ROW_CHUNK = 64


def _in_proj_kernel(*refs, kind, blocks_per_seq):
    has_conv = kind in ("conv", "conv_silu")
    if has_conv:
        x_ref, w_ref, cw_ref, cb_ref, o_ref, tail_ref, xb_ref, ybuf, carry = refs
    else:
        x_ref, w_ref, o_ref, xb_ref = refs
    i = pl.program_id(0)
    j = pl.program_id(1)
    tm, tn = o_ref.shape
    front = SUBLANES

    @pl.when(j == 0)
    def _():
        xb_ref[...] = x_ref[...].astype(BF16)

    if has_conv:
        @pl.when(i % blocks_per_seq == 0)
        def _():
            ybuf[0:front, :] = jnp.zeros((front, tn), F32)

        @pl.when(i % blocks_per_seq != 0)
        def _():
            ybuf[0:front, :] = carry[j]

    for c0 in range(0, tn, IN_PROJ_COL_CHUNK):
        cols = slice(c0, c0 + IN_PROJ_COL_CHUNK)
        y = jnp.dot(xb_ref[...], w_ref[:, cols], preferred_element_type=F32)
        if has_conv:
            ybuf[front:front + tm, cols] = y
        for r0 in range(0, tm, IN_PROJ_ROW_CHUNK):
            rows = slice(r0, r0 + IN_PROJ_ROW_CHUNK)
            if has_conv:
                ext = ybuf[r0:r0 + front + IN_PROJ_ROW_CHUNK, cols]
                acc = cb_ref[:, cols] + cw_ref[CONV_W - 1:CONV_W, cols] * ext[front:, :]
                for k in range(CONV_W - 1):
                    shifted = pltpu.roll(ext, CONV_W - 1 - k, axis=0)[front:, :]
                    acc = acc + cw_ref[k:k + 1, cols] * shifted
                o_ref[rows, cols] = _silu(acc) if kind == "conv_silu" else acc
            elif kind == "gelu":
                o_ref[rows, cols] = jax.nn.gelu(y[rows, :])
            else:
                o_ref[rows, cols] = _silu(y[rows, :])

    if has_conv:
        last = ybuf[tm:tm + front, :]
        carry[j] = last
        tail_ref[...] = last


def in_proj(x, w, col0, n_out, kind, nb, conv_w=None, conv_b=None, *, tm, tn):
    m, k = x.shape
    off = col0 // tn
    grid = (m // tm, n_out // tn)
    blocks_per_seq = (m // nb) // tm
    has_conv = kind in ("conv", "conv_silu")
    in_specs = [pl.BlockSpec((tm, k), lambda i, j: (i, 0)),
                pl.BlockSpec((None, k, tn), lambda i, j: (0, 0, j + off))]
    out_specs = [pl.BlockSpec((tm, tn), lambda i, j: (i, j))]
    out_shape = [jax.ShapeDtypeStruct((m, n_out), F32)]
    scratch = [pltpu.VMEM((tm, k), BF16)]
    operands = [x, w]
    if has_conv:
        in_specs += [pl.BlockSpec((CONV_W, tn), lambda i, j: (0, j + off)),
                     pl.BlockSpec((1, tn), lambda i, j: (0, j + off))]
        out_specs += [pl.BlockSpec((None, SUBLANES, tn), lambda i, j: (i, 0, j))]
        out_shape += [jax.ShapeDtypeStruct((grid[0], SUBLANES, n_out), F32)]
        scratch += [pltpu.VMEM((tm + SUBLANES, tn), F32), pltpu.VMEM((grid[1], SUBLANES, tn), F32)]
        operands += [conv_w, conv_b]
    outs = pl.pallas_call(
        functools.partial(_in_proj_kernel, kind=kind, blocks_per_seq=blocks_per_seq),
        grid=grid,
        in_specs=in_specs,
        out_specs=out_specs,
        out_shape=out_shape,
        scratch_shapes=scratch,
        compiler_params=_params("arbitrary", "arbitrary"),
        name="in_proj_" + kind,
    )(*operands)
    if not has_conv:
        return outs[0]
    return outs[0], outs[1][blocks_per_seq - 1::blocks_per_seq]


def matmul(a, w, n, *, tm, tn):
    m, k = a.shape
    return pl.pallas_call(
        _matmul_kernel,
        grid=(pl.cdiv(m, tm), pl.cdiv(n, tn)),
        in_specs=[pl.BlockSpec((tm, k), lambda i, j: (i, 0)),
                  pl.BlockSpec((None, k, tn), lambda i, j: (0, 0, j))],
        out_specs=pl.BlockSpec((tm, tn), lambda i, j: (i, j)),
        out_shape=jax.ShapeDtypeStruct((m, n), F32),
        scratch_shapes=[pltpu.VMEM((tm, k), BF16)],
        compiler_params=_params("parallel", "arbitrary"),
        name="matmul",
    )(a, w)


def _resident_w_kernel(a_ref, w_ref, o_ref):
    o_ref[...] = _bdot(a_ref[...], w_ref[...])


def qkv_proj(a, w, *, tm):
    m, k = a.shape
    n = w.shape[1] // 3

    def part(idx):
        return pl.pallas_call(
            _resident_w_kernel,
            grid=(pl.cdiv(m, tm),),
            in_specs=[pl.BlockSpec((tm, k), lambda i: (i, 0)),
                      pl.BlockSpec((k, n), lambda i: (0, idx), pipeline_mode=pl.Buffered(1))],
            out_specs=pl.BlockSpec((tm, n), lambda i: (i, 0)),
            out_shape=jax.ShapeDtypeStruct((m, n), F32),
            compiler_params=_params("parallel"),
            name="qkv_proj",
        )(a, w)

    return part(0), part(1), part(2)


LN_ROW_CHUNK = 256


def _matmul_ln_kernel(a_ref, w_ref, res_ref, g_ref, b_ref, o_ref):
    tm = o_ref.shape[0]
    chunk = min(tm, LN_ROW_CHUNK)
    for r0 in range(0, tm, chunk):
        rows = slice(r0, r0 + chunk)
        y = ALPHA * res_ref[rows, :] + _bdot(a_ref[rows, :], w_ref[...])
        o_ref[rows, :] = _layer_norm_rows(y, g_ref[...], b_ref[...])


def matmul_ln(a, w, res, g, b, *, tm):
    m, k = a.shape
    n = w.shape[1]
    return pl.pallas_call(
        _matmul_ln_kernel,
        grid=(pl.cdiv(m, tm),),
        in_specs=[pl.BlockSpec((tm, k), lambda i: (i, 0)),
                  pl.BlockSpec((k, n), lambda i: (0, 0), pipeline_mode=pl.Buffered(1)),
                  pl.BlockSpec((tm, n), lambda i: (i, 0)),
                  pl.BlockSpec((1, n), lambda i: (0, 0)),
                  pl.BlockSpec((1, n), lambda i: (0, 0))],
        out_specs=pl.BlockSpec((tm, n), lambda i: (i, 0)),
        out_shape=jax.ShapeDtypeStruct((m, n), F32),
        compiler_params=_params("parallel"),
        name="matmul_ln",
    )(a, w, res, g.reshape(1, n), b.reshape(1, n))


def _mlp_ln_body(x_ref, wu_ref, wd_ref, g_ref, b_ref, o_ref, xb_ref, shadow_work=None):
    f = pl.program_id(1)

    @pl.when(f == 0)
    def _():
        xb_ref[...] = x_ref[...].astype(BF16)
        o_ref[...] = jnp.zeros_like(o_ref)

    if shadow_work is not None:
        shadow_work()
    hdn = jnp.maximum(jnp.dot(xb_ref[...], wu_ref[...], preferred_element_type=F32), 0.0)
    o_ref[...] += _bdot(hdn * hdn, wd_ref[...])

    @pl.when(f == pl.num_programs(1) - 1)
    def _():
        y = ALPHA * x_ref[...] + o_ref[...]
        o_ref[...] = _layer_norm_rows(y, g_ref[...], b_ref[...])


def _mlp_ln_kernel(x_ref, wu_ref, wd_ref, g_ref, b_ref, o_ref, xb_ref):
    _mlp_ln_body(x_ref, wu_ref, wd_ref, g_ref, b_ref, o_ref, xb_ref)


def _mlp_ln_ksum_kernel(pt_ref, x_ref, wu_ref, wd_ref, g_ref, b_ref, ck_hbm, o_ref, ksum_ref,
                        xb_ref, kbuf, sem, *, seq0, nblk):
    step = pl.program_id(0) * pl.num_programs(1) + pl.program_id(1)
    n_steps = pl.num_programs(0) * pl.num_programs(1)
    pages_per_step = KSUM_BLOCKS_PER_STEP * PAGES_PER_BLOCK
    slot = step % 2

    def page_copies(at_step, at_slot):
        copies = []
        for pg in range(pages_per_step):
            flat = at_step * KSUM_BLOCKS_PER_STEP + pg // PAGES_PER_BLOCK
            page = pt_ref[seq0 + flat // nblk, (flat % nblk) * PAGES_PER_BLOCK + pg % PAGES_PER_BLOCK]
            copies.append(pltpu.make_async_copy(ck_hbm.at[0, page], kbuf.at[at_slot, pg], sem.at[at_slot, pg]))
        return copies

    @pl.when(step == 0)
    def _():
        for cp in page_copies(0, 0):
            cp.start()

    @pl.when(step + 1 < n_steps)
    def _():
        for cp in page_copies(step + 1, 1 - slot):
            cp.start()

    def sum_pages():
        for cp in page_copies(step, slot):
            cp.wait()
        for j in range(KSUM_BLOCKS_PER_STEP):
            acc = jnp.sum(kbuf[slot, j * PAGES_PER_BLOCK], axis=0)
            for p in range(1, PAGES_PER_BLOCK):
                acc = acc + jnp.sum(kbuf[slot, j * PAGES_PER_BLOCK + p], axis=0)
            ksum_ref[j] = acc

    _mlp_ln_body(x_ref, wu_ref, wd_ref, g_ref, b_ref, o_ref, xb_ref, sum_pages)


def mlp_ln(x, w_up, w_down, layer, g, b, *, tm, tf):
    m, d = x.shape
    ff = w_up.shape[2]
    return pl.pallas_call(
        _mlp_ln_kernel,
        grid=(pl.cdiv(m, tm), ff // tf),
        in_specs=[pl.BlockSpec((tm, d), lambda i, f: (i, 0)),
                  pl.BlockSpec((None, d, tf), lambda i, f: (layer, 0, f)),
                  pl.BlockSpec((None, tf, d), lambda i, f: (layer, f, 0)),
                  pl.BlockSpec((1, d), lambda i, f: (0, 0)),
                  pl.BlockSpec((1, d), lambda i, f: (0, 0))],
        out_specs=pl.BlockSpec((tm, d), lambda i, f: (i, 0)),
        out_shape=jax.ShapeDtypeStruct((m, d), F32),
        scratch_shapes=[pltpu.VMEM((tm, d), BF16)],
        compiler_params=_params("parallel", "arbitrary"),
        name="mlp_ln",
    )(x, w_up, w_down, g.reshape(1, d), b.reshape(1, d))


def mlp_ln_ksum(x, w_up, w_down, layer, g, b, cache_k, page_table, seq0, n_seq, *, tm, tf):
    m, d = x.shape
    ff = w_up.shape[2]
    grid = (m // tm, ff // tf)
    nblk = page_table.shape[1] // PAGES_PER_BLOCK
    assert n_seq * nblk == grid[0] * grid[1] * KSUM_BLOCKS_PER_STEP and nblk % KSUM_BLOCKS_PER_STEP == 0
    steps_per_seq = nblk // KSUM_BLOCKS_PER_STEP

    def ksum_index(i, f, pt):
        step = i * grid[1] + f
        return (step // steps_per_seq, step % steps_per_seq, 0, 0)

    return pl.pallas_call(
        functools.partial(_mlp_ln_ksum_kernel, seq0=seq0, nblk=nblk),
        grid_spec=pltpu.PrefetchScalarGridSpec(
            num_scalar_prefetch=1,
            grid=grid,
            in_specs=[pl.BlockSpec((tm, d), lambda i, f, pt: (i, 0)),
                      pl.BlockSpec((None, d, tf), lambda i, f, pt: (layer, 0, f)),
                      pl.BlockSpec((None, tf, d), lambda i, f, pt: (layer, f, 0)),
                      pl.BlockSpec((1, d), lambda i, f, pt: (0, 0)),
                      pl.BlockSpec((1, d), lambda i, f, pt: (0, 0)),
                      pl.BlockSpec(memory_space=pl.ANY)],
            out_specs=[pl.BlockSpec((tm, d), lambda i, f, pt: (i, 0)),
                       pl.BlockSpec((None, KSUM_BLOCKS_PER_STEP, N_HEADS, HEAD_DIM), ksum_index)],
            scratch_shapes=[pltpu.VMEM((tm, d), BF16),
                            pltpu.VMEM((2, KSUM_BLOCKS_PER_STEP * PAGES_PER_BLOCK, PAGE_SIZE, N_HEADS, HEAD_DIM), F32),
                            pltpu.SemaphoreType.DMA((2, KSUM_BLOCKS_PER_STEP * PAGES_PER_BLOCK))]),
        out_shape=[jax.ShapeDtypeStruct((m, d), F32),
                   jax.ShapeDtypeStruct((n_seq, nblk, N_HEADS, HEAD_DIM), F32)],
        compiler_params=_params("arbitrary", "arbitrary"),
        name="mlp_ln_ksum",
    )(page_table, x, w_up, w_down, g.reshape(1, d), b.reshape(1, d), cache_k)


def _mixer_prompt_kernel(u_ref, xbc_ref, gate_ref, z_ref, x_ref,
                         wdt_ref, wa_ref, wx_ref, ba_ref, bx_ref, lam_ref,
                         dtb_ref, alog_ref, dskip_ref, normw_ref,
                         ymix_ref, hlast_ref, state_ref,
                         a_scr, h_scr, y_scr, decay_rate_scr):
    t = SSD_CHUNK
    c = pl.program_id(1)

    @pl.when(c == 0)
    def _():
        hlast_ref[...] = jnp.zeros_like(hlast_ref)
        state_ref[...] = jnp.zeros_like(state_ref)
        decay_rate_scr[...] = -RG_C * _softplus(-lam_ref[...])

    row = lax.broadcasted_iota(jnp.int32, (t, RG_BW), 0)
    row_in_tile = row % SUBLANES
    is_start = jnp.logical_and(c == 0, row == 0)
    for hd in range(RG_HEADS):
        lo = hd * RG_BW
        u = u_ref[:, lo:lo + RG_BW]
        ub = u.astype(BF16)
        r = jax.nn.sigmoid(jnp.dot(ub, wa_ref[hd], preferred_element_type=F32) + ba_ref[:, lo:lo + RG_BW])
        i = jax.nn.sigmoid(jnp.dot(ub, wx_ref[hd], preferred_element_type=F32) + bx_ref[:, lo:lo + RG_BW])
        log_a = decay_rate_scr[:, lo:lo + RG_BW] * r
        a_raw = jnp.exp(log_a)
        a = jnp.where(is_start, 0.0, a_raw)
        mult = jnp.where(is_start, 1.0, _sqrt_one_minus_sq(log_a, a_raw))
        bt = mult * i * u
        for s in (1, 2, 4):
            keep = row_in_tile >= s
            a_sh = pltpu.roll(a, s, axis=0)
            b_sh = pltpu.roll(bt, s, axis=0)
            bt = jnp.where(keep, a * b_sh + bt, bt)
            a = jnp.where(keep, a * a_sh, a)
        a_scr[:, lo:lo + RG_BW] = a
        h_scr[:, lo:lo + RG_BW] = bt

    def carry_tile(gi, carry):
        r0 = pl.multiple_of(gi * SUBLANES, SUBLANES)
        h = a_scr[pl.ds(r0, SUBLANES), :] * carry + h_scr[pl.ds(r0, SUBLANES), :]
        h_scr[pl.ds(r0, SUBLANES), :] = h
        return h[SUBLANES - 1:SUBLANES, :]

    hlast_ref[...] = lax.fori_loop(0, t // SUBLANES, carry_tile, hlast_ref[...])
    ymix_ref[:, 0:D_RG] = (h_scr[...] * gate_ref[...]).astype(BF16)

    lane = lax.broadcasted_iota(jnp.int32, (t, LANES), 1)
    rows = lax.broadcasted_iota(jnp.int32, (t, LANES), 0)
    dt_raw = jnp.dot(x_ref[...].astype(BF16), wdt_ref[...], preferred_element_type=F32)
    dt = jnp.where(lane < SSD_HEADS, _softplus(dt_raw + dtb_ref[...]), 0.0)
    cs = dt * (-jnp.exp(alog_ref[...]))
    s = 1
    while s < t:
        cs = cs + jnp.where(rows >= s, pltpu.roll(cs, s, axis=0), 0.0)
        s *= 2
    cs_t = cs.T
    cs_last = cs[t - 1:t, :]
    decay_to_end = jnp.exp(cs_last - cs)
    exp_cs = jnp.exp(cs)
    chunk_decay = jnp.exp(cs_last)

    ii = lax.broadcasted_iota(jnp.int32, (t, t), 0)
    jj = lax.broadcasted_iota(jnp.int32, (t, t), 1)
    causal = jj <= ii
    first_head = lane < SSD_HEAD_DIM
    first_head_rows = rows < SSD_HEAD_DIM

    for g in range(SSD_GROUPS):
        b_lo = D_SSD + g * SSD_STATE
        bmat = xbc_ref[:, b_lo:b_lo + SSD_STATE].astype(BF16)
        cmat = xbc_ref[:, b_lo + SSD_GN:b_lo + SSD_GN + SSD_STATE].astype(BF16)
        cb = lax.dot_general(cmat, bmat, NT_DIMS, preferred_element_type=F32)
        for pair in range(SSD_HEADS // SSD_GROUPS // 2):
            h0 = g * (SSD_HEADS // SSD_GROUPS) + 2 * pair
            lo = h0 * SSD_HEAD_DIM

            def per_head(x2d):
                return jnp.where(first_head, x2d[:, h0:h0 + 1], x2d[:, h0 + 1:h0 + 2])

            xs = xbc_ref[:, lo:lo + LANES]
            xdt = xs * per_head(dt)
            xdt_b = xdt.astype(BF16)
            y_heads = []
            for hh in (h0, h0 + 1):
                seg = cs[:, hh:hh + 1] - cs_t[hh:hh + 1, :]
                lmat = jnp.exp(jnp.where(causal, seg, -jnp.inf))
                y_heads.append(jnp.dot((cb * lmat).astype(BF16), xdt_b, preferred_element_type=F32))
            y = jnp.where(first_head, y_heads[0], y_heads[1])
            s_prev = state_ref[lo:lo + LANES, :]
            y = y + lax.dot_general(cmat, s_prev.astype(BF16), NT_DIMS,
                                    preferred_element_type=F32) * per_head(exp_cs)
            y_scr[:, lo:lo + LANES] = y + dskip_ref[:, lo:lo + LANES] * xs
            decay_rows = jnp.where(first_head_rows, chunk_decay[:, h0:h0 + 1], chunk_decay[:, h0 + 1:h0 + 2])
            state_ref[lo:lo + LANES, :] = s_prev * decay_rows + lax.dot_general(
                (xdt * per_head(decay_to_end)).astype(BF16), bmat, TN_DIMS, preferred_element_type=F32)

    gw = D_SSD // SSD_GROUPS
    for g in range(SSD_GROUPS):
        lo = g * gw
        yz = y_scr[:, lo:lo + gw] * z_ref[:, lo:lo + gw]
        ms = jnp.mean(yz * yz, axis=-1, keepdims=True)
        ymix_ref[:, D_RG + lo:D_RG + lo + gw] = (yz * lax.rsqrt(ms + RMS_EPS)
                                                 * normw_ref[:, lo:lo + gw]).astype(BF16)


def _pad_lanes(v):
    return jnp.zeros((1, LANES), F32).at[0, :v.shape[0]].set(v)


def mixer_prompt(u, xbc, gate, z, x, nb, w_dt, wa, wx, ba, bx, lam, dt_bias, a_log, d_skip, norm_w):
    m = u.shape[0]
    l = m // nb
    t = SSD_CHUNK
    nc = l // t
    row_blk = lambda bi, ci: bi * nc + ci
    rows = lambda width: pl.BlockSpec((t, width), lambda bi, ci: (row_blk(bi, ci), 0))
    full = lambda shape: pl.BlockSpec(shape, lambda bi, ci: (0,) * len(shape))
    in_specs = [rows(D_RG), rows(D_XBC), rows(D_RG), rows(D_SSD), rows(D_MODEL)]
    in_specs += [full((D_MODEL, LANES)),
                 full((RG_HEADS, RG_BW, RG_BW)), full((RG_HEADS, RG_BW, RG_BW)),
                 full((1, D_RG)), full((1, D_RG)), full((1, D_RG)),
                 full((1, LANES)), full((1, LANES)), full((1, D_SSD)), full((1, D_SSD))]
    ymix, hlast, state = pl.pallas_call(
        _mixer_prompt_kernel,
        grid=(nb, nc),
        in_specs=in_specs,
        out_specs=[pl.BlockSpec((t, D_MIX), lambda bi, ci: (row_blk(bi, ci), 0)),
                   pl.BlockSpec((None, 1, D_RG), lambda bi, ci: (bi, 0, 0)),
                   pl.BlockSpec((None, D_SSD, SSD_STATE), lambda bi, ci: (bi, 0, 0))],
        out_shape=[jax.ShapeDtypeStruct((m, D_MIX), BF16),
                   jax.ShapeDtypeStruct((nb, 1, D_RG), F32),
                   jax.ShapeDtypeStruct((nb, D_SSD, SSD_STATE), F32)],
        scratch_shapes=[pltpu.VMEM((t, D_RG), F32), pltpu.VMEM((t, D_RG), F32),
                        pltpu.VMEM((t, D_SSD), F32), pltpu.VMEM((1, D_RG), F32)],
        compiler_params=_params("parallel", "arbitrary"),
        name="mixer_prompt",
    )(u, xbc, gate, z, x, w_dt, wa.astype(BF16), wx.astype(BF16),
      ba.reshape(1, D_RG), bx.reshape(1, D_RG), lam.reshape(1, D_RG),
      _pad_lanes(dt_bias), _pad_lanes(a_log),
      jnp.repeat(d_skip, SSD_HEAD_DIM).reshape(1, D_SSD), norm_w.reshape(1, D_SSD))
    return ymix, hlast.reshape(nb, D_RG), state


def _moba_prompt_kernel(slopes_ref, q_ref, k_ref, v_ref, o_ref, k_rows_hbm, v_rows_hbm, kb_ref, vt_ref, pen_ref, sem):
    blk = MOBA_BLOCK
    seq = k_ref.shape[0]
    nblk = seq // blk
    row_copies = [
        pltpu.make_async_copy(k_ref, k_rows_hbm.at[pl.program_id(0), :, pl.program_id(1), :], sem.at[0]),
        pltpu.make_async_copy(v_ref, v_rows_hbm.at[pl.program_id(0), :, pl.program_id(1), :], sem.at[1])]
    for cp in row_copies:
        cp.start()
    slope = slopes_ref[pl.program_id(1)] * LOG2_E
    scale = HEAD_DIM ** -0.5 * LOG2_E

    kmeans = []
    for n in range(nblk):
        rows = slice(n * blk, (n + 1) * blk)
        kn = k_ref[rows, :]
        kmeans.append(jnp.mean(kn, axis=0, keepdims=True))
        kb_ref[rows, :] = kn.astype(BF16)
        vt_ref[0:HEAD_DIM, rows] = v_ref[rows, :].T.astype(BF16)
    vt_ref[HEAD_DIM:, :] = jnp.ones((vt_ref.shape[0] - HEAD_DIM, seq), BF16)
    kmean = jnp.concatenate(kmeans, axis=0)
    pen_ref[...] = slope * (lax.broadcasted_iota(jnp.int32, (seq, blk), 1)
                            - lax.broadcasted_iota(jnp.int32, (seq, blk), 0)).astype(F32)
    blk_id = lax.broadcasted_iota(jnp.int32, (nblk, blk), 0)
    causal = (lax.broadcasted_iota(jnp.int32, (blk, blk), 0) <= lax.broadcasted_iota(jnp.int32, (blk, blk), 1))

    for qi in range(nblk):
        q = q_ref[qi * blk:(qi + 1) * blk, :]
        qb = q.astype(BF16)
        base = -slope * float(qi * blk)
        if qi > MOBA_TOPK:
            gate = lax.dot_general(kmean, q, NT_DIMS, precision=lax.Precision.HIGHEST,
                                   preferred_element_type=F32)
            gate = jnp.where(blk_id < qi, gate, -jnp.inf)
            ahead = jnp.zeros((nblk, blk), jnp.int32)
            for n in range(qi):
                gate_n = gate[n:n + 1, :]
                before = jnp.logical_or(gate_n > gate, jnp.logical_and(gate_n == gate, n < blk_id))
                ahead = ahead + before.astype(jnp.int32)
            row_terms = jnp.where(ahead < MOBA_TOPK, base, -jnp.inf)
        else:
            row_terms = jnp.full((nblk, blk), base, F32)

        s_blocks = []
        for n in range(qi + 1):
            rows = slice(n * blk, (n + 1) * blk)
            s = lax.dot_general(kb_ref[rows, :], qb, NT_DIMS, preferred_element_type=F32)
            s = s * scale - pen_ref[rows, :]
            if n < qi:
                s = s + row_terms[n:n + 1, :]
            else:
                s = jnp.where(causal, s + base, -jnp.inf)
            s_blocks.append(s)
        m = jnp.max(s_blocks[0], axis=0, keepdims=True)
        for s in s_blocks[1:]:
            m = jnp.maximum(m, jnp.max(s, axis=0, keepdims=True))
        acc = jnp.zeros((vt_ref.shape[0], blk), F32)
        for n, s in enumerate(s_blocks):
            p = jnp.exp2(s - m)
            acc = acc + jnp.dot(vt_ref[:, n * blk:(n + 1) * blk], p.astype(BF16), preferred_element_type=F32)
        denom = acc[HEAD_DIM:HEAD_DIM + 1, :]
        o_ref[qi * blk:(qi + 1) * blk, :] = (acc[0:HEAD_DIM, :] / denom).T.astype(o_ref.dtype)

    for cp in row_copies:
        cp.wait()


def alibi_slopes():
    return jnp.exp2(-8.0 * jnp.arange(1, N_HEADS + 1, dtype=F32) / N_HEADS)


def moba_prompt(q, k, v, nb):
    m = q.shape[0]
    l = m // nb
    head_spec = pl.BlockSpec((l, HEAD_DIM), lambda bi, hi: (bi, hi))
    rows_shape = jax.ShapeDtypeStruct((nb, l, N_HEADS, HEAD_DIM), F32)
    return pl.pallas_call(
        _moba_prompt_kernel,
        grid=(nb, N_HEADS),
        in_specs=[pl.BlockSpec(memory_space=pltpu.SMEM), head_spec, head_spec, head_spec],
        out_specs=[head_spec, pl.BlockSpec(memory_space=pl.ANY), pl.BlockSpec(memory_space=pl.ANY)],
        out_shape=[jax.ShapeDtypeStruct((m, D_MODEL), BF16), rows_shape, rows_shape],
        scratch_shapes=[pltpu.VMEM((l, HEAD_DIM), BF16), pltpu.VMEM((HEAD_DIM + BF16_SUBLANES, l), BF16),
                        pltpu.VMEM((l, MOBA_BLOCK), F32), pltpu.SemaphoreType.DMA((2,))],
        compiler_params=_params("parallel", "parallel"),
        name="moba_prompt",
    )(alibi_slopes(), q, k, v)


def _mixer_step_a_kernel(proj_ref, x_ref, b0_ref, b1_ref, b2_ref, h0_ref,
                         wdt_ref, convw_ref, convb_ref, wa_ref, wx_ref, ba_ref, bx_ref, lam_ref, dtb_ref,
                         yrg_ref, hnew_ref, xbc_ref, dt_ref):
    conv = (convb_ref[...] + convw_ref[0:1, :] * b0_ref[...] + convw_ref[1:2, :] * b1_ref[...]
            + convw_ref[2:3, :] * b2_ref[...] + convw_ref[3:4, :] * proj_ref[:, 0:D_CONV])
    xbc_ref[...] = _silu(conv[:, D_RG:])
    dt_raw = jnp.dot(x_ref[...].astype(BF16), wdt_ref[...], preferred_element_type=F32)
    dt_ref[...] = _softplus(dt_raw[:, 0:SSD_HEADS] + dtb_ref[...])
    for hd in range(RG_HEADS):
        lo = hd * RG_BW
        u = conv[:, lo:lo + RG_BW]
        ub = u.astype(BF16)
        r = jax.nn.sigmoid(jnp.dot(ub, wa_ref[hd], preferred_element_type=F32) + ba_ref[:, lo:lo + RG_BW])
        i = jax.nn.sigmoid(jnp.dot(ub, wx_ref[hd], preferred_element_type=F32) + bx_ref[:, lo:lo + RG_BW])
        log_a = -RG_C * r * _softplus(-lam_ref[:, lo:lo + RG_BW])
        a = jnp.exp(log_a)
        h = a * h0_ref[:, lo:lo + RG_BW] + _sqrt_one_minus_sq(log_a, a) * i * u
        hnew_ref[:, lo:lo + RG_BW] = h
        gate = proj_ref[:, D_CONV + lo:D_CONV + lo + RG_BW]
        yrg_ref[:, lo:lo + RG_BW] = h * jax.nn.gelu(gate)


def _ssd_step_kernel(s0_ref, xs_ref, dt_ref, alog_ref, b_ref, c_ref, snew_ref, y_ref):
    dt = dt_ref[...]
    decay = jnp.exp(dt * (-jnp.exp(alog_ref[...])))
    s_new = s0_ref[...] * decay + (xs_ref[...] * dt) * b_ref[...]
    snew_ref[...] = s_new
    y_ref[...] = jnp.sum(s_new * c_ref[...], axis=-1, keepdims=True)


def _mixer_step_b_kernel(yrg_ref, y_ref, xs_ref, z_ref, dskip_ref, normw_ref, o_ref):
    o_ref[:, 0:D_RG] = yrg_ref[...].astype(BF16)
    yz = (y_ref[...] + dskip_ref[...] * xs_ref[...]) * _silu(z_ref[...])
    gw = D_SSD // SSD_GROUPS
    for g in range(SSD_GROUPS):
        blk = yz[:, g * gw:(g + 1) * gw]
        ms = jnp.mean(blk * blk, axis=-1, keepdims=True)
        o_ref[:, D_RG + g * gw:D_RG + (g + 1) * gw] = (
            blk * lax.rsqrt(ms + RMS_EPS) * normw_ref[:, g * gw:(g + 1) * gw]).astype(BF16)


def _whole(shape):
    return pl.BlockSpec(shape, lambda *_: (0,) * len(shape))


def mixer_sample(proj, x, conv_buf, h0, s0, w_dt, conv_w, conv_b, wa, wx, ba, bx, lam, dt_bias, a_log, d_skip,
                 norm_w):
    nb = proj.shape[0]
    hpg = SSD_HEADS // SSD_GROUPS
    a_ins = (proj, x, conv_buf[:, 0], conv_buf[:, 1], conv_buf[:, 2], h0,
             w_dt, conv_w, conv_b.reshape(1, D_CONV), wa.astype(BF16), wx.astype(BF16),
             ba.reshape(1, D_RG), bx.reshape(1, D_RG), lam.reshape(1, D_RG), dt_bias.reshape(1, SSD_HEADS))
    yrg, hnew, xbc, dt = pl.pallas_call(
        _mixer_step_a_kernel,
        grid=(1,),
        in_specs=[_whole(a.shape) for a in a_ins],
        out_specs=[_whole((nb, D_RG)), _whole((nb, D_RG)), _whole((nb, D_XBC)), _whole((nb, SSD_HEADS))],
        out_shape=[jax.ShapeDtypeStruct((nb, D_RG), F32), jax.ShapeDtypeStruct((nb, D_RG), F32),
                   jax.ShapeDtypeStruct((nb, D_XBC), F32), jax.ShapeDtypeStruct((nb, SSD_HEADS), F32)],
        compiler_params=_params("arbitrary"),
        name="mixer_step_a",
    )(*a_ins)
    xs = xbc[:, :D_SSD]
    snew, y = pl.pallas_call(
        _ssd_step_kernel,
        grid=(nb, SSD_GROUPS),
        in_specs=[pl.BlockSpec((None, hpg, SSD_HEAD_DIM, SSD_STATE), lambda bi, gi: (bi, gi, 0, 0)),
                  pl.BlockSpec((None, hpg, SSD_HEAD_DIM, 1), lambda bi, gi: (bi, gi, 0, 0)),
                  pl.BlockSpec((None, hpg, 1, 1), lambda bi, gi: (bi, gi, 0, 0)),
                  pl.BlockSpec((hpg, 1, 1), lambda bi, gi: (gi, 0, 0)),
                  pl.BlockSpec((None, None, 1, SSD_STATE), lambda bi, gi: (bi, gi, 0, 0)),
                  pl.BlockSpec((None, None, 1, SSD_STATE), lambda bi, gi: (bi, gi, 0, 0))],
        out_specs=[pl.BlockSpec((None, hpg, SSD_HEAD_DIM, SSD_STATE), lambda bi, gi: (bi, gi, 0, 0)),
                   pl.BlockSpec((None, hpg, SSD_HEAD_DIM, 1), lambda bi, gi: (bi, gi, 0, 0))],
        out_shape=[jax.ShapeDtypeStruct((nb, SSD_HEADS, SSD_HEAD_DIM, SSD_STATE), F32),
                   jax.ShapeDtypeStruct((nb, SSD_HEADS, SSD_HEAD_DIM, 1), F32)],
        compiler_params=_params("parallel", "parallel"),
        name="ssd_step",
    )(s0, xs.reshape(nb, SSD_HEADS, SSD_HEAD_DIM, 1), dt.reshape(nb, SSD_HEADS, 1, 1),
      a_log.reshape(SSD_HEADS, 1, 1),
      xbc[:, D_SSD:D_SSD + SSD_GN].reshape(nb, SSD_GROUPS, 1, SSD_STATE),
      xbc[:, D_SSD + SSD_GN:].reshape(nb, SSD_GROUPS, 1, SSD_STATE))
    b_ins = (yrg, y.reshape(nb, D_SSD), xs, proj[:, D_CONV + D_RG:D_CONV + D_RG + D_SSD],
             jnp.repeat(d_skip, SSD_HEAD_DIM).reshape(1, D_SSD), norm_w.reshape(1, D_SSD))
    ymix = pl.pallas_call(
        _mixer_step_b_kernel,
        grid=(1,),
        in_specs=[_whole(a.shape) for a in b_ins],
        out_specs=_whole((nb, D_MIX)),
        out_shape=jax.ShapeDtypeStruct((nb, D_MIX), BF16),
        compiler_params=_params("arbitrary"),
        name="mixer_step_b",
    )(*b_ins)
    return ymix, hnew, snew


def _moba_topk_kernel(q_ref, ksum_ref, idx_ref):
    nblk = ksum_ref.shape[0]
    qh = q_ref[...] * (1.0 / MOBA_BLOCK)
    lane = lax.broadcasted_iota(jnp.int32, (N_HEADS, LANES), 1)
    gate = jnp.full((N_HEADS, LANES), -jnp.inf, F32)
    for n in range(nblk):
        gate = jnp.where(lane == n, jnp.sum(ksum_ref[n] * qh, axis=-1, keepdims=True), gate)
    out = jnp.zeros((N_HEADS, LANES), jnp.int32)
    for kk in range(MOBA_TOPK):
        best = jnp.max(gate, axis=-1, keepdims=True)
        pick = jnp.min(jnp.where(gate == best, lane, LANES), axis=-1, keepdims=True)
        out = jnp.where(lane == kk, pick, out)
        gate = jnp.where(lane == pick, -jnp.inf, gate)
    idx_ref[...] = out


def moba_topk(q, ksum):
    nb, nblk = ksum.shape[:2]
    assert MOBA_TOPK <= nblk <= LANES
    return pl.pallas_call(
        _moba_topk_kernel,
        grid=(nb,),
        in_specs=[pl.BlockSpec((None, N_HEADS, HEAD_DIM), lambda bi: (bi, 0, 0)),
                  pl.BlockSpec((None, nblk, N_HEADS, HEAD_DIM), lambda bi: (bi, 0, 0, 0))],
        out_specs=pl.BlockSpec((None, N_HEADS, LANES), lambda bi: (bi, 0, 0)),
        out_shape=jax.ShapeDtypeStruct((nb, N_HEADS, LANES), jnp.int32),
        compiler_params=_params("parallel"),
        name="moba_topk",
    )(q, ksum)


def _moba_decode_kernel(pt_ref, top_ref, slopes_ref, q_ref, kn_ref, vn_ref, ck_hbm, cv_hbm, o_ref,
                        kbuf, vbuf, sem, *, past_len):
    bi = pl.program_id(0)
    n_seq = pl.num_programs(0)
    buf = bi % 2
    scale = HEAD_DIM ** -0.5
    slots = MOBA_TOPK * PAGES_PER_BLOCK

    def head_copies(seq, at_buf, hd):
        copies = []
        for kk in range(MOBA_TOPK):
            blk_id = top_ref[seq, hd, kk]
            for half in range(PAGES_PER_BLOCK):
                page = pt_ref[seq, PAGES_PER_BLOCK * blk_id + half]
                slot = kk * PAGES_PER_BLOCK + half
                copies.append(pltpu.make_async_copy(ck_hbm.at[0, page, :, hd, :], kbuf.at[at_buf, hd, slot],
                                                    sem.at[at_buf, 0, hd, slot]))
                copies.append(pltpu.make_async_copy(cv_hbm.at[0, page, :, hd, :], vbuf.at[at_buf, hd, slot],
                                                    sem.at[at_buf, 1, hd, slot]))
        return copies

    @pl.when(bi == 0)
    def _():
        for hd in range(N_HEADS):
            for cp in head_copies(0, 0, hd):
                cp.start()

    @pl.when(bi + 1 < n_seq)
    def _():
        for hd in range(N_HEADS):
            for cp in head_copies(bi + 1, 1 - buf, hd):
                cp.start()

    key_off = lax.broadcasted_iota(jnp.int32, (1, MOBA_BLOCK), 1)
    for hd in range(N_HEADS):
        for cp in head_copies(bi, buf, hd):
            cp.wait()
        lo = hd * HEAD_DIM
        slope = slopes_ref[hd]
        q = q_ref[:, lo:lo + HEAD_DIM]
        q8 = jnp.broadcast_to(q, (SUBLANES, HEAD_DIM)).astype(BF16)
        kh = kbuf[buf, hd].reshape(slots * PAGE_SIZE, HEAD_DIM).astype(BF16)
        vh = vbuf[buf, hd].reshape(slots * PAGE_SIZE, HEAD_DIM).astype(BF16)
        dist = jnp.concatenate(
            [(past_len - top_ref[bi, hd, kk] * MOBA_BLOCK - key_off).astype(F32) for kk in range(MOBA_TOPK)],
            axis=1)
        s = lax.dot_general(q8, kh, NT_DIMS, preferred_element_type=F32)[0:1, :] * scale - slope * dist
        s_own = jnp.sum(q * kn_ref[:, lo:lo + HEAD_DIM], axis=-1, keepdims=True) * scale
        m = jnp.maximum(jnp.max(s, axis=-1, keepdims=True), s_own)
        p = jnp.exp(s - m)
        p_own = jnp.exp(s_own - m)
        denom = jnp.sum(p, axis=-1, keepdims=True) + p_own
        p8 = jnp.broadcast_to(p, (SUBLANES, slots * PAGE_SIZE)).astype(BF16)
        acc = jnp.dot(p8, vh, preferred_element_type=F32)[0:1, :] + p_own * vn_ref[:, lo:lo + HEAD_DIM]
        o_ref[:, lo:lo + HEAD_DIM] = acc / denom


def moba_decode(q, k_new, v_new, cache_k, cache_v, page_table, top_idx):
    nb, n_pages = page_table.shape
    slots = MOBA_TOPK * PAGES_PER_BLOCK
    row_spec = pl.BlockSpec((None, 1, D_MODEL), lambda bi, pt, top: (bi, 0, 0))
    return pl.pallas_call(
        functools.partial(_moba_decode_kernel, past_len=n_pages * PAGE_SIZE),
        grid_spec=pltpu.PrefetchScalarGridSpec(
            num_scalar_prefetch=2,
            grid=(nb,),
            in_specs=[pl.BlockSpec(memory_space=pltpu.SMEM), row_spec, row_spec, row_spec,
                      pl.BlockSpec(memory_space=pl.ANY), pl.BlockSpec(memory_space=pl.ANY)],
            out_specs=row_spec,
            scratch_shapes=[pltpu.VMEM((2, N_HEADS, slots, PAGE_SIZE, HEAD_DIM), F32),
                            pltpu.VMEM((2, N_HEADS, slots, PAGE_SIZE, HEAD_DIM), F32),
                            pltpu.SemaphoreType.DMA((2, 2, N_HEADS, slots))]),
        out_shape=jax.ShapeDtypeStruct((nb, 1, D_MODEL), F32),
        compiler_params=_params("arbitrary"),
        name="moba_decode",
    )(page_table, top_idx, alibi_slopes(), q, k_new, v_new, cache_k, cache_v)


def _trunk_prompt(x, w, nb, cache_k, page_table):
    tm = ROW_TILE
    n_seq = page_table.shape[0]
    half = n_seq // DEPTH
    conv_w, conv_b, *mixer_w = w["mixer"]
    proj_tiles = dict(tm=IN_PROJ_ROW_TILE, tn=COL_TILE)
    cw, cb = conv_w, conv_b.reshape(1, D_CONV)
    u, tail_u = in_proj(x, w["w_in"], 0, D_RG, "conv", nb, cw, cb, **proj_tiles)
    xbc, tail_xbc = in_proj(x, w["w_in"], D_RG, D_XBC, "conv_silu", nb, cw, cb, **proj_tiles)
    gate = in_proj(x, w["w_in"], D_CONV, D_RG, "gelu", nb, **proj_tiles)
    z = in_proj(x, w["w_in"], D_CONV + D_RG, D_SSD, "silu", nb, **proj_tiles)
    conv_tail = jnp.concatenate([tail_u, tail_xbc], axis=2)[:, SUBLANES - (CONV_W - 1):, :]
    ymix, h_last, s_last = mixer_prompt(u, xbc, gate, z, x, nb, w["w_dt"], *mixer_w)
    x = matmul_ln(ymix, w["w_out"], x, w["ln1_g"][0], w["ln1_b"][0], tm=tm)
    x, ksum0 = mlp_ln_ksum(x, w["w_up"], w["w_down"], 0, w["ln2_g"][0], w["ln2_b"][0],
                           cache_k, page_table, 0, half, tm=tm, tf=HIDDEN_TILE)
    q, k, v = qkv_proj(x, w["w_qkv"], tm=tm)
    o, k, v = moba_prompt(q, k, v, nb)
    x = matmul_ln(o, w["w_o"], x, w["ln1_g"][1], w["ln1_b"][1], tm=tm)
    x, ksum1 = mlp_ln_ksum(x, w["w_up"], w["w_down"], 1, w["ln2_g"][1], w["ln2_b"][1],
                           cache_k, page_table, half, n_seq - half, tm=tm, tf=HIDDEN_TILE)
    return x, conv_tail, h_last, s_last, k, v, jnp.concatenate([ksum0, ksum1], axis=0)


def _trunk_sample(x, conv_buf, h0, s0, cache_k, cache_v, page_table, ksum, w):
    nb = x.shape[0]
    proj = matmul(x, w["w_in"], D_IN - SSD_HEADS, tm=nb, tn=COL_TILE)
    new_buf = jnp.concatenate([conv_buf[:, 1:], proj[:, None, :D_CONV]], axis=1)
    ymix, h_new, s_new = mixer_sample(proj, x, conv_buf, h0, s0, w["w_dt"], *w["mixer"])
    x = matmul_ln(ymix, w["w_out"], x, w["ln1_g"][0], w["ln1_b"][0], tm=nb)
    x = mlp_ln(x, w["w_up"], w["w_down"], 0, w["ln2_g"][0], w["ln2_b"][0], tm=nb, tf=HIDDEN_TILE)
    q, k, v = qkv_proj(x, w["w_qkv"], tm=nb)
    top_idx = moba_topk(q.reshape(nb, N_HEADS, HEAD_DIM), ksum)[:, :, :MOBA_TOPK]
    o = moba_decode(q.reshape(nb, 1, D_MODEL), k.reshape(nb, 1, D_MODEL), v.reshape(nb, 1, D_MODEL),
                    cache_k, cache_v, page_table, top_idx).reshape(nb, D_MODEL)
    x = matmul_ln(o, w["w_o"], x, w["ln1_g"][1], w["ln1_b"][1], tm=nb)
    x = mlp_ln(x, w["w_up"], w["w_down"], 1, w["ln2_g"][1], w["ln2_b"][1], tm=nb, tf=HIDDEN_TILE)
    return x, new_buf, h_new, s_new, k, v


def kernel(x_prompt, x_sample, state_conv, state_rglru, state_ssd, cache_k, cache_v, page_table, w_in, conv_w, conv_b, rg_wa, rg_ba, rg_wx, rg_bx, rg_lambda, ssd_dt_bias, ssd_a_log, ssd_d, ssd_norm_w, w_out, w_qkv, w_o, ln1_g, ln1_b, ln2_g, ln2_b, w_up, w_down):
    bp, lp, _ = x_prompt.shape
    bs = x_sample.shape[0]
    w = {
        "w_in": w_in.astype(BF16),
        "w_dt": jnp.pad(w_in[0][:, D_IN - SSD_HEADS:], ((0, 0), (0, LANES - SSD_HEADS))).astype(BF16),
        "w_out": w_out[0].astype(BF16), "w_qkv": w_qkv[0].astype(BF16), "w_o": w_o[0].astype(BF16),
        "w_up": w_up.astype(BF16), "w_down": w_down.astype(BF16),
        "ln1_g": ln1_g, "ln1_b": ln1_b, "ln2_g": ln2_g, "ln2_b": ln2_b,
        "mixer": (conv_w[0], conv_b[0], rg_wa[0], rg_wx[0], rg_ba[0], rg_bx[0], rg_lambda[0],
                  ssd_dt_bias[0], ssd_a_log[0], ssd_d[0], ssd_norm_w[0]),
    }
    yp, conv_p, h_p, s_p, k_p, v_p, ksum = _trunk_prompt(x_prompt.reshape(bp * lp, D_MODEL), w, bp,
                                                         cache_k, page_table)
    ys, conv_s, h_s, s_s, k_s, v_s = _trunk_sample(
        x_sample.reshape(bs, D_MODEL), state_conv[0], state_rglru[0], state_ssd[0],
        cache_k, cache_v, page_table, ksum, w)
    return (yp.reshape(bp, lp, D_MODEL), ys.reshape(bs, 1, D_MODEL),
            conv_p[None], h_p[None], s_p.reshape(1, bp, SSD_HEADS, SSD_HEAD_DIM, SSD_STATE),
            k_p[None], v_p[None],
            conv_s[None], h_s[None], s_s[None],
            k_s.reshape(1, bs, 1, N_HEADS, HEAD_DIM), v_s.reshape(1, bs, 1, N_HEADS, HEAD_DIM))
```

```python
import functools
import math

import jax
import jax.numpy as jnp
from jax import lax
from jax.experimental import pallas as pl
from jax.experimental.pallas import tpu as pltpu

F32 = jnp.float32
BF16 = jnp.bfloat16

D_MODEL = 2048
DEPTH = 2
PAGE_SIZE = 128
CONV_W = 4
D_RG = D_MODEL
RG_HEADS = 8
RG_BW = D_RG // RG_HEADS
RG_C = 8.0
D_SSD = D_MODEL
SSD_HEAD_DIM = 64
SSD_HEADS = D_SSD // SSD_HEAD_DIM
SSD_GROUPS = 4
SSD_STATE = 128
SSD_CHUNK = 128
SSD_GN = SSD_GROUPS * SSD_STATE
D_XBC = D_SSD + 2 * SSD_GN
D_CONV = D_RG + D_XBC
D_IN = D_CONV + D_RG + D_SSD + SSD_HEADS
D_MIX = D_RG + D_SSD
N_HEADS = 16
HEAD_DIM = D_MODEL // N_HEADS
MOBA_BLOCK = 256
MOBA_TOPK = 3
PAGES_PER_BLOCK = MOBA_BLOCK // PAGE_SIZE
KSUM_BLOCKS_PER_STEP = 2
D_FF = 4 * D_MODEL
ALPHA = (2 * DEPTH) ** 0.25
LN_EPS = 1e-5
RMS_EPS = 1e-6

SUBLANES = 8
BF16_SUBLANES = 16
LANES = 128
LOG2_E = 1.4426950408889634
VMEM_LIMIT_BYTES = 56 * 1024 * 1024

ROW_TILE = 512
IN_PROJ_ROW_TILE = 1024
COL_TILE = 1024
HIDDEN_TILE = 1024

NT_DIMS = (((1,), (1,)), ((), ()))
TN_DIMS = (((0,), (0,)), ((), ()))


def _params(*semantics):
    return pltpu.CompilerParams(dimension_semantics=semantics, vmem_limit_bytes=VMEM_LIMIT_BYTES)


def _bdot(a, b):
    return jnp.dot(a.astype(BF16), b.astype(BF16), preferred_element_type=F32)


def _layer_norm_rows(y, g, b):
    mu = jnp.mean(y, axis=-1, keepdims=True)
    yc = y - mu
    var = jnp.mean(yc * yc, axis=-1, keepdims=True)
    return yc * lax.rsqrt(var + LN_EPS) * g + b


def _softplus(x):
    u = jnp.exp(-jnp.abs(x))
    w = 1.0 + u
    log1p_u = jnp.where(w == 1.0, u, jnp.log(w) * (u / (w - 1.0)))
    return jnp.maximum(x, 0.0) + log1p_u


def _sqrt_one_minus_sq(log_a, a):
    return jnp.sqrt(-jnp.tanh(log_a) * (a * a + 1.0))


def _silu(x):
    return x * jax.nn.sigmoid(x)


def _matmul_kernel(a_ref, w_ref, o_ref, ab_ref):
    @pl.when(pl.program_id(1) == 0)
    def _():
        ab_ref[...] = a_ref[...].astype(BF16)

    o_ref[...] = jnp.dot(ab_ref[...], w_ref[...], preferred_element_type=F32)


IN_PROJ_COL_CHUNK = 256
IN_PROJ_ROW_CHUNK = 64


def _in_proj_kernel(*refs, kind, blocks_per_seq):
    has_conv = kind in ("conv", "conv_silu")
    if has_conv:
        x_ref, w_ref, cw_ref, cb_ref, o_ref, tail_ref, xb_ref, ybuf, carry = refs
    else:
        x_ref, w_ref, o_ref, xb_ref = refs
    i = pl.program_id(0)
    j = pl.program_id(1)
    tm, tn = o_ref.shape
    front = SUBLANES

    @pl.when(j == 0)
    def _():
        xb_ref[...] = x_ref[...].astype(BF16)

    if has_conv:
        @pl.when(i % blocks_per_seq == 0)
        def _():
            ybuf[0:front, :] = jnp.zeros((front, tn), F32)

        @pl.when(i % blocks_per_seq != 0)
        def _():
            ybuf[0:front, :] = carry[j]

    for c0 in range(0, tn, IN_PROJ_COL_CHUNK):
        cols = slice(c0, c0 + IN_PROJ_COL_CHUNK)
        y = jnp.dot(xb_ref[...], w_ref[:, cols], preferred_element_type=F32)
        if has_conv:
            ybuf[front:front + tm, cols] = y
        for r0 in range(0, tm, IN_PROJ_ROW_CHUNK):
            rows = slice(r0, r0 + IN_PROJ_ROW_CHUNK)
            if has_conv:
                ext = ybuf[r0:r0 + front + IN_PROJ_ROW_CHUNK, cols]
                acc = cb_ref[:, cols] + cw_ref[CONV_W - 1:CONV_W, cols] * ext[front:, :]
                for k in range(CONV_W - 1):
                    shifted = pltpu.roll(ext, CONV_W - 1 - k, axis=0)[front:, :]
                    acc = acc + cw_ref[k:k + 1, cols] * shifted
                o_ref[rows, cols] = _silu(acc) if kind == "conv_silu" else acc
            elif kind == "gelu":
                o_ref[rows, cols] = jax.nn.gelu(y[rows, :])
            else:
                o_ref[rows, cols] = _silu(y[rows, :])

    if has_conv:
        last = ybuf[tm:tm + front, :]
        carry[j] = last
        tail_ref[...] = last


def in_proj(x, w, col0, n_out, kind, nb, conv_w=None, conv_b=None, *, tm, tn):
    m, k = x.shape
    off = col0 // tn
    grid = (m // tm, n_out // tn)
    blocks_per_seq = (m // nb) // tm
    has_conv = kind in ("conv", "conv_silu")
    in_specs = [pl.BlockSpec((tm, k), lambda i, j: (i, 0)),
                pl.BlockSpec((None, k, tn), lambda i, j: (0, 0, j + off))]
    out_specs = [pl.BlockSpec((tm, tn), lambda i, j: (i, j))]
    out_shape = [jax.ShapeDtypeStruct((m, n_out), F32)]
    scratch = [pltpu.VMEM((tm, k), BF16)]
    operands = [x, w]
    if has_conv:
        in_specs += [pl.BlockSpec((CONV_W, tn), lambda i, j: (0, j + off)),
                     pl.BlockSpec((1, tn), lambda i, j: (0, j + off))]
        out_specs += [pl.BlockSpec((None, SUBLANES, tn), lambda i, j: (i, 0, j))]
        out_shape += [jax.ShapeDtypeStruct((grid[0], SUBLANES, n_out), F32)]
        scratch += [pltpu.VMEM((tm + SUBLANES, tn), F32), pltpu.VMEM((grid[1], SUBLANES, tn), F32)]
        operands += [conv_w, conv_b]
    outs = pl.pallas_call(
        functools.partial(_in_proj_kernel, kind=kind, blocks_per_seq=blocks_per_seq),
        grid=grid,
        in_specs=in_specs,
        out_specs=out_specs,
        out_shape=out_shape,
        scratch_shapes=scratch,
        compiler_params=_params("arbitrary", "arbitrary"),
        name="in_proj_" + kind,
    )(*operands)
    if not has_conv:
        return outs[0]
    return outs[0], outs[1][blocks_per_seq - 1::blocks_per_seq]


def matmul(a, w, n, *, tm, tn):
    m, k = a.shape
    return pl.pallas_call(
        _matmul_kernel,
        grid=(pl.cdiv(m, tm), pl.cdiv(n, tn)),
        in_specs=[pl.BlockSpec((tm, k), lambda i, j: (i, 0)),
                  pl.BlockSpec((None, k, tn), lambda i, j: (0, 0, j))],
        out_specs=pl.BlockSpec((tm, tn), lambda i, j: (i, j)),
        out_shape=jax.ShapeDtypeStruct((m, n), F32),
        scratch_shapes=[pltpu.VMEM((tm, k), BF16)],
        compiler_params=_params("parallel", "arbitrary"),
        name="matmul",
    )(a, w)


def _resident_w_kernel(a_ref, w_ref, o_ref):
    o_ref[...] = _bdot(a_ref[...], w_ref[...])


def qkv_proj(a, w, *, tm):
    m, k = a.shape
    n = w.shape[1] // 3

    def part(idx):
        return pl.pallas_call(
            _resident_w_kernel,
            grid=(pl.cdiv(m, tm),),
            in_specs=[pl.BlockSpec((tm, k), lambda i: (i, 0)),
                      pl.BlockSpec((k, n), lambda i: (0, idx), pipeline_mode=pl.Buffered(1))],
            out_specs=pl.BlockSpec((tm, n), lambda i: (i, 0)),
            out_shape=jax.ShapeDtypeStruct((m, n), F32),
            compiler_params=_params("parallel"),
            name="qkv_proj",
        )(a, w)

    return part(0), part(1), part(2)


LN_ROW_CHUNK = 256


def _matmul_ln_kernel(a_ref, w_ref, res_ref, g_ref, b_ref, o_ref):
    tm = o_ref.shape[0]
    chunk = min(tm, LN_ROW_CHUNK)
    for r0 in range(0, tm, chunk):
        rows = slice(r0, r0 + chunk)
        y = ALPHA * res_ref[rows, :] + _bdot(a_ref[rows, :], w_ref[...])
        o_ref[rows, :] = _layer_norm_rows(y, g_ref[...], b_ref[...])


def matmul_ln(a, w, res, g, b, *, tm):
    m, k = a.shape
    n = w.shape[1]
    return pl.pallas_call(
        _matmul_ln_kernel,
        grid=(pl.cdiv(m, tm),),
        in_specs=[pl.BlockSpec((tm, k), lambda i: (i, 0)),
                  pl.BlockSpec((k, n), lambda i: (0, 0), pipeline_mode=pl.Buffered(1)),
                  pl.BlockSpec((tm, n), lambda i: (i, 0)),
                  pl.BlockSpec((1, n), lambda i: (0, 0)),
                  pl.BlockSpec((1, n), lambda i: (0, 0))],
        out_specs=pl.BlockSpec((tm, n), lambda i: (i, 0)),
        out_shape=jax.ShapeDtypeStruct((m, n), F32),
        compiler_params=_params("parallel"),
        name="matmul_ln",
    )(a, w, res, g.reshape(1, n), b.reshape(1, n))


def _mlp_ln_body(x_ref, wu_ref, wd_ref, g_ref, b_ref, o_ref, xb_ref, shadow_work=None):
    f = pl.program_id(1)

    @pl.when(f == 0)
    def _():
        xb_ref[...] = x_ref[...].astype(BF16)
        o_ref[...] = jnp.zeros_like(o_ref)

    if shadow_work is not None:
        shadow_work()
    hdn = jnp.maximum(jnp.dot(xb_ref[...], wu_ref[...], preferred_element_type=F32), 0.0)
    o_ref[...] += _bdot(hdn * hdn, wd_ref[...])

    @pl.when(f == pl.num_programs(1) - 1)
    def _():
        y = ALPHA * x_ref[...] + o_ref[...]
        o_ref[...] = _layer_norm_rows(y, g_ref[...], b_ref[...])


def _mlp_ln_kernel(x_ref, wu_ref, wd_ref, g_ref, b_ref, o_ref, xb_ref):
    _mlp_ln_body(x_ref, wu_ref, wd_ref, g_ref, b_ref, o_ref, xb_ref)


def _mlp_ln_ksum_kernel(pt_ref, x_ref, wu_ref, wd_ref, g_ref, b_ref, ck_hbm, o_ref, ksum_ref,
                        xb_ref, kbuf, sem, *, seq0, nblk):
    step = pl.program_id(0) * pl.num_programs(1) + pl.program_id(1)
    n_steps = pl.num_programs(0) * pl.num_programs(1)
    pages_per_step = KSUM_BLOCKS_PER_STEP * PAGES_PER_BLOCK
    slot = step % 2

    def page_copies(at_step, at_slot):
        copies = []
        for pg in range(pages_per_step):
            flat = at_step * KSUM_BLOCKS_PER_STEP + pg // PAGES_PER_BLOCK
            page = pt_ref[seq0 + flat // nblk, (flat % nblk) * PAGES_PER_BLOCK + pg % PAGES_PER_BLOCK]
            copies.append(pltpu.make_async_copy(ck_hbm.at[0, page], kbuf.at[at_slot, pg], sem.at[at_slot, pg]))
        return copies

    @pl.when(step == 0)
    def _():
        for cp in page_copies(0, 0):
            cp.start()

    @pl.when(step + 1 < n_steps)
    def _():
        for cp in page_copies(step + 1, 1 - slot):
            cp.start()

    def sum_pages():
        for cp in page_copies(step, slot):
            cp.wait()
        for j in range(KSUM_BLOCKS_PER_STEP):
            acc = jnp.sum(kbuf[slot, j * PAGES_PER_BLOCK], axis=0)
            for p in range(1, PAGES_PER_BLOCK):
                acc = acc + jnp.sum(kbuf[slot, j * PAGES_PER_BLOCK + p], axis=0)
            ksum_ref[j] = acc

    _mlp_ln_body(x_ref, wu_ref, wd_ref, g_ref, b_ref, o_ref, xb_ref, sum_pages)


def mlp_ln(x, w_up, w_down, layer, g, b, *, tm, tf):
    m, d = x.shape
    ff = w_up.shape[2]
    return pl.pallas_call(
        _mlp_ln_kernel,
        grid=(pl.cdiv(m, tm), ff // tf),
        in_specs=[pl.BlockSpec((tm, d), lambda i, f: (i, 0)),
                  pl.BlockSpec((None, d, tf), lambda i, f: (layer, 0, f)),
                  pl.BlockSpec((None, tf, d), lambda i, f: (layer, f, 0)),
                  pl.BlockSpec((1, d), lambda i, f: (0, 0)),
                  pl.BlockSpec((1, d), lambda i, f: (0, 0))],
        out_specs=pl.BlockSpec((tm, d), lambda i, f: (i, 0)),
        out_shape=jax.ShapeDtypeStruct((m, d), F32),
        scratch_shapes=[pltpu.VMEM((tm, d), BF16)],
        compiler_params=_params("parallel", "arbitrary"),
        name="mlp_ln",
    )(x, w_up, w_down, g.reshape(1, d), b.reshape(1, d))


def mlp_ln_ksum(x, w_up, w_down, layer, g, b, cache_k, page_table, seq0, n_seq, *, tm, tf):
    m, d = x.shape
    ff = w_up.shape[2]
    grid = (m // tm, ff // tf)
    nblk = page_table.shape[1] // PAGES_PER_BLOCK
    assert n_seq * nblk == grid[0] * grid[1] * KSUM_BLOCKS_PER_STEP and nblk % KSUM_BLOCKS_PER_STEP == 0
    steps_per_seq = nblk // KSUM_BLOCKS_PER_STEP

    def ksum_index(i, f, pt):
        step = i * grid[1] + f
        return (step // steps_per_seq, step % steps_per_seq, 0, 0)

    return pl.pallas_call(
        functools.partial(_mlp_ln_ksum_kernel, seq0=seq0, nblk=nblk),
        grid_spec=pltpu.PrefetchScalarGridSpec(
            num_scalar_prefetch=1,
            grid=grid,
            in_specs=[pl.BlockSpec((tm, d), lambda i, f, pt: (i, 0)),
                      pl.BlockSpec((None, d, tf), lambda i, f, pt: (layer, 0, f)),
                      pl.BlockSpec((None, tf, d), lambda i, f, pt: (layer, f, 0)),
                      pl.BlockSpec((1, d), lambda i, f, pt: (0, 0)),
                      pl.BlockSpec((1, d), lambda i, f, pt: (0, 0)),
                      pl.BlockSpec(memory_space=pl.ANY)],
            out_specs=[pl.BlockSpec((tm, d), lambda i, f, pt: (i, 0)),
                       pl.BlockSpec((None, KSUM_BLOCKS_PER_STEP, N_HEADS, HEAD_DIM), ksum_index)],
            scratch_shapes=[pltpu.VMEM((tm, d), BF16),
                            pltpu.VMEM((2, KSUM_BLOCKS_PER_STEP * PAGES_PER_BLOCK, PAGE_SIZE, N_HEADS, HEAD_DIM), F32),
                            pltpu.SemaphoreType.DMA((2, KSUM_BLOCKS_PER_STEP * PAGES_PER_BLOCK))]),
        out_shape=[jax.ShapeDtypeStruct((m, d), F32),
                   jax.ShapeDtypeStruct((n_seq, nblk, N_HEADS, HEAD_DIM), F32)],
        compiler_params=_params("arbitrary", "arbitrary"),
        name="mlp_ln_ksum",
    )(page_table, x, w_up, w_down, g.reshape(1, d), b.reshape(1, d), cache_k)


def _mixer_prompt_kernel(u_ref, xbc_ref, gate_ref, z_ref, x_ref,
                         wdt_ref, wa_ref, wx_ref, ba_ref, bx_ref, lam_ref,
                         dtb_ref, alog_ref, dskip_ref, normw_ref,
                         ymix_ref, hlast_ref, state_ref,
                         a_scr, h_scr, y_scr, decay_rate_scr):
    t = SSD_CHUNK
    c = pl.program_id(1)

    @pl.when(c == 0)
    def _():
        hlast_ref[...] = jnp.zeros_like(hlast_ref)
        state_ref[...] = jnp.zeros_like(state_ref)
        decay_rate_scr[...] = -RG_C * _softplus(-lam_ref[...])

    row = lax.broadcasted_iota(jnp.int32, (t, RG_BW), 0)
    row_in_tile = row % SUBLANES
    is_start = jnp.logical_and(c == 0, row == 0)
    for hd in range(RG_HEADS):
        lo = hd * RG_BW
        u = u_ref[:, lo:lo + RG_BW]
        ub = u.astype(BF16)
        r = jax.nn.sigmoid(jnp.dot(ub, wa_ref[hd], preferred_element_type=F32) + ba_ref[:, lo:lo + RG_BW])
        i = jax.nn.sigmoid(jnp.dot(ub, wx_ref[hd], preferred_element_type=F32) + bx_ref[:, lo:lo + RG_BW])
        log_a = decay_rate_scr[:, lo:lo + RG_BW] * r
        a_raw = jnp.exp(log_a)
        a = jnp.where(is_start, 0.0, a_raw)
        mult = jnp.where(is_start, 1.0, _sqrt_one_minus_sq(log_a, a_raw))
        bt = mult * i * u
        for s in (1, 2, 4):
            keep = row_in_tile >= s
            a_sh = pltpu.roll(a, s, axis=0)
            b_sh = pltpu.roll(bt, s, axis=0)
            bt = jnp.where(keep, a * b_sh + bt, bt)
            a = jnp.where(keep, a * a_sh, a)
        a_scr[:, lo:lo + RG_BW] = a
        h_scr[:, lo:lo + RG_BW] = bt

    def carry_tile(gi, carry):
        r0 = pl.multiple_of(gi * SUBLANES, SUBLANES)
        h = a_scr[pl.ds(r0, SUBLANES), :] * carry + h_scr[pl.ds(r0, SUBLANES), :]
        h_scr[pl.ds(r0, SUBLANES), :] = h
        return h[SUBLANES - 1:SUBLANES, :]

    hlast_ref[...] = lax.fori_loop(0, t // SUBLANES, carry_tile, hlast_ref[...])
    ymix_ref[:, 0:D_RG] = (h_scr[...] * gate_ref[...]).astype(BF16)

    lane = lax.broadcasted_iota(jnp.int32, (t, LANES), 1)
    rows = lax.broadcasted_iota(jnp.int32, (t, LANES), 0)
    dt_raw = jnp.dot(x_ref[...].astype(BF16), wdt_ref[...], preferred_element_type=F32)
    dt = jnp.where(lane < SSD_HEADS, _softplus(dt_raw + dtb_ref[...]), 0.0)
    cs = dt * (-jnp.exp(alog_ref[...]))
    s = 1
    while s < t:
        cs = cs + jnp.where(rows >= s, pltpu.roll(cs, s, axis=0), 0.0)
        s *= 2
    cs_t = cs.T
    cs_last = cs[t - 1:t, :]
    decay_to_end = jnp.exp(cs_last - cs)
    exp_cs = jnp.exp(cs)
    chunk_decay = jnp.exp(cs_last)

    ii = lax.broadcasted_iota(jnp.int32, (t, t), 0)
    jj = lax.broadcasted_iota(jnp.int32, (t, t), 1)
    causal = jj <= ii
    first_head = lane < SSD_HEAD_DIM
    first_head_rows = rows < SSD_HEAD_DIM

    for g in range(SSD_GROUPS):
        b_lo = D_SSD + g * SSD_STATE
        bmat = xbc_ref[:, b_lo:b_lo + SSD_STATE].astype(BF16)
        cmat = xbc_ref[:, b_lo + SSD_GN:b_lo + SSD_GN + SSD_STATE].astype(BF16)
        cb = lax.dot_general(cmat, bmat, NT_DIMS, preferred_element_type=F32)
        for pair in range(SSD_HEADS // SSD_GROUPS // 2):
            h0 = g * (SSD_HEADS // SSD_GROUPS) + 2 * pair
            lo = h0 * SSD_HEAD_DIM

            def per_head(x2d):
                return jnp.where(first_head, x2d[:, h0:h0 + 1], x2d[:, h0 + 1:h0 + 2])

            xs = xbc_ref[:, lo:lo + LANES]
            xdt = xs * per_head(dt)
            xdt_b = xdt.astype(BF16)
            y_heads = []
            for hh in (h0, h0 + 1):
                seg = cs[:, hh:hh + 1] - cs_t[hh:hh + 1, :]
                lmat = jnp.exp(jnp.where(causal, seg, -jnp.inf))
                y_heads.append(jnp.dot((cb * lmat).astype(BF16), xdt_b, preferred_element_type=F32))
            y = jnp.where(first_head, y_heads[0], y_heads[1])
            s_prev = state_ref[lo:lo + LANES, :]
            y = y + lax.dot_general(cmat, s_prev.astype(BF16), NT_DIMS,
                                    preferred_element_type=F32) * per_head(exp_cs)
            y_scr[:, lo:lo + LANES] = y + dskip_ref[:, lo:lo + LANES] * xs
            decay_rows = jnp.where(first_head_rows, chunk_decay[:, h0:h0 + 1], chunk_decay[:, h0 + 1:h0 + 2])
            state_ref[lo:lo + LANES, :] = s_prev * decay_rows + lax.dot_general(
                (xdt * per_head(decay_to_end)).astype(BF16), bmat, TN_DIMS, preferred_element_type=F32)

    gw = D_SSD // SSD_GROUPS
    for g in range(SSD_GROUPS):
        lo = g * gw
        yz = y_scr[:, lo:lo + gw] * z_ref[:, lo:lo + gw]
        ms = jnp.mean(yz * yz, axis=-1, keepdims=True)
        ymix_ref[:, D_RG + lo:D_RG + lo + gw] = (yz * lax.rsqrt(ms + RMS_EPS)
                                                 * normw_ref[:, lo:lo + gw]).astype(BF16)


def _pad_lanes(v):
    return jnp.zeros((1, LANES), F32).at[0, :v.shape[0]].set(v)


def mixer_prompt(u, xbc, gate, z, x, nb, w_dt, wa, wx, ba, bx, lam, dt_bias, a_log, d_skip, norm_w):
    m = u.shape[0]
    l = m // nb
    t = SSD_CHUNK
    nc = l // t
    row_blk = lambda bi, ci: bi * nc + ci
    rows = lambda width: pl.BlockSpec((t, width), lambda bi, ci: (row_blk(bi, ci), 0))
    full = lambda shape: pl.BlockSpec(shape, lambda bi, ci: (0,) * len(shape))
    in_specs = [rows(D_RG), rows(D_XBC), rows(D_RG), rows(D_SSD), rows(D_MODEL)]
    in_specs += [full((D_MODEL, LANES)),
                 full((RG_HEADS, RG_BW, RG_BW)), full((RG_HEADS, RG_BW, RG_BW)),
                 full((1, D_RG)), full((1, D_RG)), full((1, D_RG)),
                 full((1, LANES)), full((1, LANES)), full((1, D_SSD)), full((1, D_SSD))]
    ymix, hlast, state = pl.pallas_call(
        _mixer_prompt_kernel,
        grid=(nb, nc),
        in_specs=in_specs,
        out_specs=[pl.BlockSpec((t, D_MIX), lambda bi, ci: (row_blk(bi, ci), 0)),
                   pl.BlockSpec((None, 1, D_RG), lambda bi, ci: (bi, 0, 0)),
                   pl.BlockSpec((None, D_SSD, SSD_STATE), lambda bi, ci: (bi, 0, 0))],
        out_shape=[jax.ShapeDtypeStruct((m, D_MIX), BF16),
                   jax.ShapeDtypeStruct((nb, 1, D_RG), F32),
                   jax.ShapeDtypeStruct((nb, D_SSD, SSD_STATE), F32)],
        scratch_shapes=[pltpu.VMEM((t, D_RG), F32), pltpu.VMEM((t, D_RG), F32),
                        pltpu.VMEM((t, D_SSD), F32), pltpu.VMEM((1, D_RG), F32)],
        compiler_params=_params("parallel", "arbitrary"),
        name="mixer_prompt",
    )(u, xbc, gate, z, x, w_dt, wa.astype(BF16), wx.astype(BF16),
      ba.reshape(1, D_RG), bx.reshape(1, D_RG), lam.reshape(1, D_RG),
      _pad_lanes(dt_bias), _pad_lanes(a_log),
      jnp.repeat(d_skip, SSD_HEAD_DIM).reshape(1, D_SSD), norm_w.reshape(1, D_SSD))
    return ymix, hlast.reshape(nb, D_RG), state


def _moba_prompt_kernel(slopes_ref, q_ref, k_ref, v_ref, o_ref, k_rows_hbm, v_rows_hbm, kb_ref, vt_ref, pen_ref, sem):
    blk = MOBA_BLOCK
    seq = k_ref.shape[0]
    nblk = seq // blk
    row_copies = [
        pltpu.make_async_copy(k_ref, k_rows_hbm.at[pl.program_id(0), :, pl.program_id(1), :], sem.at[0]),
        pltpu.make_async_copy(v_ref, v_rows_hbm.at[pl.program_id(0), :, pl.program_id(1), :], sem.at[1])]
    for thread, cp in enumerate(row_copies):
        cp.start(priority=thread)
    slope = slopes_ref[pl.program_id(1)] * LOG2_E
    scale = HEAD_DIM ** -0.5 * LOG2_E

    kmeans = []
    for n in range(nblk):
        rows = slice(n * blk, (n + 1) * blk)
        kn = k_ref[rows, :]
        kmeans.append(jnp.mean(kn, axis=0, keepdims=True))
        kb_ref[rows, :] = kn.astype(BF16)
        vt_ref[0:HEAD_DIM, rows] = v_ref[rows, :].T.astype(BF16)
    vt_ref[HEAD_DIM:, :] = jnp.ones((vt_ref.shape[0] - HEAD_DIM, seq), BF16)
    kmean = jnp.concatenate(kmeans, axis=0)
    pen_ref[...] = slope * (lax.broadcasted_iota(jnp.int32, (seq, blk), 1)
                            - lax.broadcasted_iota(jnp.int32, (seq, blk), 0)).astype(F32)
    blk_id = lax.broadcasted_iota(jnp.int32, (nblk, blk), 0)
    causal = (lax.broadcasted_iota(jnp.int32, (blk, blk), 0) <= lax.broadcasted_iota(jnp.int32, (blk, blk), 1))

    for qi in range(nblk):
        q = q_ref[qi * blk:(qi + 1) * blk, :]
        qb = q.astype(BF16)
        base = -slope * float(qi * blk)
        if qi > MOBA_TOPK:
            gate = lax.dot_general(kmean, q, NT_DIMS, precision=lax.Precision.HIGHEST,
                                   preferred_element_type=F32)
            gate = jnp.where(blk_id < qi, gate, -jnp.inf)
            ahead = jnp.zeros((nblk, blk), jnp.int32)
            for n in range(qi):
                gate_n = gate[n:n + 1, :]
                before = jnp.logical_or(gate_n > gate, jnp.logical_and(gate_n == gate, n < blk_id))
                ahead = ahead + before.astype(jnp.int32)
            row_terms = jnp.where(ahead < MOBA_TOPK, base, -jnp.inf)
        else:
            row_terms = jnp.full((nblk, blk), base, F32)

        s_blocks = []
        for n in range(qi + 1):
            rows = slice(n * blk, (n + 1) * blk)
            s = lax.dot_general(kb_ref[rows, :], qb, NT_DIMS, preferred_element_type=F32)
            s = s * scale - pen_ref[rows, :]
            if n < qi:
                s = s + row_terms[n:n + 1, :]
            else:
                s = jnp.where(causal, s + base, -jnp.inf)
            s_blocks.append(s)
        m = jnp.max(s_blocks[0], axis=0, keepdims=True)
        for s in s_blocks[1:]:
            m = jnp.maximum(m, jnp.max(s, axis=0, keepdims=True))
        acc = jnp.zeros((vt_ref.shape[0], blk), F32)
        for n, s in enumerate(s_blocks):
            p = jnp.exp2(s - m)
            acc = acc + jnp.dot(vt_ref[:, n * blk:(n + 1) * blk], p.astype(BF16), preferred_element_type=F32)
        denom = acc[HEAD_DIM:HEAD_DIM + 1, :]
        o_ref[qi * blk:(qi + 1) * blk, :] = (acc[0:HEAD_DIM, :] / denom).T.astype(o_ref.dtype)

    for cp in row_copies:
        cp.wait()


def alibi_slopes():
    return jnp.exp2(-8.0 * jnp.arange(1, N_HEADS + 1, dtype=F32) / N_HEADS)


def moba_prompt(q, k, v, nb):
    m = q.shape[0]
    l = m // nb
    head_spec = pl.BlockSpec((l, HEAD_DIM), lambda bi, hi: (bi, hi))
    rows_shape = jax.ShapeDtypeStruct((nb, l, N_HEADS, HEAD_DIM), F32)
    return pl.pallas_call(
        _moba_prompt_kernel,
        grid=(nb, N_HEADS),
        in_specs=[pl.BlockSpec(memory_space=pltpu.SMEM), head_spec, head_spec, head_spec],
        out_specs=[head_spec, pl.BlockSpec(memory_space=pl.ANY), pl.BlockSpec(memory_space=pl.ANY)],
        out_shape=[jax.ShapeDtypeStruct((m, D_MODEL), BF16), rows_shape, rows_shape],
        scratch_shapes=[pltpu.VMEM((l, HEAD_DIM), BF16), pltpu.VMEM((HEAD_DIM + BF16_SUBLANES, l), BF16),
                        pltpu.VMEM((l, MOBA_BLOCK), F32), pltpu.SemaphoreType.DMA((2,))],
        compiler_params=_params("parallel", "parallel"),
        name="moba_prompt",
    )(alibi_slopes(), q, k, v)


def _mixer_step_a_kernel(proj_ref, x_ref, b0_ref, b1_ref, b2_ref, h0_ref,
                         wdt_ref, convw_ref, convb_ref, wa_ref, wx_ref, ba_ref, bx_ref, lam_ref, dtb_ref,
                         yrg_ref, hnew_ref, xbc_ref, dt_ref):
    conv = (convb_ref[...] + convw_ref[0:1, :] * b0_ref[...] + convw_ref[1:2, :] * b1_ref[...]
            + convw_ref[2:3, :] * b2_ref[...] + convw_ref[3:4, :] * proj_ref[:, 0:D_CONV])
    xbc_ref[...] = _silu(conv[:, D_RG:])
    dt_raw = jnp.dot(x_ref[...].astype(BF16), wdt_ref[...], preferred_element_type=F32)
    dt_ref[...] = _softplus(dt_raw[:, 0:SSD_HEADS] + dtb_ref[...])
    for hd in range(RG_HEADS):
        lo = hd * RG_BW
        u = conv[:, lo:lo + RG_BW]
        ub = u.astype(BF16)
        r = jax.nn.sigmoid(jnp.dot(ub, wa_ref[hd], preferred_element_type=F32) + ba_ref[:, lo:lo + RG_BW])
        i = jax.nn.sigmoid(jnp.dot(ub, wx_ref[hd], preferred_element_type=F32) + bx_ref[:, lo:lo + RG_BW])
        log_a = -RG_C * r * _softplus(-lam_ref[:, lo:lo + RG_BW])
        a = jnp.exp(log_a)
        h = a * h0_ref[:, lo:lo + RG_BW] + _sqrt_one_minus_sq(log_a, a) * i * u
        hnew_ref[:, lo:lo + RG_BW] = h
        gate = proj_ref[:, D_CONV + lo:D_CONV + lo + RG_BW]
        yrg_ref[:, lo:lo + RG_BW] = h * jax.nn.gelu(gate)


def _ssd_step_kernel(s0_ref, xs_ref, dt_ref, alog_ref, b_ref, c_ref, snew_ref, y_ref):
    dt = dt_ref[...]
    decay = jnp.exp(dt * (-jnp.exp(alog_ref[...])))
    s_new = s0_ref[...] * decay + (xs_ref[...] * dt) * b_ref[...]
    snew_ref[...] = s_new
    y_ref[...] = jnp.sum(s_new * c_ref[...], axis=-1, keepdims=True)


def _mixer_step_b_kernel(yrg_ref, y_ref, xs_ref, z_ref, dskip_ref, normw_ref, o_ref):
    o_ref[:, 0:D_RG] = yrg_ref[...].astype(BF16)
    yz = (y_ref[...] + dskip_ref[...] * xs_ref[...]) * _silu(z_ref[...])
    gw = D_SSD // SSD_GROUPS
    for g in range(SSD_GROUPS):
        blk = yz[:, g * gw:(g + 1) * gw]
        ms = jnp.mean(blk * blk, axis=-1, keepdims=True)
        o_ref[:, D_RG + g * gw:D_RG + (g + 1) * gw] = (
            blk * lax.rsqrt(ms + RMS_EPS) * normw_ref[:, g * gw:(g + 1) * gw]).astype(BF16)


def _whole(shape):
    return pl.BlockSpec(shape, lambda *_: (0,) * len(shape))


def mixer_sample(proj, x, conv_buf, h0, s0, w_dt, conv_w, conv_b, wa, wx, ba, bx, lam, dt_bias, a_log, d_skip,
                 norm_w):
    nb = proj.shape[0]
    hpg = SSD_HEADS // SSD_GROUPS
    a_ins = (proj, x, conv_buf[:, 0], conv_buf[:, 1], conv_buf[:, 2], h0,
             w_dt, conv_w, conv_b.reshape(1, D_CONV), wa.astype(BF16), wx.astype(BF16),
             ba.reshape(1, D_RG), bx.reshape(1, D_RG), lam.reshape(1, D_RG), dt_bias.reshape(1, SSD_HEADS))
    yrg, hnew, xbc, dt = pl.pallas_call(
        _mixer_step_a_kernel,
        grid=(1,),
        in_specs=[_whole(a.shape) for a in a_ins],
        out_specs=[_whole((nb, D_RG)), _whole((nb, D_RG)), _whole((nb, D_XBC)), _whole((nb, SSD_HEADS))],
        out_shape=[jax.ShapeDtypeStruct((nb, D_RG), F32), jax.ShapeDtypeStruct((nb, D_RG), F32),
                   jax.ShapeDtypeStruct((nb, D_XBC), F32), jax.ShapeDtypeStruct((nb, SSD_HEADS), F32)],
        compiler_params=_params("arbitrary"),
        name="mixer_step_a",
    )(*a_ins)
    xs = xbc[:, :D_SSD]
    snew, y = pl.pallas_call(
        _ssd_step_kernel,
        grid=(nb, SSD_GROUPS),
        in_specs=[pl.BlockSpec((None, hpg, SSD_HEAD_DIM, SSD_STATE), lambda bi, gi: (bi, gi, 0, 0)),
                  pl.BlockSpec((None, hpg, SSD_HEAD_DIM, 1), lambda bi, gi: (bi, gi, 0, 0)),
                  pl.BlockSpec((None, hpg, 1, 1), lambda bi, gi: (bi, gi, 0, 0)),
                  pl.BlockSpec((hpg, 1, 1), lambda bi, gi: (gi, 0, 0)),
                  pl.BlockSpec((None, None, 1, SSD_STATE), lambda bi, gi: (bi, gi, 0, 0)),
                  pl.BlockSpec((None, None, 1, SSD_STATE), lambda bi, gi: (bi, gi, 0, 0))],
        out_specs=[pl.BlockSpec((None, hpg, SSD_HEAD_DIM, SSD_STATE), lambda bi, gi: (bi, gi, 0, 0)),
                   pl.BlockSpec((None, hpg, SSD_HEAD_DIM, 1), lambda bi, gi: (bi, gi, 0, 0))],
        out_shape=[jax.ShapeDtypeStruct((nb, SSD_HEADS, SSD_HEAD_DIM, SSD_STATE), F32),
                   jax.ShapeDtypeStruct((nb, SSD_HEADS, SSD_HEAD_DIM, 1), F32)],
        compiler_params=_params("parallel", "parallel"),
        name="ssd_step",
    )(s0, xs.reshape(nb, SSD_HEADS, SSD_HEAD_DIM, 1), dt.reshape(nb, SSD_HEADS, 1, 1),
      a_log.reshape(SSD_HEADS, 1, 1),
      xbc[:, D_SSD:D_SSD + SSD_GN].reshape(nb, SSD_GROUPS, 1, SSD_STATE),
      xbc[:, D_SSD + SSD_GN:].reshape(nb, SSD_GROUPS, 1, SSD_STATE))
    b_ins = (yrg, y.reshape(nb, D_SSD), xs, proj[:, D_CONV + D_RG:D_CONV + D_RG + D_SSD],
             jnp.repeat(d_skip, SSD_HEAD_DIM).reshape(1, D_SSD), norm_w.reshape(1, D_SSD))
    ymix = pl.pallas_call(
        _mixer_step_b_kernel,
        grid=(1,),
        in_specs=[_whole(a.shape) for a in b_ins],
        out_specs=_whole((nb, D_MIX)),
        out_shape=jax.ShapeDtypeStruct((nb, D_MIX), BF16),
        compiler_params=_params("arbitrary"),
        name="mixer_step_b",
    )(*b_ins)
    return ymix, hnew, snew


def _moba_topk_kernel(q_ref, ksum_ref, idx_ref):
    nblk = ksum_ref.shape[0]
    qh = q_ref[...] * (1.0 / MOBA_BLOCK)
    lane = lax.broadcasted_iota(jnp.int32, (N_HEADS, LANES), 1)
    gate = jnp.full((N_HEADS, LANES), -jnp.inf, F32)
    for n in range(nblk):
        gate = jnp.where(lane == n, jnp.sum(ksum_ref[n] * qh, axis=-1, keepdims=True), gate)
    out = jnp.zeros((N_HEADS, LANES), jnp.int32)
    for kk in range(MOBA_TOPK):
        best = jnp.max(gate, axis=-1, keepdims=True)
        pick = jnp.min(jnp.where(gate == best, lane, LANES), axis=-1, keepdims=True)
        out = jnp.where(lane == kk, pick, out)
        gate = jnp.where(lane == pick, -jnp.inf, gate)
    idx_ref[...] = out


def moba_topk(q, ksum):
    nb, nblk = ksum.shape[:2]
    assert MOBA_TOPK <= nblk <= LANES
    return pl.pallas_call(
        _moba_topk_kernel,
        grid=(nb,),
        in_specs=[pl.BlockSpec((None, N_HEADS, HEAD_DIM), lambda bi: (bi, 0, 0)),
                  pl.BlockSpec((None, nblk, N_HEADS, HEAD_DIM), lambda bi: (bi, 0, 0, 0))],
        out_specs=pl.BlockSpec((None, N_HEADS, LANES), lambda bi: (bi, 0, 0)),
        out_shape=jax.ShapeDtypeStruct((nb, N_HEADS, LANES), jnp.int32),
        compiler_params=_params("parallel"),
        name="moba_topk",
    )(q, ksum)


def _moba_decode_kernel(pt_ref, top_ref, slopes_ref, q_ref, kn_ref, vn_ref, ck_hbm, cv_hbm, o_ref,
                        kbuf, vbuf, sem, *, past_len):
    bi = pl.program_id(0)
    n_seq = pl.num_programs(0)
    buf = bi % 2
    scale = HEAD_DIM ** -0.5
    slots = MOBA_TOPK * PAGES_PER_BLOCK

    def head_copies(seq, at_buf, hd):
        copies = []
        for kk in range(MOBA_TOPK):
            blk_id = top_ref[seq, hd, kk]
            for half in range(PAGES_PER_BLOCK):
                page = pt_ref[seq, PAGES_PER_BLOCK * blk_id + half]
                slot = kk * PAGES_PER_BLOCK + half
                copies.append(pltpu.make_async_copy(ck_hbm.at[0, page, :, hd, :], kbuf.at[at_buf, hd, slot],
                                                    sem.at[at_buf, 0, hd, slot]))
                copies.append(pltpu.make_async_copy(cv_hbm.at[0, page, :, hd, :], vbuf.at[at_buf, hd, slot],
                                                    sem.at[at_buf, 1, hd, slot]))
        return copies

    @pl.when(bi == 0)
    def _():
        for hd in range(N_HEADS):
            for cp in head_copies(0, 0, hd):
                cp.start()

    @pl.when(bi + 1 < n_seq)
    def _():
        for hd in range(N_HEADS):
            for cp in head_copies(bi + 1, 1 - buf, hd):
                cp.start()

    key_off = lax.broadcasted_iota(jnp.int32, (1, MOBA_BLOCK), 1)
    for hd in range(N_HEADS):
        for cp in head_copies(bi, buf, hd):
            cp.wait()
        lo = hd * HEAD_DIM
        slope = slopes_ref[hd]
        q = q_ref[:, lo:lo + HEAD_DIM]
        q8 = jnp.broadcast_to(q, (SUBLANES, HEAD_DIM)).astype(BF16)
        kh = kbuf[buf, hd].reshape(slots * PAGE_SIZE, HEAD_DIM).astype(BF16)
        vh = vbuf[buf, hd].reshape(slots * PAGE_SIZE, HEAD_DIM).astype(BF16)
        dist = jnp.concatenate(
            [(past_len - top_ref[bi, hd, kk] * MOBA_BLOCK - key_off).astype(F32) for kk in range(MOBA_TOPK)],
            axis=1)
        s = lax.dot_general(q8, kh, NT_DIMS, preferred_element_type=F32)[0:1, :] * scale - slope * dist
        s_own = jnp.sum(q * kn_ref[:, lo:lo + HEAD_DIM], axis=-1, keepdims=True) * scale
        m = jnp.maximum(jnp.max(s, axis=-1, keepdims=True), s_own)
        p = jnp.exp(s - m)
        p_own = jnp.exp(s_own - m)
        denom = jnp.sum(p, axis=-1, keepdims=True) + p_own
        p8 = jnp.broadcast_to(p, (SUBLANES, slots * PAGE_SIZE)).astype(BF16)
        acc = jnp.dot(p8, vh, preferred_element_type=F32)[0:1, :] + p_own * vn_ref[:, lo:lo + HEAD_DIM]
        o_ref[:, lo:lo + HEAD_DIM] = acc / denom


def moba_decode(q, k_new, v_new, cache_k, cache_v, page_table, top_idx):
    nb, n_pages = page_table.shape
    slots = MOBA_TOPK * PAGES_PER_BLOCK
    row_spec = pl.BlockSpec((None, 1, D_MODEL), lambda bi, pt, top: (bi, 0, 0))
    return pl.pallas_call(
        functools.partial(_moba_decode_kernel, past_len=n_pages * PAGE_SIZE),
        grid_spec=pltpu.PrefetchScalarGridSpec(
            num_scalar_prefetch=2,
            grid=(nb,),
            in_specs=[pl.BlockSpec(memory_space=pltpu.SMEM), row_spec, row_spec, row_spec,
                      pl.BlockSpec(memory_space=pl.ANY), pl.BlockSpec(memory_space=pl.ANY)],
            out_specs=row_spec,
            scratch_shapes=[pltpu.VMEM((2, N_HEADS, slots, PAGE_SIZE, HEAD_DIM), F32),
                            pltpu.VMEM((2, N_HEADS, slots, PAGE_SIZE, HEAD_DIM), F32),
                            pltpu.SemaphoreType.DMA((2, 2, N_HEADS, slots))]),
        out_shape=jax.ShapeDtypeStruct((nb, 1, D_MODEL), F32),
        compiler_params=_params("arbitrary"),
        name="moba_decode",
    )(page_table, top_idx, alibi_slopes(), q, k_new, v_new, cache_k, cache_v)


def _trunk_prompt(x, w, nb, cache_k, page_table):
    tm = ROW_TILE
    n_seq = page_table.shape[0]
    half = n_seq // DEPTH
    conv_w, conv_b, *mixer_w = w["mixer"]
    proj_tiles = dict(tm=IN_PROJ_ROW_TILE, tn=COL_TILE)
    cw, cb = conv_w, conv_b.reshape(1, D_CONV)
    u, tail_u = in_proj(x, w["w_in"], 0, D_RG, "conv", nb, cw, cb, **proj_tiles)
    xbc, tail_xbc = in_proj(x, w["w_in"], D_RG, D_XBC, "conv_silu", nb, cw, cb, **proj_tiles)
    gate = in_proj(x, w["w_in"], D_CONV, D_RG, "gelu", nb, **proj_tiles)
    z = in_proj(x, w["w_in"], D_CONV + D_RG, D_SSD, "silu", nb, **proj_tiles)
    conv_tail = jnp.concatenate([tail_u, tail_xbc], axis=2)[:, SUBLANES - (CONV_W - 1):, :]
    ymix, h_last, s_last = mixer_prompt(u, xbc, gate, z, x, nb, w["w_dt"], *mixer_w)
    x = matmul_ln(ymix, w["w_out"], x, w["ln1_g"][0], w["ln1_b"][0], tm=tm)
    x, ksum0 = mlp_ln_ksum(x, w["w_up"], w["w_down"], 0, w["ln2_g"][0], w["ln2_b"][0],
                           cache_k, page_table, 0, half, tm=tm, tf=HIDDEN_TILE)
    q, k, v = qkv_proj(x, w["w_qkv"], tm=tm)
    o, k, v = moba_prompt(q, k, v, nb)
    x = matmul_ln(o, w["w_o"], x, w["ln1_g"][1], w["ln1_b"][1], tm=tm)
    x, ksum1 = mlp_ln_ksum(x, w["w_up"], w["w_down"], 1, w["ln2_g"][1], w["ln2_b"][1],
                           cache_k, page_table, half, n_seq - half, tm=tm, tf=HIDDEN_TILE)
    return x, conv_tail, h_last, s_last, k, v, jnp.concatenate([ksum0, ksum1], axis=0)


def _trunk_sample(x, conv_buf, h0, s0, cache_k, cache_v, page_table, ksum, w):
    nb = x.shape[0]
    proj = matmul(x, w["w_in"], D_IN - SSD_HEADS, tm=nb, tn=COL_TILE)
    new_buf = jnp.concatenate([conv_buf[:, 1:], proj[:, None, :D_CONV]], axis=1)
    ymix, h_new, s_new = mixer_sample(proj, x, conv_buf, h0, s0, w["w_dt"], *w["mixer"])
    x = matmul_ln(ymix, w["w_out"], x, w["ln1_g"][0], w["ln1_b"][0], tm=nb)
    x = mlp_ln(x, w["w_up"], w["w_down"], 0, w["ln2_g"][0], w["ln2_b"][0], tm=nb, tf=HIDDEN_TILE)
    q, k, v = qkv_proj(x, w["w_qkv"], tm=nb)
    top_idx = moba_topk(q.reshape(nb, N_HEADS, HEAD_DIM), ksum)[:, :, :MOBA_TOPK]
    o = moba_decode(q.reshape(nb, 1, D_MODEL), k.reshape(nb, 1, D_MODEL), v.reshape(nb, 1, D_MODEL),
                    cache_k, cache_v, page_table, top_idx).reshape(nb, D_MODEL)
    x = matmul_ln(o, w["w_o"], x, w["ln1_g"][1], w["ln1_b"][1], tm=nb)
    x = mlp_ln(x, w["w_up"], w["w_down"], 1, w["ln2_g"][1], w["ln2_b"][1], tm=nb, tf=HIDDEN_TILE)
    return x, new_buf, h_new, s_new, k, v


def kernel(x_prompt, x_sample, state_conv, state_rglru, state_ssd, cache_k, cache_v, page_table, w_in, conv_w, conv_b, rg_wa, rg_ba, rg_wx, rg_bx, rg_lambda, ssd_dt_bias, ssd_a_log, ssd_d, ssd_norm_w, w_out, w_qkv, w_o, ln1_g, ln1_b, ln2_g, ln2_b, w_up, w_down):
    bp, lp, _ = x_prompt.shape
    bs = x_sample.shape[0]
    w = {
        "w_in": w_in.astype(BF16),
        "w_dt": jnp.pad(w_in[0][:, D_IN - SSD_HEADS:], ((0, 0), (0, LANES - SSD_HEADS))).astype(BF16),
        "w_out": w_out[0].astype(BF16), "w_qkv": w_qkv[0].astype(BF16), "w_o": w_o[0].astype(BF16),
        "w_up": w_up.astype(BF16), "w_down": w_down.astype(BF16),
        "ln1_g": ln1_g, "ln1_b": ln1_b, "ln2_g": ln2_g, "ln2_b": ln2_b,
        "mixer": (conv_w[0], conv_b[0], rg_wa[0], rg_wx[0], rg_ba[0], rg_bx[0], rg_lambda[0],
                  ssd_dt_bias[0], ssd_a_log[0], ssd_d[0], ssd_norm_w[0]),
    }
    yp, conv_p, h_p, s_p, k_p, v_p, ksum = _trunk_prompt(x_prompt.reshape(bp * lp, D_MODEL), w, bp,
                                                         cache_k, page_table)
    ys, conv_s, h_s, s_s, k_s, v_s = _trunk_sample(
        x_sample.reshape(bs, D_MODEL), state_conv[0], state_rglru[0], state_ssd[0],
        cache_k, cache_v, page_table, ksum, w)
    return (yp.reshape(bp, lp, D_MODEL), ys.reshape(bs, 1, D_MODEL),
            conv_p[None], h_p[None], s_p.reshape(1, bp, SSD_HEADS, SSD_HEAD_DIM, SSD_STATE),
            k_p[None], v_p[None],
            conv_s[None], h_s[None], s_s[None],
            k_s.reshape(1, bs, 1, N_HEADS, HEAD_DIM), v_s.reshape(1, bs, 1, N_HEADS, HEAD_DIM))
```
